```python
import jax, jax.numpy as jnp
from jax import lax
import numpy as np

D_MODEL = 2048
BATCH = 8
SEQ = 4096
DEPTH = 4

HEAD_DIM = 128
D_MIX = D_MODEL
BLOCK = 128
EPS = 1e-6

MLA_HEADS = 6
MLA_Q_LORA = 512
MLA_KV_LORA = 256
MLA_NOPE = 128
MLA_ROPE = 64
MLA_V = HEAD_DIM
ROPE_THETA = 10000.0

SB_HEADS = 4

DIL_HEADS = 6
DIL_PATTERNS = ((128, 1), (512, 4), (2048, 16))
ALIBI_MAX_EXP = 8.0

N_HEADS_TOTAL = MLA_HEADS + SB_HEADS + DIL_HEADS
SB_W = SB_HEADS * HEAD_DIM
DIL_W = DIL_HEADS * HEAD_DIM

D_FF = 5632

MLA_IN = MLA_Q_LORA + MLA_KV_LORA + MLA_ROPE
IN_SPLITS = (
    MLA_Q_LORA,
    MLA_Q_LORA + MLA_KV_LORA,
    MLA_IN,
    MLA_IN + SB_W,
    MLA_IN + 2 * SB_W,
    MLA_IN + 3 * SB_W,
    MLA_IN + 3 * SB_W + DIL_W,
    MLA_IN + 3 * SB_W + 2 * DIL_W,
)
N_IN = MLA_IN + 3 * SB_W + 3 * DIL_W

kernel_name = "hybrid_mla_stickbreak_dilated_macaron"


def rmsnorm(x, g):
    xf = x.astype(jnp.float32)
    y = xf * lax.rsqrt(jnp.mean(xf * xf, axis=-1, keepdims=True) + EPS)
    return (y * g.astype(jnp.float32)).astype(x.dtype)


def swiglu(x, w_gu, w_down):
    g, u = jnp.split(x @ w_gu, 2, axis=-1)
    return (jax.nn.silu(g) * u) @ w_down


def rope_tables(s):
    inv = ROPE_THETA ** (-jnp.arange(0, MLA_ROPE, 2, dtype=jnp.float32) / MLA_ROPE)
    ang = jnp.arange(s, dtype=jnp.float32)[:, None] * inv[None, :]
    return jnp.cos(ang), jnp.sin(ang)


def apply_rope(x, cos, sin):
    xf = x.astype(jnp.float32)
    x1, x2 = jnp.split(xf, 2, axis=-1)
    out = jnp.concatenate([x1 * cos - x2 * sin, x2 * cos + x1 * sin], axis=-1)
    return out.astype(x.dtype)


def to_blocks(x):
    b, s = x.shape[:2]
    return jnp.moveaxis(x.reshape(b, s // BLOCK, BLOCK, *x.shape[2:]), 1, 0)


def from_blocks(x):
    x = jnp.moveaxis(x, 0, 1)
    return x.reshape(x.shape[0], -1, *x.shape[3:])


def causal_softmax_attention(q, k, v, scale):
    s_len = k.shape[1]
    kpos = jnp.arange(s_len)

    def block(args):
        qb, i = args
        qpos = i * BLOCK + jnp.arange(BLOCK)
        sc = jnp.einsum('bqhd,bkhd->bhqk', qb, k).astype(jnp.float32) * scale
        sc = jnp.where(kpos[None, :] <= qpos[:, None], sc, -jnp.inf)
        p = jax.nn.softmax(sc, axis=-1).astype(v.dtype)
        return jnp.einsum('bhqk,bkhd->bqhd', p, v)

    nb = q.shape[1] // BLOCK
    return from_blocks(lax.map(block, (to_blocks(q), jnp.arange(nb))))


def stick_breaking_attention(q, k, v, scale):
    s_len = k.shape[1]
    kpos = jnp.arange(s_len)

    def block(args):
        qb, i = args
        qpos = i * BLOCK + jnp.arange(BLOCK)
        z = jnp.einsum('bqhd,bkhd->bhqk', qb, k).astype(jnp.float32) * scale
        strict = kpos[None, :] < qpos[:, None]
        log_keep = jnp.where(strict, jax.nn.log_sigmoid(-z), 0.0)
        log_rest = lax.cumsum(log_keep, axis=3, reverse=True) - log_keep
        a = jnp.where(strict, jnp.exp(jax.nn.log_sigmoid(z) + log_rest), 0.0)
        return jnp.einsum('bhqk,bkhd->bqhd', a.astype(v.dtype), v)

    nb = q.shape[1] // BLOCK
    return from_blocks(lax.map(block, (to_blocks(q), jnp.arange(nb))))


def dilated_branch(q, k, v, window, dilation, slopes, scale):
    b, s, h, dh = q.shape
    L = s // dilation
    nb = -(-L // BLOCK)
    Lp = nb * BLOCK
    back = window // dilation

    def sub(x):
        return x.reshape(b, L, dilation, h, dh).transpose(0, 2, 1, 3, 4)

    qs = jnp.pad(sub(q), ((0, 0), (0, 0), (0, Lp - L), (0, 0), (0, 0)))
    qs = qs.reshape(b, dilation, nb, BLOCK, h, dh)

    def band_keys(x):
        xp = jnp.pad(sub(x), ((0, 0), (0, 0), (BLOCK, Lp - L), (0, 0), (0, 0)))
        prev = xp[:, :, :Lp].reshape(b, dilation, nb, BLOCK, h, dh)
        cur = xp[:, :, BLOCK:].reshape(b, dilation, nb, BLOCK, h, dh)
        return jnp.concatenate([prev, cur], axis=3)

    kb = band_keys(k)
    vb = band_keys(v)
    qi = jnp.arange(nb)[:, None] * BLOCK + jnp.arange(BLOCK)[None, :]
    kj = jnp.arange(nb)[:, None] * BLOCK - BLOCK + jnp.arange(2 * BLOCK)[None, :]
    dist = qi[:, :, None] - kj[:, None, :]
    valid = (dist >= 0) & (dist <= back) & (kj[:, None, :] >= 0)
    real_dist = (dist * dilation).astype(jnp.float32)

    sc = jnp.einsum('brnqhd,brnkhd->brnhqk', qs, kb).astype(jnp.float32) * scale
    sc = sc - slopes[:, None, None] * real_dist[:, None]
    sc = jnp.where(valid[:, None], sc, -jnp.inf)
    m = jnp.max(sc, axis=-1, keepdims=True)
    p = jnp.exp(sc - m)
    den = jnp.sum(p, axis=-1)
    o = jnp.einsum('brnhqk,brnkhd->brnqhd', p, vb.astype(jnp.float32))
    o = o / jnp.swapaxes(den, 3, 4)[..., None]
    lse = jnp.swapaxes(m[..., 0] + jnp.log(den), 3, 4)

    def unsub(y):
        y = y.reshape(b, dilation, Lp, *y.shape[4:])[:, :, :L]
        y = jnp.swapaxes(y, 1, 2)
        return y.reshape(b, s, *y.shape[3:])

    return unsub(o), unsub(lse)


def dilated_attention(q, k, v):
    slopes = 2.0 ** (-ALIBI_MAX_EXP * jnp.arange(1, DIL_HEADS + 1, dtype=jnp.float32) / DIL_HEADS)
    outs, lses = [], []
    for window, dilation in DIL_PATTERNS:
        o, lse = dilated_branch(q, k, v, window, dilation, slopes, HEAD_DIM ** -0.5)
        outs.append(o)
        lses.append(lse)
    w = jax.nn.softmax(jnp.stack(lses), axis=0)
    return jnp.sum(w[..., None] * jnp.stack(outs), axis=0).astype(q.dtype)


def token_mixer(h, w_in, mla_q_norm, w_mla_uq, mla_kv_norm, w_mla_ukv, head_out_norm, w_out, cos, sin):
    b, s, _ = h.shape
    proj = h @ w_in
    q_a, kv_a, k_rope, sb_q, sb_k, sb_v, dl_q, dl_k, dl_v = jnp.split(proj, IN_SPLITS, axis=-1)

    q = (rmsnorm(q_a, mla_q_norm) @ w_mla_uq).reshape(b, s, MLA_HEADS, MLA_NOPE + MLA_ROPE)
    q_nope, q_rope = q[..., :MLA_NOPE], q[..., MLA_NOPE:]
    kv = (rmsnorm(kv_a, mla_kv_norm) @ w_mla_ukv).reshape(b, s, MLA_HEADS, MLA_NOPE + MLA_V)
    k_nope, v_mla = kv[..., :MLA_NOPE], kv[..., MLA_NOPE:]
    q_rope = apply_rope(q_rope, cos[:, None, :], sin[:, None, :])
    k_rope = apply_rope(k_rope, cos, sin)
    q_mla = jnp.concatenate([q_nope, q_rope], axis=-1)
    k_mla = jnp.concatenate(
        [k_nope, jnp.broadcast_to(k_rope[:, :, None, :], (b, s, MLA_HEADS, MLA_ROPE))], axis=-1)
    o_mla = causal_softmax_attention(q_mla, k_mla, v_mla, (MLA_NOPE + MLA_ROPE) ** -0.5)

    o_sb = stick_breaking_attention(
        sb_q.reshape(b, s, SB_HEADS, HEAD_DIM),
        sb_k.reshape(b, s, SB_HEADS, HEAD_DIM),
        sb_v.reshape(b, s, SB_HEADS, HEAD_DIM),
        HEAD_DIM ** -0.5)

    o_dl = dilated_attention(
        dl_q.reshape(b, s, DIL_HEADS, HEAD_DIM),
        dl_k.reshape(b, s, DIL_HEADS, HEAD_DIM),
        dl_v.reshape(b, s, DIL_HEADS, HEAD_DIM))

    o = jnp.concatenate([o_mla, o_sb, o_dl], axis=2)
    o = rmsnorm(o, head_out_norm.reshape(N_HEADS_TOTAL, HEAD_DIM)).reshape(b, s, D_MIX)
    return o @ w_out


def setup_inputs(seed: int = 0) -> dict:
    key = jax.random.key(seed)
    ks = jax.random.split(key, 16)
    f32 = jnp.float32

    def w(k, shape, fan_in):
        return jax.random.normal(k, shape, f32) * (fan_in ** -0.5)

    def gain(k, shape):
        return 1.0 + 0.02 * jax.random.normal(k, shape, f32)

    return {
        "x": jax.random.normal(ks[0], (BATCH, SEQ, D_MODEL), f32),
        "ffn1_norm": gain(ks[1], (DEPTH, D_MODEL)),
        "ffn1_w_gu": w(ks[2], (DEPTH, D_MODEL, 2 * D_FF), D_MODEL),
        "ffn1_w_down": w(ks[3], (DEPTH, D_FF, D_MODEL), D_FF),
        "mix_norm": gain(ks[4], (DEPTH, D_MODEL)),
        "w_in": w(ks[5], (DEPTH, D_MODEL, N_IN), D_MODEL),
        "mla_q_norm": gain(ks[6], (DEPTH, MLA_Q_LORA)),
        "w_mla_uq": w(ks[7], (DEPTH, MLA_Q_LORA, MLA_HEADS * (MLA_NOPE + MLA_ROPE)), MLA_Q_LORA),
        "mla_kv_norm": gain(ks[8], (DEPTH, MLA_KV_LORA)),
        "w_mla_ukv": w(ks[9], (DEPTH, MLA_KV_LORA, MLA_HEADS * (MLA_NOPE + MLA_V)), MLA_KV_LORA),
        "head_out_norm": gain(ks[10], (DEPTH, D_MIX)),
        "w_out": w(ks[11], (DEPTH, D_MIX, D_MODEL), D_MIX),
        "ffn2_norm": gain(ks[12], (DEPTH, D_MODEL)),
        "ffn2_w_gu": w(ks[13], (DEPTH, D_MODEL, 2 * D_FF), D_MODEL),
        "ffn2_w_down": w(ks[14], (DEPTH, D_FF, D_MODEL), D_FF),
        "final_norm": gain(ks[15], (D_MODEL,)),
    }


def reference(x, ffn1_norm, ffn1_w_gu, ffn1_w_down, mix_norm, w_in, mla_q_norm, w_mla_uq,
              mla_kv_norm, w_mla_ukv, head_out_norm, w_out, ffn2_norm, ffn2_w_gu, ffn2_w_down,
              final_norm):
    cos, sin = rope_tables(x.shape[1])
    for l in range(DEPTH):
        x = x + 0.5 * swiglu(rmsnorm(x, ffn1_norm[l]), ffn1_w_gu[l], ffn1_w_down[l])
        x = x + token_mixer(rmsnorm(x, mix_norm[l]), w_in[l], mla_q_norm[l], w_mla_uq[l],
                            mla_kv_norm[l], w_mla_ukv[l], head_out_norm[l], w_out[l], cos, sin)
        x = x + 0.5 * swiglu(rmsnorm(x, ffn2_norm[l]), ffn2_w_gu[l], ffn2_w_down[l])
    return rmsnorm(x, final_norm)
```

```python
import functools

import jax
import jax.numpy as jnp
from jax import lax
from jax.experimental import pallas as pl
from jax.experimental.pallas import tpu as pltpu

F32 = jnp.float32
BF16 = jnp.bfloat16

EPS = 1e-6
HEAD_DIM = 128
LANES = 128
MLA_HEADS = 6
MLA_Q_LORA = 512
MLA_KV_LORA = 256
MLA_NOPE = 128
MLA_ROPE = 64
MLA_QK_PAD = 256
ROPE_THETA = 10000.0
SB_HEADS = 4
DIL_HEADS = 6
DIL_PATTERNS = ((128, 1), (512, 4), (2048, 16))
DIL_BACK = 128
ALIBI_MAX_EXP = 8.0
NEG_BIG = -1e30

MLA_IN = MLA_Q_LORA + MLA_KV_LORA + MLA_ROPE
MLA_IN_PAD = 896
SB_W = SB_HEADS * HEAD_DIM
DIL_W = DIL_HEADS * HEAD_DIM
PROJ_HALF = MLA_IN_PAD + 3 * SB_W
assert PROJ_HALF == 3 * DIL_W + LANES
SB_COL0 = MLA_IN_PAD // LANES
DIL_COL0 = PROJ_HALF // LANES

VMEM_LIMIT = 56 * 1024 * 1024


def _cparams(grid_rank):
    return pltpu.CompilerParams(dimension_semantics=("arbitrary",) * grid_rank,
                                vmem_limit_bytes=VMEM_LIMIT)


def _rms(x, g):
    return x * lax.rsqrt(jnp.mean(x * x, axis=-1, keepdims=True) + EPS) * g


def _dot(a, b):
    return jnp.dot(a, b, preferred_element_type=F32)


def _dot_nt(a, b):
    return lax.dot_general(a, b, (((1,), (1,)), ((), ())), preferred_element_type=F32)


def _ffn_body(x_ref, g_ref, wg_ref, wu_ref, wd_ref, o_ref, h_scr, acc_scr):
    j = pl.program_id(1)

    @pl.when(j == 0)
    def _():
        h_scr[...] = _rms(x_ref[...], g_ref[...]).astype(BF16)
        acc_scr[...] = jnp.zeros_like(acc_scr)

    h = h_scr[...]
    g = _dot(h, wg_ref[...])
    u = _dot(h, wu_ref[...])
    a = (g * (1.0 / (1.0 + jnp.exp(-g))) * u).astype(BF16)
    acc_scr[...] += _dot(a, wd_ref[...])

    @pl.when(j == pl.num_programs(1) - 1)
    def _():
        o_ref[...] = x_ref[...] + 0.5 * acc_scr[...]


def ffn_block(x, gain, w_gu, w_down, *, tm=512, tf=512):
    t, d = x.shape
    f = w_down.shape[0]
    nf = f // tf
    return pl.pallas_call(
        _ffn_body,
        grid=(t // tm, nf),
        in_specs=[
            pl.BlockSpec((tm, d), lambda i, j: (i, 0)),
            pl.BlockSpec((1, d), lambda i, j: (0, 0)),
            pl.BlockSpec((d, tf), lambda i, j: (0, j)),
            pl.BlockSpec((d, tf), lambda i, j: (0, j + nf)),
            pl.BlockSpec((tf, d), lambda i, j: (j, 0)),
        ],
        out_specs=pl.BlockSpec((tm, d), lambda i, j: (i, 0)),
        out_shape=jax.ShapeDtypeStruct((t, d), F32),
        scratch_shapes=[pltpu.VMEM((tm, d), BF16), pltpu.VMEM((tm, d), F32)],
        compiler_params=_cparams(2),
        name="ffn_block",
    )(x, gain.reshape(1, d), w_gu, w_gu, w_down)


def _normproj_body(x_ref, g_ref, w_ref, o_ref):
    h = _rms(x_ref[...], g_ref[...]).astype(BF16)
    o_ref[...] = _dot(h, w_ref[...])


def norm_proj(x, gain, w, *, tm=512):
    t, d = x.shape
    n = w.shape[1]
    tn = PROJ_HALF
    return pl.pallas_call(
        _normproj_body,
        grid=(n // tn, t // tm),
        in_specs=[
            pl.BlockSpec((tm, d), lambda c, i: (i, 0)),
            pl.BlockSpec((1, d), lambda c, i: (0, 0)),
            pl.BlockSpec((d, tn), lambda c, i: (0, c)),
        ],
        out_specs=pl.BlockSpec((tm, tn), lambda c, i: (i, c)),
        out_shape=jax.ShapeDtypeStruct((t, n), F32),
        compiler_params=_cparams(2),
        name="norm_proj",
    )(x, gain.reshape(1, d), w)


def _mla_prep_body(a_ref, gq_ref, gkv_ref, wuq_ref, wukv_ref, cos_ref, sin_ref,
                   q_ref, k_ref, v_ref, *, scale):
    a = a_ref[...]
    qn = _rms(a[:, :MLA_Q_LORA], gq_ref[...]).astype(BF16)
    q = _dot(qn, wuq_ref[...]) * scale
    kvn = _rms(a[:, MLA_Q_LORA:MLA_Q_LORA + MLA_KV_LORA], gkv_ref[...]).astype(BF16)
    kv = _dot(kvn, wukv_ref[...])

    cos_t = cos_ref[...]
    sin_t = sin_ref[...]
    lane = lax.broadcasted_iota(jnp.int32, cos_t.shape, 1)
    first_half = (lane & (MLA_ROPE // 2)) == 0

    def rope(x):
        swapped = jnp.where(first_half,
                            pltpu.roll(x, LANES - MLA_ROPE // 2, 1),
                            pltpu.roll(x, MLA_ROPE // 2, 1))
        return x * cos_t + swapped * sin_t

    k_rope = rope(a[:, MLA_Q_LORA + MLA_KV_LORA:]).astype(BF16)
    nk = MLA_HEADS * MLA_NOPE
    for h in range(MLA_HEADS):
        c0 = h * MLA_QK_PAD
        q_ref[:, c0:c0 + LANES] = q[:, c0:c0 + LANES].astype(BF16)
        q_ref[:, c0 + LANES:c0 + 2 * LANES] = rope(q[:, c0 + LANES:c0 + 2 * LANES]).astype(BF16)
        k_ref[:, c0:c0 + LANES] = kv[:, h * MLA_NOPE:(h + 1) * MLA_NOPE].astype(BF16)
        k_ref[:, c0 + LANES:c0 + 2 * LANES] = k_rope
    v_ref[...] = kv[:, nk:].astype(BF16)


def mla_prep(proj, gq, gkv, wuq, wukv, cos_t, sin_t, *, seq, tm=512):
    t = proj.shape[0]
    nseq = seq // tm
    wq = MLA_HEADS * MLA_QK_PAD
    wv = MLA_HEADS * HEAD_DIM
    scale = float((MLA_NOPE + MLA_ROPE) ** -0.5)
    return pl.pallas_call(
        functools.partial(_mla_prep_body, scale=scale),
        grid=(t // tm,),
        in_specs=[
            pl.BlockSpec((tm, MLA_IN_PAD), lambda i: (i, 0)),
            pl.BlockSpec((1, MLA_Q_LORA), lambda i: (0, 0)),
            pl.BlockSpec((1, MLA_KV_LORA), lambda i: (0, 0)),
            pl.BlockSpec((MLA_Q_LORA, wq), lambda i: (0, 0)),
            pl.BlockSpec((MLA_KV_LORA, 2 * wv), lambda i: (0, 0)),
            pl.BlockSpec((tm, LANES), lambda i: (i % nseq, 0)),
            pl.BlockSpec((tm, LANES), lambda i: (i % nseq, 0)),
        ],
        out_specs=[
            pl.BlockSpec((tm, wq), lambda i: (i, 0)),
            pl.BlockSpec((tm, wq), lambda i: (i, 0)),
            pl.BlockSpec((tm, wv), lambda i: (i, 0)),
        ],
        out_shape=[
            jax.ShapeDtypeStruct((t, wq), BF16),
            jax.ShapeDtypeStruct((t, wq), BF16),
            jax.ShapeDtypeStruct((t, wv), BF16),
        ],
        compiler_params=_cparams(1),
        name="mla_prep",
    )(proj, gq.reshape(1, -1), gkv.reshape(1, -1), wuq, wukv, cos_t, sin_t)


def _mla_attn_body(q_ref, k_ref, v_ref, gn_ref, o_ref, *, seq, blk):
    row = lax.broadcasted_iota(jnp.int32, (blk, blk), 0)
    col = lax.broadcasted_iota(jnp.int32, (blk, blk), 1)
    causal = col <= row

    def q_loop(qi, _):
        q0 = pl.multiple_of(qi * blk, blk)
        q = q_ref[pl.ds(q0, blk), :]

        def step(k0, carry, diag):
            m, l, acc = carry
            s = _dot_nt(q, k_ref[pl.ds(k0, blk), :])
            if diag:
                s = jnp.where(causal, s, NEG_BIG)
            m_new = jnp.maximum(m, jnp.max(s, axis=-1, keepdims=True))
            alpha = jnp.exp(m - m_new)
            p = jnp.exp(s - m_new)
            l = alpha * l + jnp.sum(p, axis=-1, keepdims=True)
            acc = alpha * acc + _dot(p.astype(BF16), v_ref[pl.ds(k0, blk), :])
            return m_new, l, acc

        carry = (jnp.full((blk, 1), NEG_BIG, F32), jnp.zeros((blk, 1), F32),
                 jnp.zeros((blk, HEAD_DIM), F32))
        carry = lax.fori_loop(
            0, qi, lambda j, c: step(pl.multiple_of(j * blk, blk), c, False), carry)
        _, l, acc = step(q0, carry, True)
        o_ref[pl.ds(q0, blk), :] = _rms(acc / l, gn_ref[...]).astype(BF16)
        return 0

    lax.fori_loop(0, seq // blk, q_loop, 0)


def mla_attention(q, k, v, head_gain, *, batch, seq, blk=256):
    t = q.shape[0]
    return pl.pallas_call(
        functools.partial(_mla_attn_body, seq=seq, blk=blk),
        grid=(batch, MLA_HEADS),
        in_specs=[
            pl.BlockSpec((seq, MLA_QK_PAD), lambda b, h: (b, h)),
            pl.BlockSpec((seq, MLA_QK_PAD), lambda b, h: (b, h)),
            pl.BlockSpec((seq, HEAD_DIM), lambda b, h: (b, h)),
            pl.BlockSpec((1, HEAD_DIM), lambda b, h: (0, h)),
        ],
        out_specs=pl.BlockSpec((seq, HEAD_DIM), lambda b, h: (b, h)),
        out_shape=jax.ShapeDtypeStruct((t, MLA_HEADS * HEAD_DIM), BF16),
        compiler_params=_cparams(2),
        name="mla_attention",
    )(q, k, v, head_gain.reshape(1, -1))


def _sb_attn_body(q_ref, k_ref, v_ref, gn_ref, o_ref, *, seq, blk, scale):
    row = lax.broadcasted_iota(jnp.int32, (blk, blk), 0)
    col = lax.broadcasted_iota(jnp.int32, (blk, blk), 1)
    strict = col < row
    suffix_ones = (row >= col).astype(BF16)

    def q_loop(qi, _):
        q0 = pl.multiple_of(qi * blk, blk)
        q = (q_ref[pl.ds(q0, blk), :] * scale).astype(BF16)

        def step(k0, carry, diag):
            rest, acc = carry
            z = _dot_nt(q, k_ref[pl.ds(k0, blk), :].astype(BF16))
            sp = jnp.maximum(z, 0.0) + jnp.log1p(jnp.exp(-jnp.abs(z)))
            if diag:
                sp = jnp.where(strict, sp, 0.0)
            sp_hi = sp.astype(BF16)
            sp_lo = (sp - sp_hi.astype(F32)).astype(BF16)
            c = _dot(sp_hi, suffix_ones) + _dot(sp_lo, suffix_ones) + rest
            a = jnp.exp(z - c)
            if diag:
                a = jnp.where(strict, a, 0.0)
            acc = acc + _dot(a.astype(BF16), v_ref[pl.ds(k0, blk), :].astype(BF16))
            return c[:, 0:1], acc

        carry = step(q0, (jnp.zeros((blk, 1), F32), jnp.zeros((blk, HEAD_DIM), F32)), True)
        _, acc = lax.fori_loop(
            0, qi, lambda i, c: step(pl.multiple_of((qi - 1 - i) * blk, blk), c, False), carry)
        o_ref[pl.ds(q0, blk), :] = _rms(acc, gn_ref[...]).astype(BF16)
        return 0

    lax.fori_loop(0, seq // blk, q_loop, 0)


def sb_attention(proj, head_gain, *, batch, seq, blk=256):
    t = proj.shape[0]
    scale = float(HEAD_DIM ** -0.5)
    gain_col0 = MLA_HEADS
    return pl.pallas_call(
        functools.partial(_sb_attn_body, seq=seq, blk=blk, scale=scale),
        grid=(batch, SB_HEADS),
        in_specs=[
            pl.BlockSpec((seq, HEAD_DIM), lambda b, h: (b, SB_COL0 + h)),
            pl.BlockSpec((seq, HEAD_DIM), lambda b, h: (b, SB_COL0 + SB_HEADS + h)),
            pl.BlockSpec((seq, HEAD_DIM), lambda b, h: (b, SB_COL0 + 2 * SB_HEADS + h)),
            pl.BlockSpec((1, HEAD_DIM), lambda b, h: (0, gain_col0 + h)),
        ],
        out_specs=pl.BlockSpec((seq, HEAD_DIM), lambda b, h: (b, h)),
        out_shape=jax.ShapeDtypeStruct((t, SB_W), BF16),
        compiler_params=_cparams(2),
        name="sb_attention",
    )(proj, proj, proj, head_gain.reshape(1, -1))


def _dil_attn_body(q_ref, k_ref, v_ref, slope_ref, gn_ref, o_ref, m_scr, l_scr, acc_scr,
                   *, seq, scale):
    blk = DIL_BACK
    row = lax.broadcasted_iota(jnp.int32, (blk, blk), 0)
    col = lax.broadcasted_iota(jnp.int32, (blk, blk), 1)
    dist_cur = (row - col).astype(F32)
    dist_prev = dist_cur + float(blk)
    ok_cur = row >= col
    ok_prev_band = row <= col
    slope = slope_ref[...]

    for p, (window, d) in enumerate(DIL_PATTERNS):
        assert window // d == blk
        nblk = seq // d // blk
        bias_cur = -(slope * float(d)) * dist_cur
        bias_prev = -(slope * float(d)) * dist_prev

        def block(idx, _, d=d, nblk=nblk, first=(p == 0), bias_cur=bias_cur, bias_prev=bias_prev):
            r = idx // nblk
            nb = idx % nblk
            nbp = jnp.maximum(nb - 1, 0)

            def rows(b):
                start = r + b * (d * blk)
                if d == 1:
                    return pl.ds(pl.multiple_of(start, blk), blk)
                return pl.ds(start, blk, stride=d)

            q = (q_ref[rows(nb), :] * scale).astype(BF16)
            s_cur = _dot_nt(q, k_ref[rows(nb), :].astype(BF16)) + bias_cur
            s_prev = _dot_nt(q, k_ref[rows(nbp), :].astype(BF16)) + bias_prev
            s_cur = jnp.where(ok_cur, s_cur, NEG_BIG)
            no_prev = jnp.where(nb > 0, 0.0, NEG_BIG)
            s_prev = jnp.where(ok_prev_band, s_prev + no_prev, NEG_BIG)
            m_blk = jnp.maximum(jnp.max(s_cur, axis=-1, keepdims=True),
                                jnp.max(s_prev, axis=-1, keepdims=True))
            p_cur = jnp.exp(s_cur - m_blk)
            p_prev = jnp.exp(s_prev - m_blk)
            l_blk = (jnp.sum(p_cur, axis=-1, keepdims=True)
                     + jnp.sum(p_prev, axis=-1, keepdims=True))
            o_blk = (_dot(p_cur.astype(BF16), v_ref[rows(nb), :].astype(BF16))
                     + _dot(p_prev.astype(BF16), v_ref[rows(nbp), :].astype(BF16)))
            m_b = jnp.broadcast_to(m_blk, (blk, HEAD_DIM))
            l_b = jnp.broadcast_to(l_blk, (blk, HEAD_DIM))
            if first:
                m_scr[rows(nb), :] = m_b
                l_scr[rows(nb), :] = l_b
                acc_scr[rows(nb), :] = o_blk
            else:
                m_old = m_scr[rows(nb), :]
                m_new = jnp.maximum(m_old, m_b)
                a_old = jnp.exp(m_old - m_new)
                a_blk = jnp.exp(m_b - m_new)
                m_scr[rows(nb), :] = m_new
                l_scr[rows(nb), :] = l_scr[rows(nb), :] * a_old + l_b * a_blk
                acc_scr[rows(nb), :] = acc_scr[rows(nb), :] * a_old + o_blk * a_blk
            return 0

        lax.fori_loop(0, d * nblk, block, 0)

    def finish(i, _):
        r0 = pl.multiple_of(i * blk, blk)
        o = acc_scr[pl.ds(r0, blk), :] / l_scr[pl.ds(r0, blk), :]
        o_ref[pl.ds(r0, blk), :] = _rms(o, gn_ref[...]).astype(BF16)
        return 0

    lax.fori_loop(0, seq // blk, finish, 0)


def dil_attention(proj, slopes, head_gain, *, batch, seq):
    t = proj.shape[0]
    scale = float(HEAD_DIM ** -0.5)
    gain_col0 = MLA_HEADS + SB_HEADS
    return pl.pallas_call(
        functools.partial(_dil_attn_body, seq=seq, scale=scale),
        grid=(batch, DIL_HEADS),
        in_specs=[
            pl.BlockSpec((seq, HEAD_DIM), lambda b, h: (b, DIL_COL0 + h)),
            pl.BlockSpec((seq, HEAD_DIM), lambda b, h: (b, DIL_COL0 + DIL_HEADS + h)),
            pl.BlockSpec((seq, HEAD_DIM), lambda b, h: (b, DIL_COL0 + 2 * DIL_HEADS + h)),
            pl.BlockSpec((None, 1, LANES), lambda b, h: (h, 0, 0)),
            pl.BlockSpec((1, HEAD_DIM), lambda b, h: (0, gain_col0 + h)),
        ],
        out_specs=pl.BlockSpec((seq, HEAD_DIM), lambda b, h: (b, h)),
        out_shape=jax.ShapeDtypeStruct((t, DIL_W), BF16),
        scratch_shapes=[pltpu.VMEM((seq, HEAD_DIM), F32)] * 3,
        compiler_params=_cparams(2),
        name="dil_attention",
    )(proj, proj, proj, slopes, head_gain.reshape(1, -1))


def _outproj_body(x_ref, o1_ref, o2_ref, o3_ref, w1_ref, w2_ref, w3_ref, out_ref):
    acc = _dot(o1_ref[...], w1_ref[...])
    acc = acc + _dot(o2_ref[...], w2_ref[...])
    acc = acc + _dot(o3_ref[...], w3_ref[...])
    out_ref[...] = x_ref[...] + acc


def out_proj(x, o_mla, o_sb, o_dil, w1, w2, w3, *, tm=512):
    t, d = x.shape
    full = lambda a: pl.BlockSpec(a.shape, lambda i: (0, 0))
    rows = lambda a: pl.BlockSpec((tm, a.shape[1]), lambda i: (i, 0))
    return pl.pallas_call(
        _outproj_body,
        grid=(t // tm,),
        in_specs=[rows(x), rows(o_mla), rows(o_sb), rows(o_dil), full(w1), full(w2), full(w3)],
        out_specs=rows(x),
        out_shape=jax.ShapeDtypeStruct((t, d), F32),
        compiler_params=_cparams(1),
        name="out_proj",
    )(x, o_mla, o_sb, o_dil, w1, w2, w3)


def _final_norm_body(x_ref, g_ref, o_ref):
    o_ref[...] = _rms(x_ref[...], g_ref[...])


def final_norm(x, gain, *, tm=512):
    t, d = x.shape
    return pl.pallas_call(
        _final_norm_body,
        grid=(t // tm,),
        in_specs=[pl.BlockSpec((tm, d), lambda i: (i, 0)), pl.BlockSpec((1, d), lambda i: (0, 0))],
        out_specs=pl.BlockSpec((tm, d), lambda i: (i, 0)),
        out_shape=jax.ShapeDtypeStruct((t, d), F32),
        compiler_params=_cparams(1),
        name="final_norm",
    )(x, gain.reshape(1, d))


def _rope_tables(seq):
    inv = ROPE_THETA ** (-jnp.arange(0, MLA_ROPE, 2, dtype=F32) / MLA_ROPE)
    ang = jnp.arange(seq, dtype=F32)[:, None] * inv[None, :]
    cos, sin = jnp.cos(ang), jnp.sin(ang)
    reps = LANES // MLA_ROPE
    cos_t = jnp.tile(jnp.concatenate([cos, cos], axis=1), (1, reps))
    sin_t = jnp.tile(jnp.concatenate([-sin, sin], axis=1), (1, reps))
    return cos_t, sin_t


def _prep_w_in(w_in):
    d = w_in.shape[0]
    z = lambda n: jnp.zeros((d, n), w_in.dtype)
    return jnp.concatenate(
        [w_in[:, :MLA_IN], z(MLA_IN_PAD - MLA_IN), w_in[:, MLA_IN:], z(LANES)], axis=1).astype(BF16)


def _prep_w_uq(w):
    r = w.shape[0]
    w = w.reshape(r, MLA_HEADS, MLA_NOPE + MLA_ROPE)
    w = jnp.pad(w, ((0, 0), (0, 0), (0, MLA_QK_PAD - MLA_NOPE - MLA_ROPE)))
    return w.reshape(r, MLA_HEADS * MLA_QK_PAD).astype(BF16)


def _prep_w_ukv(w):
    r = w.shape[0]
    w = w.reshape(r, MLA_HEADS, 2, HEAD_DIM).transpose(0, 2, 1, 3)
    return w.reshape(r, 2 * MLA_HEADS * HEAD_DIM).astype(BF16)


def kernel(x, ffn1_norm, ffn1_w_gu, ffn1_w_down, mix_norm, w_in, mla_q_norm, w_mla_uq, mla_kv_norm, w_mla_ukv, head_out_norm, w_out, ffn2_norm, ffn2_w_gu, ffn2_w_down, final_norm_g):
    batch, seq, d = x.shape
    depth = ffn1_norm.shape[0]
    cos_t, sin_t = _rope_tables(seq)
    slopes = 2.0 ** (-ALIBI_MAX_EXP * jnp.arange(1, DIL_HEADS + 1, dtype=F32) / DIL_HEADS)
    slopes = jnp.broadcast_to(slopes[:, None, None], (DIL_HEADS, 1, LANES))
    w_mla = MLA_HEADS * HEAD_DIM

    xt = x.reshape(batch * seq, d)
    for l in range(depth):
        xt = ffn_block(xt, ffn1_norm[l], ffn1_w_gu[l].astype(BF16), ffn1_w_down[l].astype(BF16))

        proj = norm_proj(xt, mix_norm[l], _prep_w_in(w_in[l]))
        q, k, v = mla_prep(proj, mla_q_norm[l], mla_kv_norm[l], _prep_w_uq(w_mla_uq[l]),
                           _prep_w_ukv(w_mla_ukv[l]), cos_t, sin_t, seq=seq)
        o_mla = mla_attention(q, k, v, head_out_norm[l], batch=batch, seq=seq)
        o_sb = sb_attention(proj, head_out_norm[l], batch=batch, seq=seq)
        o_dil = dil_attention(proj, slopes, head_out_norm[l], batch=batch, seq=seq)
        wo = w_out[l].astype(BF16)
        xt = out_proj(xt, o_mla, o_sb, o_dil, wo[:w_mla], wo[w_mla:w_mla + SB_W], wo[w_mla + SB_W:])

        xt = ffn_block(xt, ffn2_norm[l], ffn2_w_gu[l].astype(BF16), ffn2_w_down[l].astype(BF16))
    return final_norm(xt, final_norm_g).reshape(batch, seq, d)
```

```python
import functools

import jax
import jax.numpy as jnp
from jax import lax
from jax.experimental import pallas as pl
from jax.experimental.pallas import tpu as pltpu

F32 = jnp.float32
BF16 = jnp.bfloat16

EPS = 1e-6
HEAD_DIM = 128
LANES = 128
MLA_HEADS = 6
MLA_Q_LORA = 512
MLA_KV_LORA = 256
MLA_NOPE = 128
MLA_ROPE = 64
MLA_QK_PAD = 256
ROPE_THETA = 10000.0
SB_HEADS = 4
DIL_HEADS = 6
DIL_PATTERNS = ((128, 1), (512, 4), (2048, 16))
DIL_BACK = 128
DIL_GROUP = 8
ALIBI_MAX_EXP = 8.0
NEG_BIG = -1e30
SB_KEYS = 256

MLA_IN = MLA_Q_LORA + MLA_KV_LORA + MLA_ROPE
MLA_IN_PAD = 896
SB_W = SB_HEADS * HEAD_DIM
DIL_W = DIL_HEADS * HEAD_DIM
PROJ_HALF = MLA_IN_PAD + 3 * SB_W
assert PROJ_HALF == 3 * DIL_W + LANES
SB_COL0 = MLA_IN_PAD // LANES
DIL_COL0 = PROJ_HALF // LANES

VMEM_LIMIT = 56 * 1024 * 1024


def _cparams(grid_rank):
    return pltpu.CompilerParams(dimension_semantics=("arbitrary",) * grid_rank,
                                vmem_limit_bytes=VMEM_LIMIT)


def _rms(x, g):
    return x * lax.rsqrt(jnp.mean(x * x, axis=-1, keepdims=True) + EPS) * g


def _dot(a, b):
    return jnp.dot(a, b, preferred_element_type=F32)


def _dot_nt(a, b):
    return lax.dot_general(a, b, (((1,), (1,)), ((), ())), preferred_element_type=F32)


def _ffn_body(x_ref, g_ref, wg_ref, wu_ref, wd_ref, o_ref, h_scr):
    @pl.when(pl.program_id(1) == 0)
    def _():
        h_scr[...] = _rms(x_ref[...], g_ref[...]).astype(BF16)
        o_ref[...] = x_ref[...]

    h = h_scr[...]
    g = _dot(h, wg_ref[...])
    u = _dot(h, wu_ref[...])
    a = (g * (1.0 / (1.0 + jnp.exp(-g))) * u).astype(BF16)
    o_ref[...] += _dot(a, wd_ref[...])


def ffn_block(x, gain, w_gu, w_down_half, *, tm=512, tf=512):
    t, d = x.shape
    f = w_down_half.shape[0]
    nf = f // tf
    return pl.pallas_call(
        _ffn_body,
        grid=(t // tm, nf),
        in_specs=[
            pl.BlockSpec((tm, d), lambda i, j: (i, 0)),
            pl.BlockSpec((1, d), lambda i, j: (0, 0)),
            pl.BlockSpec((d, tf), lambda i, j: (0, j)),
            pl.BlockSpec((d, tf), lambda i, j: (0, j + nf)),
            pl.BlockSpec((tf, d), lambda i, j: (j, 0)),
        ],
        out_specs=pl.BlockSpec((tm, d), lambda i, j: (i, 0)),
        out_shape=jax.ShapeDtypeStruct((t, d), F32),
        scratch_shapes=[pltpu.VMEM((tm, d), BF16)],
        compiler_params=_cparams(2),
        name="ffn_block",
    )(x, gain.reshape(1, d), w_gu, w_gu, w_down_half)


def _normproj_body(x_ref, g_ref, w_ref, o_ref):
    h = _rms(x_ref[...], g_ref[...]).astype(BF16)
    o_ref[...] = _dot(h, w_ref[...])


def norm_proj(x, gain, w, *, tm=512):
    t, d = x.shape
    n = w.shape[1]
    tn = PROJ_HALF
    return pl.pallas_call(
        _normproj_body,
        grid=(n // tn, t // tm),
        in_specs=[
            pl.BlockSpec((tm, d), lambda c, i: (i, 0)),
            pl.BlockSpec((1, d), lambda c, i: (0, 0)),
            pl.BlockSpec((d, tn), lambda c, i: (0, c)),
        ],
        out_specs=pl.BlockSpec((tm, tn), lambda c, i: (i, c)),
        out_shape=jax.ShapeDtypeStruct((t, n), F32),
        compiler_params=_cparams(2),
        name="norm_proj",
    )(x, gain.reshape(1, d), w)


def _mla_prep_body(a_ref, gq_ref, gkv_ref, wuq_ref, wukv_ref, cos_ref, sin_ref,
                   q_ref, k_ref, v_ref, *, scale):
    a = a_ref[...]
    qn = _rms(a[:, :MLA_Q_LORA], gq_ref[...]).astype(BF16)
    q = _dot(qn, wuq_ref[...]) * scale
    kvn = _rms(a[:, MLA_Q_LORA:MLA_Q_LORA + MLA_KV_LORA], gkv_ref[...]).astype(BF16)
    kv = _dot(kvn, wukv_ref[...])

    cos_t = cos_ref[...]
    sin_t = sin_ref[...]
    lane = lax.broadcasted_iota(jnp.int32, cos_t.shape, 1)
    first_half = (lane & (MLA_ROPE // 2)) == 0

    def rope(x):
        swapped = jnp.where(first_half,
                            pltpu.roll(x, LANES - MLA_ROPE // 2, 1),
                            pltpu.roll(x, MLA_ROPE // 2, 1))
        return x * cos_t + swapped * sin_t

    k_rope = rope(a[:, MLA_Q_LORA + MLA_KV_LORA:]).astype(BF16)
    nk = MLA_HEADS * MLA_NOPE
    for h in range(MLA_HEADS):
        c0 = h * MLA_QK_PAD
        q_ref[:, c0:c0 + LANES] = q[:, c0:c0 + LANES].astype(BF16)
        q_ref[:, c0 + LANES:c0 + 2 * LANES] = rope(q[:, c0 + LANES:c0 + 2 * LANES]).astype(BF16)
        k_ref[:, c0:c0 + LANES] = kv[:, h * MLA_NOPE:(h + 1) * MLA_NOPE].astype(BF16)
        k_ref[:, c0 + LANES:c0 + 2 * LANES] = k_rope
    v_ref[...] = kv[:, nk:].astype(BF16)


def mla_prep(proj, gq, gkv, wuq, wukv, cos_t, sin_t, *, seq, tm=512):
    t = proj.shape[0]
    nseq = seq // tm
    wq = MLA_HEADS * MLA_QK_PAD
    wv = MLA_HEADS * HEAD_DIM
    scale = float((MLA_NOPE + MLA_ROPE) ** -0.5)
    return pl.pallas_call(
        functools.partial(_mla_prep_body, scale=scale),
        grid=(t // tm,),
        in_specs=[
            pl.BlockSpec((tm, MLA_IN_PAD), lambda i: (i, 0)),
            pl.BlockSpec((1, MLA_Q_LORA), lambda i: (0, 0)),
            pl.BlockSpec((1, MLA_KV_LORA), lambda i: (0, 0)),
            pl.BlockSpec((MLA_Q_LORA, wq), lambda i: (0, 0)),
            pl.BlockSpec((MLA_KV_LORA, 2 * wv), lambda i: (0, 0)),
            pl.BlockSpec((tm, LANES), lambda i: (i % nseq, 0)),
            pl.BlockSpec((tm, LANES), lambda i: (i % nseq, 0)),
        ],
        out_specs=[
            pl.BlockSpec((tm, wq), lambda i: (i, 0)),
            pl.BlockSpec((tm, wq), lambda i: (i, 0)),
            pl.BlockSpec((tm, wv), lambda i: (i, 0)),
        ],
        out_shape=[
            jax.ShapeDtypeStruct((t, wq), BF16),
            jax.ShapeDtypeStruct((t, wq), BF16),
            jax.ShapeDtypeStruct((t, wv), BF16),
        ],
        compiler_params=_cparams(1),
        name="mla_prep",
    )(proj, gq.reshape(1, -1), gkv.reshape(1, -1), wuq, wukv, cos_t, sin_t)


def _mla_attn_body(q_ref, k_ref, v_ref, gn_ref, o_ref, s0_scr, s1_scr, acc_scr, *, seq, rc):
    nc = 2
    tq = nc * rc
    row = lax.broadcasted_iota(jnp.int32, (rc, rc), 0)
    col = lax.broadcasted_iota(jnp.int32, (rc, rc), 1)
    causal = col <= row

    def scores(q0, k0, s_scr):
        for c in range(nc):
            s_scr[c] = _dot_nt(q_ref[pl.ds(q0 + c * rc, rc), :], k_ref[pl.ds(k0, rc), :])

    def absorb(c, s, k0, state):
        m, l = state
        m_new = jnp.maximum(m, jnp.max(s, axis=1, keepdims=True))
        alpha = jnp.exp(m - m_new)
        p = jnp.exp(s - m_new)
        l = alpha * l + jnp.sum(p, axis=1, keepdims=True)
        acc_scr[c] = alpha * acc_scr[c] + _dot(p.astype(BF16), v_ref[pl.ds(k0, rc), :])
        return m_new, l

    def q_loop(qi, _):
        q0 = pl.multiple_of(qi * tq, tq)
        for c in range(nc):
            acc_scr[c] = jnp.zeros((rc, HEAD_DIM), F32)
        init = tuple((jnp.full((rc, 1), NEG_BIG, F32), jnp.zeros((rc, 1), F32))
                     for _ in range(nc))
        scores(q0, 0, s0_scr)

        def kv_loop(i, states):
            ka = pl.multiple_of(2 * i * rc, rc)
            kb = pl.multiple_of(ka + rc, rc)
            kn = pl.multiple_of(kb + rc, rc)
            scores(q0, kb, s1_scr)
            states = tuple(absorb(c, s0_scr[c], ka, states[c]) for c in range(nc))
            scores(q0, kn, s0_scr)
            return tuple(absorb(c, s1_scr[c], kb, states[c]) for c in range(nc))

        states = lax.fori_loop(0, qi, kv_loop, init)
        s_last = _dot_nt(q_ref[pl.ds(q0 + rc, rc), :], k_ref[pl.ds(q0 + rc, rc), :])
        st0 = absorb(0, jnp.where(causal, s0_scr[0], NEG_BIG), q0, states[0])
        st1 = absorb(1, s0_scr[1], q0, states[1])
        st1 = absorb(1, jnp.where(causal, s_last, NEG_BIG), q0 + rc, st1)
        for c, (_, l) in enumerate((st0, st1)):
            o_ref[pl.ds(q0 + c * rc, rc), :] = _rms(acc_scr[c] / l, gn_ref[...]).astype(BF16)
        return 0

    lax.fori_loop(0, seq // tq, q_loop, 0)


def mla_attention(q, k, v, head_gain, *, batch, seq, rc=512):
    t = q.shape[0]
    return pl.pallas_call(
        functools.partial(_mla_attn_body, seq=seq, rc=rc),
        grid=(batch, MLA_HEADS),
        in_specs=[
            pl.BlockSpec((seq, MLA_QK_PAD), lambda b, h: (b, h)),
            pl.BlockSpec((seq, MLA_QK_PAD), lambda b, h: (b, h)),
            pl.BlockSpec((seq, HEAD_DIM), lambda b, h: (b, h)),
            pl.BlockSpec((1, HEAD_DIM), lambda b, h: (0, h)),
        ],
        out_specs=pl.BlockSpec((seq, HEAD_DIM), lambda b, h: (b, h)),
        out_shape=jax.ShapeDtypeStruct((t, MLA_HEADS * HEAD_DIM), BF16),
        scratch_shapes=[pltpu.VMEM((2, rc, rc), F32), pltpu.VMEM((2, rc, rc), F32),
                        pltpu.VMEM((2, rc, HEAD_DIM), F32)],
        compiler_params=_cparams(2),
        name="mla_attention",
    )(q, k, v, head_gain.reshape(1, -1))


def _sb_attn_body(q_ref, k_ref, v_ref, gn_ref, o_ref, qb_scr, kb_scr, vb_scr, acc_scr,
                  *, seq, nc, rc, scale):
    tq = nc * rc
    tk = SB_KEYS
    sub_per_q = tq // tk
    qb_scr[...] = (q_ref[...] * scale).astype(BF16)
    kb_scr[...] = k_ref[...].astype(BF16)
    vb_scr[...] = v_ref[...].astype(BF16)
    row = lax.broadcasted_iota(jnp.int32, (rc, tk), 0)
    col = lax.broadcasted_iota(jnp.int32, (rc, tk), 1)
    urow = lax.broadcasted_iota(jnp.int32, (tk, tk), 0)
    ucol = lax.broadcasted_iota(jnp.int32, (tk, tk), 1)
    later_keys = (urow > ucol).astype(BF16)

    def chain(q0c, c, ks, rest, mask_shift):
        k0 = pl.multiple_of(ks * tk, tk)
        z = _dot_nt(qb_scr[pl.ds(q0c, rc), :], kb_scr[pl.ds(k0, tk), :])
        t = jnp.log(1.0 + jnp.exp(-jnp.abs(z)))
        sp = jnp.maximum(z, 0.0) + t
        ls = jnp.minimum(z, 0.0) - t
        if mask_shift is not None:
            strict = col < row + mask_shift
            sp = jnp.where(strict, sp, 0.0)
        sp_b = sp.astype(BF16)
        later = _dot(sp_b, later_keys)
        a = jnp.exp(ls - later - rest)
        if mask_shift is not None:
            a = jnp.where(strict, a, 0.0)
        acc_scr[c] += _dot(a.astype(BF16), vb_scr[pl.ds(k0, tk), :])
        return rest + later[:, 0:1] + sp_b[:, 0:1].astype(F32)

    def q_loop(qi, _):
        q0 = pl.multiple_of(qi * tq, tq)
        for c in range(nc):
            acc_scr[c] = jnp.zeros((rc, HEAD_DIM), F32)
        rests = [jnp.zeros((rc, 1), F32) for _ in range(nc)]
        for c in range(nc):
            for j in range((c + 1) * rc // tk - 1, -1, -1):
                shift = c * rc - j * tk
                rests[c] = chain(q0 + c * rc, c, qi * sub_per_q + j, rests[c],
                                 shift if shift < tk else None)

        def kv_loop(i, rests):
            ks = qi * sub_per_q - 1 - 2 * i
            out = []
            for c in range(nc):
                r = chain(q0 + c * rc, c, ks, rests[c], None)
                out.append(chain(q0 + c * rc, c, ks - 1, r, None))
            return tuple(out)

        lax.fori_loop(0, qi * (sub_per_q // 2), kv_loop, tuple(rests))
        for c in range(nc):
            o_ref[pl.ds(q0 + c * rc, rc), :] = _rms(acc_scr[c], gn_ref[...]).astype(BF16)
        return 0

    lax.fori_loop(0, seq // tq, q_loop, 0)


def sb_attention(proj, head_gain, *, batch, seq, nc=2, rc=512):
    assert (nc * rc // SB_KEYS) % 2 == 0
    t = proj.shape[0]
    scale = float(HEAD_DIM ** -0.5)
    gain_col0 = MLA_HEADS
    return pl.pallas_call(
        functools.partial(_sb_attn_body, seq=seq, nc=nc, rc=rc, scale=scale),
        grid=(batch, SB_HEADS),
        in_specs=[
            pl.BlockSpec((seq, HEAD_DIM), lambda b, h: (b, SB_COL0 + h)),
            pl.BlockSpec((seq, HEAD_DIM), lambda b, h: (b, SB_COL0 + SB_HEADS + h)),
            pl.BlockSpec((seq, HEAD_DIM), lambda b, h: (b, SB_COL0 + 2 * SB_HEADS + h)),
            pl.BlockSpec((1, HEAD_DIM), lambda b, h: (0, gain_col0 + h)),
        ],
        out_specs=pl.BlockSpec((seq, HEAD_DIM), lambda b, h: (b, h)),
        out_shape=jax.ShapeDtypeStruct((t, SB_W), BF16),
        scratch_shapes=[pltpu.VMEM((seq, HEAD_DIM), BF16),
                        pltpu.VMEM((seq, HEAD_DIM), BF16),
                        pltpu.VMEM((seq, HEAD_DIM), BF16),
                        pltpu.VMEM((nc, rc, HEAD_DIM), F32)],
        compiler_params=_cparams(2),
        name="sb_attention",
    )(proj, proj, proj, head_gain.reshape(1, -1))


def _dil_attn_body(q_ref, k_ref, v_ref, slope_ref, gn_ref, o_ref, *scr, seq, scale):
    blk = DIL_BACK
    npat = len(DIL_PATTERNS)
    lse_scr, out_scr = scr[:npat], scr[npat:]
    row = lax.broadcasted_iota(jnp.int32, (blk, blk), 0)
    col = lax.broadcasted_iota(jnp.int32, (blk, blk), 1)
    dist_cur = (row - col).astype(F32)
    dist_prev = dist_cur + float(blk)
    ok_cur = row >= col
    ok_prev_band = row <= col
    slope = slope_ref[...]

    for p, (window, d) in enumerate(DIL_PATTERNS):
        assert window // d == blk
        nblk = seq // d // blk
        assert nblk & (nblk - 1) == 0
        bias_cur = -(slope * float(d)) * dist_cur
        bias_prev = -(slope * float(d)) * dist_prev

        def group(gi, _, p=p, d=d, nblk=nblk, bias_cur=bias_cur, bias_prev=bias_prev):
            def rows(idx, back):
                r = lax.shift_right_logical(idx, nblk.bit_length() - 1)
                nb = lax.bitwise_and(idx, nblk - 1)
                b = jnp.maximum(nb - back, 0)
                start = r + b * (d * blk)
                if d == 1:
                    return pl.ds(pl.multiple_of(start, blk), blk)
                return pl.ds(start, blk, stride=d)

            idxs = [gi * DIL_GROUP + g for g in range(DIL_GROUP)]
            k_own = [k_ref[rows(idx, 0), :].astype(BF16) for idx in idxs]
            k_before = [k_ref[rows(idxs[0], 1), :].astype(BF16)] + k_own[:-1]
            scores = []
            for idx, kc, kp in zip(idxs, k_own, k_before):
                q = (q_ref[rows(idx, 0), :] * scale).astype(BF16)
                scores.append((_dot_nt(q, kc), _dot_nt(q, kp)))
            probs = []
            for idx, (s_cur, s_prev) in zip(idxs, scores):
                nb = lax.bitwise_and(idx, nblk - 1)
                s_cur = jnp.where(ok_cur, s_cur + bias_cur, NEG_BIG)
                no_prev = jnp.where(nb > 0, 0.0, NEG_BIG)
                s_prev = jnp.where(ok_prev_band, s_prev + bias_prev + no_prev, NEG_BIG)
                m_blk = jnp.max(jnp.maximum(s_cur, s_prev), axis=-1, keepdims=True)
                p_cur = jnp.exp(s_cur - m_blk)
                p_prev = jnp.exp(s_prev - m_blk)
                l_blk = jnp.broadcast_to(jnp.sum(p_cur + p_prev, axis=-1, keepdims=True),
                                         (blk, HEAD_DIM))
                lse_scr[p][rows(idx, 0), :] = m_blk + jnp.log(l_blk)
                probs.append((p_cur.astype(BF16), p_prev.astype(BF16), l_blk))
            v_own = [v_ref[rows(idx, 0), :].astype(BF16) for idx in idxs]
            v_before = [v_ref[rows(idxs[0], 1), :].astype(BF16)] + v_own[:-1]
            for idx, (p_cur, p_prev, l_blk), vc, vp in zip(idxs, probs, v_own, v_before):
                out_scr[p][rows(idx, 0), :] = (_dot(p_cur, vc) + _dot(p_prev, vp)) / l_blk
            return 0

        lax.fori_loop(0, d * nblk // DIL_GROUP, group, 0)

    def finish(i, _):
        rs = pl.ds(pl.multiple_of(i * blk, blk), blk)
        lses = [lse_scr[p][rs, :] for p in range(npat)]
        top = functools.reduce(jnp.maximum, lses)
        den = jnp.zeros((blk, HEAD_DIM), F32)
        num = jnp.zeros((blk, HEAD_DIM), F32)
        for p in range(npat):
            w = jnp.exp(lses[p] - top)
            den = den + w
            num = num + out_scr[p][rs, :] * w
        o_ref[rs, :] = _rms(num / den, gn_ref[...]).astype(BF16)
        return 0

    lax.fori_loop(0, seq // blk, finish, 0, unroll=2)


def dil_attention(proj, slopes, head_gain, *, batch, seq):
    t = proj.shape[0]
    scale = float(HEAD_DIM ** -0.5)
    gain_col0 = MLA_HEADS + SB_HEADS
    return pl.pallas_call(
        functools.partial(_dil_attn_body, seq=seq, scale=scale),
        grid=(batch, DIL_HEADS),
        in_specs=[
            pl.BlockSpec((seq, HEAD_DIM), lambda b, h: (b, DIL_COL0 + h)),
            pl.BlockSpec((seq, HEAD_DIM), lambda b, h: (b, DIL_COL0 + DIL_HEADS + h)),
            pl.BlockSpec((seq, HEAD_DIM), lambda b, h: (b, DIL_COL0 + 2 * DIL_HEADS + h)),
            pl.BlockSpec((None, 1, LANES), lambda b, h: (h, 0, 0)),
            pl.BlockSpec((1, HEAD_DIM), lambda b, h: (0, gain_col0 + h)),
        ],
        out_specs=pl.BlockSpec((seq, HEAD_DIM), lambda b, h: (b, h)),
        out_shape=jax.ShapeDtypeStruct((t, DIL_W), BF16),
        scratch_shapes=[pltpu.VMEM((seq, HEAD_DIM), F32)] * (2 * len(DIL_PATTERNS)),
        compiler_params=_cparams(2),
        name="dil_attention",
    )(proj, proj, proj, slopes, head_gain.reshape(1, -1))


def _outproj_body(x_ref, o1_ref, o2_ref, o3_ref, w1_ref, w2_ref, w3_ref, out_ref):
    acc = _dot(o1_ref[...], w1_ref[...])
    acc = acc + _dot(o2_ref[...], w2_ref[...])
    acc = acc + _dot(o3_ref[...], w3_ref[...])
    out_ref[...] = x_ref[...] + acc


def out_proj(x, o_mla, o_sb, o_dil, w1, w2, w3, *, tm=512):
    t, d = x.shape
    full = lambda a: pl.BlockSpec(a.shape, lambda i: (0, 0))
    rows = lambda a: pl.BlockSpec((tm, a.shape[1]), lambda i: (i, 0))
    return pl.pallas_call(
        _outproj_body,
        grid=(t // tm,),
        in_specs=[rows(x), rows(o_mla), rows(o_sb), rows(o_dil), full(w1), full(w2), full(w3)],
        out_specs=rows(x),
        out_shape=jax.ShapeDtypeStruct((t, d), F32),
        compiler_params=_cparams(1),
        name="out_proj",
    )(x, o_mla, o_sb, o_dil, w1, w2, w3)


def _final_norm_body(x_ref, g_ref, o_ref):
    o_ref[...] = _rms(x_ref[...], g_ref[...])


def final_norm(x, gain, *, tm=512):
    t, d = x.shape
    return pl.pallas_call(
        _final_norm_body,
        grid=(t // tm,),
        in_specs=[pl.BlockSpec((tm, d), lambda i: (i, 0)), pl.BlockSpec((1, d), lambda i: (0, 0))],
        out_specs=pl.BlockSpec((tm, d), lambda i: (i, 0)),
        out_shape=jax.ShapeDtypeStruct((t, d), F32),
        compiler_params=_cparams(1),
        name="final_norm",
    )(x, gain.reshape(1, d))


def _rope_tables(seq):
    inv = ROPE_THETA ** (-jnp.arange(0, MLA_ROPE, 2, dtype=F32) / MLA_ROPE)
    ang = jnp.arange(seq, dtype=F32)[:, None] * inv[None, :]
    cos, sin = jnp.cos(ang), jnp.sin(ang)
    reps = LANES // MLA_ROPE
    cos_t = jnp.tile(jnp.concatenate([cos, cos], axis=1), (1, reps))
    sin_t = jnp.tile(jnp.concatenate([-sin, sin], axis=1), (1, reps))
    return cos_t, sin_t


def _prep_w_in(w_in):
    d = w_in.shape[0]
    z = lambda n: jnp.zeros((d, n), w_in.dtype)
    return jnp.concatenate(
        [w_in[:, :MLA_IN], z(MLA_IN_PAD - MLA_IN), w_in[:, MLA_IN:], z(LANES)], axis=1).astype(BF16)


def _prep_w_uq(w):
    r = w.shape[0]
    w = w.reshape(r, MLA_HEADS, MLA_NOPE + MLA_ROPE)
    w = jnp.pad(w, ((0, 0), (0, 0), (0, MLA_QK_PAD - MLA_NOPE - MLA_ROPE)))
    return w.reshape(r, MLA_HEADS * MLA_QK_PAD).astype(BF16)


def _prep_w_ukv(w):
    r = w.shape[0]
    w = w.reshape(r, MLA_HEADS, 2, HEAD_DIM).transpose(0, 2, 1, 3)
    return w.reshape(r, 2 * MLA_HEADS * HEAD_DIM).astype(BF16)


def kernel(x, ffn1_norm, ffn1_w_gu, ffn1_w_down, mix_norm, w_in, mla_q_norm, w_mla_uq, mla_kv_norm, w_mla_ukv, head_out_norm, w_out, ffn2_norm, ffn2_w_gu, ffn2_w_down, final_norm_g):
    batch, seq, d = x.shape
    depth = ffn1_norm.shape[0]
    cos_t, sin_t = _rope_tables(seq)
    slopes = 2.0 ** (-ALIBI_MAX_EXP * jnp.arange(1, DIL_HEADS + 1, dtype=F32) / DIL_HEADS)
    slopes = jnp.broadcast_to(slopes[:, None, None], (DIL_HEADS, 1, LANES))
    w_mla = MLA_HEADS * HEAD_DIM

    xt = x.reshape(batch * seq, d)
    for l in range(depth):
        xt = ffn_block(xt, ffn1_norm[l], ffn1_w_gu[l].astype(BF16),
                       (0.5 * ffn1_w_down[l]).astype(BF16))

        proj = norm_proj(xt, mix_norm[l], _prep_w_in(w_in[l]))
        q, k, v = mla_prep(proj, mla_q_norm[l], mla_kv_norm[l], _prep_w_uq(w_mla_uq[l]),
                           _prep_w_ukv(w_mla_ukv[l]), cos_t, sin_t, seq=seq)
        o_mla = mla_attention(q, k, v, head_out_norm[l], batch=batch, seq=seq)
        o_sb = sb_attention(proj, head_out_norm[l], batch=batch, seq=seq)
        o_dil = dil_attention(proj, slopes, head_out_norm[l], batch=batch, seq=seq)
        wo = w_out[l].astype(BF16)
        xt = out_proj(xt, o_mla, o_sb, o_dil, wo[:w_mla], wo[w_mla:w_mla + SB_W], wo[w_mla + SB_W:])

        xt = ffn_block(xt, ffn2_norm[l], ffn2_w_gu[l].astype(BF16),
                       (0.5 * ffn2_w_down[l]).astype(BF16))
    return final_norm(xt, final_norm_g).reshape(batch, seq, d)
```

```python
import functools

import jax
import jax.numpy as jnp
from jax import lax
from jax.experimental import pallas as pl
from jax.experimental.pallas import tpu as pltpu

F32 = jnp.float32
BF16 = jnp.bfloat16

EPS = 1e-6
HEAD_DIM = 128
LANES = 128
MLA_HEADS = 6
MLA_Q_LORA = 512
MLA_KV_LORA = 256
MLA_NOPE = 128
MLA_ROPE = 64
MLA_QK_PAD = 256
ROPE_THETA = 10000.0
SB_HEADS = 4
DIL_HEADS = 6
DIL_PATTERNS = ((128, 1), (512, 4), (2048, 16))
DIL_BACK = 128
DIL_GROUP = 8
ALIBI_MAX_EXP = 8.0
NEG_BIG = -1e30
LOG2E = 1.4426950408889634
SB_KEYS = 256
SB_SUBS = 2

MLA_IN = MLA_Q_LORA + MLA_KV_LORA + MLA_ROPE
MLA_IN_PAD = 896
SB_W = SB_HEADS * HEAD_DIM
DIL_W = DIL_HEADS * HEAD_DIM
PROJ_HALF = MLA_IN_PAD + 3 * SB_W
assert PROJ_HALF == 3 * DIL_W + LANES
SB_COL0 = MLA_IN_PAD // LANES
DIL_COL0 = PROJ_HALF // LANES

VMEM_LIMIT = 56 * 1024 * 1024


def _cparams(grid_rank):
    return pltpu.CompilerParams(dimension_semantics=("arbitrary",) * grid_rank,
                                vmem_limit_bytes=VMEM_LIMIT)


def _rms(x, g):
    return x * lax.rsqrt(jnp.mean(x * x, axis=-1, keepdims=True) + EPS) * g


def _dot(a, b):
    return jnp.dot(a, b, preferred_element_type=F32)


def _dot_nt(a, b):
    return lax.dot_general(a, b, (((1,), (1,)), ((), ())), preferred_element_type=F32)


def _ffn_body(x_ref, xn_ref, g_ref, wg_ref, wu_ref, wd_ref, *rest, final_norm):
    fg_ref, o_ref, h_scr = rest if final_norm else (None,) + rest
    i = pl.program_id(0)
    j = pl.program_id(1)
    last = pl.num_programs(1) - 1

    @pl.when(j == 0)
    def _():
        o_ref[...] = x_ref[...]

    @pl.when(jnp.logical_and(i == 0, j == 0))
    def _():
        h_scr[...] = _rms(x_ref[...], g_ref[...]).astype(BF16)

    def hidden_tile():
        h = h_scr[...]
        g = _dot(h, wg_ref[...])
        u = _dot(h, wu_ref[...])
        a = (g * (1.0 / (1.0 + jnp.exp(-g))) * u).astype(BF16)
        return _dot(a, wd_ref[...])

    @pl.when(j < last)
    def _():
        o_ref[...] += hidden_tile()

    @pl.when(j == last)
    def _():
        y = o_ref[...] + hidden_tile()
        h_scr[...] = _rms(xn_ref[...], g_ref[...]).astype(BF16)
        o_ref[...] = _rms(y, fg_ref[...]) if final_norm else y


def ffn_block(x, gain, w_gu, w_down_half, final_gain=None, *, tm=512, tf=512):
    t, d = x.shape
    f = w_down_half.shape[0]
    nf = f // tf
    nt = t // tm
    final_norm = final_gain is not None
    row = pl.BlockSpec((1, d), lambda i, j: (0, 0))
    return pl.pallas_call(
        functools.partial(_ffn_body, final_norm=final_norm),
        grid=(nt, nf),
        in_specs=[
            pl.BlockSpec((tm, d), lambda i, j: (i, 0)),
            pl.BlockSpec((tm, d), lambda i, j: (jnp.minimum(i + 1, nt - 1), 0)),
            row,
            pl.BlockSpec((d, tf), lambda i, j: (0, j)),
            pl.BlockSpec((d, tf), lambda i, j: (0, j + nf)),
            pl.BlockSpec((tf, d), lambda i, j: (j, 0)),
        ] + ([row] if final_norm else []),
        out_specs=pl.BlockSpec((tm, d), lambda i, j: (i, 0)),
        out_shape=jax.ShapeDtypeStruct((t, d), F32),
        scratch_shapes=[pltpu.VMEM((tm, d), BF16)],
        compiler_params=_cparams(2),
        name="ffn_block",
    )(x, x, gain.reshape(1, d), w_gu, w_gu, w_down_half,
      *([final_gain.reshape(1, d)] if final_norm else []))


def _normproj_body(x0_ref, xn_ref, g_ref, w_ref, o_ref, h_scr):
    i = pl.program_id(0)
    c = pl.program_id(1)
    last = pl.num_programs(1) - 1

    @pl.when(jnp.logical_and(i == 0, c == 0))
    def _():
        h_scr[...] = _rms(x0_ref[...], g_ref[...]).astype(BF16)

    @pl.when(c < last)
    def _():
        o_ref[...] = _dot(h_scr[...], w_ref[...])

    @pl.when(c == last)
    def _():
        o_ref[...] = _dot(h_scr[...], w_ref[...])
        h_scr[...] = _rms(xn_ref[...], g_ref[...]).astype(BF16)


def norm_proj(x, gain, w, *, tm=512):
    t, d = x.shape
    n = w.shape[1]
    tn = PROJ_HALF
    nt = t // tm
    return pl.pallas_call(
        _normproj_body,
        grid=(nt, n // tn),
        in_specs=[
            pl.BlockSpec((tm, d), lambda i, c: (0, 0)),
            pl.BlockSpec((tm, d), lambda i, c: (jnp.minimum(i + 1, nt - 1), 0)),
            pl.BlockSpec((1, d), lambda i, c: (0, 0)),
            pl.BlockSpec((d, tn), lambda i, c: (0, c)),
        ],
        out_specs=pl.BlockSpec((tm, tn), lambda i, c: (i, c)),
        out_shape=jax.ShapeDtypeStruct((t, n), F32),
        scratch_shapes=[pltpu.VMEM((tm, d), BF16)],
        compiler_params=_cparams(2),
        name="norm_proj",
    )(x, x, gain.reshape(1, d), w)


def _mla_prep_body(a_ref, gq_ref, gkv_ref, wuq_ref, wukv_ref, cos_ref, sin_ref,
                   q_ref, k_ref, v_ref, *, scale):
    a = a_ref[...]
    qn = _rms(a[:, :MLA_Q_LORA], gq_ref[...]).astype(BF16)
    q = _dot(qn, wuq_ref[...]) * scale
    kvn = _rms(a[:, MLA_Q_LORA:MLA_Q_LORA + MLA_KV_LORA], gkv_ref[...]).astype(BF16)
    kv = _dot(kvn, wukv_ref[...])

    cos_t = cos_ref[...]
    sin_t = sin_ref[...]
    lane = lax.broadcasted_iota(jnp.int32, cos_t.shape, 1)
    first_half = (lane & (MLA_ROPE // 2)) == 0

    def rope(x):
        swapped = jnp.where(first_half,
                            pltpu.roll(x, LANES - MLA_ROPE // 2, 1),
                            pltpu.roll(x, MLA_ROPE // 2, 1))
        return x * cos_t + swapped * sin_t

    k_rope = rope(a[:, MLA_Q_LORA + MLA_KV_LORA:]).astype(BF16)
    nk = MLA_HEADS * MLA_NOPE
    for h in range(MLA_HEADS):
        c0 = h * MLA_QK_PAD
        q_ref[:, c0:c0 + LANES] = q[:, c0:c0 + LANES].astype(BF16)
        q_ref[:, c0 + LANES:c0 + 2 * LANES] = rope(q[:, c0 + LANES:c0 + 2 * LANES]).astype(BF16)
        k_ref[:, c0:c0 + LANES] = kv[:, h * MLA_NOPE:(h + 1) * MLA_NOPE].astype(BF16)
        k_ref[:, c0 + LANES:c0 + 2 * LANES] = k_rope
    v_ref[...] = kv[:, nk:].astype(BF16)


def mla_prep(proj, gq, gkv, wuq, wukv, cos_t, sin_t, *, seq, tm=512):
    t = proj.shape[0]
    nseq = seq // tm
    wq = MLA_HEADS * MLA_QK_PAD
    wv = MLA_HEADS * HEAD_DIM
    scale = float((MLA_NOPE + MLA_ROPE) ** -0.5) * LOG2E
    return pl.pallas_call(
        functools.partial(_mla_prep_body, scale=scale),
        grid=(t // tm,),
        in_specs=[
            pl.BlockSpec((tm, MLA_IN_PAD), lambda i: (i, 0)),
            pl.BlockSpec((1, MLA_Q_LORA), lambda i: (0, 0)),
            pl.BlockSpec((1, MLA_KV_LORA), lambda i: (0, 0)),
            pl.BlockSpec((MLA_Q_LORA, wq), lambda i: (0, 0)),
            pl.BlockSpec((MLA_KV_LORA, 2 * wv), lambda i: (0, 0)),
            pl.BlockSpec((tm, LANES), lambda i: (i % nseq, 0)),
            pl.BlockSpec((tm, LANES), lambda i: (i % nseq, 0)),
        ],
        out_specs=[
            pl.BlockSpec((tm, wq), lambda i: (i, 0)),
            pl.BlockSpec((tm, wq), lambda i: (i, 0)),
            pl.BlockSpec((tm, wv), lambda i: (i, 0)),
        ],
        out_shape=[
            jax.ShapeDtypeStruct((t, wq), BF16),
            jax.ShapeDtypeStruct((t, wq), BF16),
            jax.ShapeDtypeStruct((t, wv), BF16),
        ],
        compiler_params=_cparams(1),
        name="mla_prep",
    )(proj, gq.reshape(1, -1), gkv.reshape(1, -1), wuq, wukv, cos_t, sin_t)


def _mla_attn_body(q_ref, k_ref, v_ref, gn_ref, o_ref, acc_scr, *, seq, nc, rc):
    tq = nc * rc
    row = lax.broadcasted_iota(jnp.int32, (rc, rc), 0)
    col = lax.broadcasted_iota(jnp.int32, (rc, rc), 1)
    causal = col <= row

    def run(q0, jobs, states):
        scores = [_dot_nt(q_ref[pl.ds(q0 + c * rc, rc), :], k_ref[pl.ds(k0, rc), :])
                  for c, k0, _ in jobs]
        states = list(states)
        for (c, k0, diag), s in zip(jobs, scores):
            if diag:
                s = jnp.where(causal, s, NEG_BIG)
            m, l = states[c]
            m_new = jnp.maximum(m, jnp.max(s, axis=1, keepdims=True))
            alpha = jnp.exp2(m - m_new)
            p = jnp.exp2(s - m_new)
            l = alpha * l + jnp.sum(p, axis=1, keepdims=True)
            acc_scr[c] = alpha * acc_scr[c] + _dot(p.astype(BF16), v_ref[pl.ds(k0, rc), :])
            states[c] = (m_new, l)
        return tuple(states)

    def q_loop(qi, _):
        q0 = pl.multiple_of(qi * tq, tq)
        for c in range(nc):
            acc_scr[c] = jnp.zeros((rc, HEAD_DIM), F32)
        init = tuple((jnp.full((rc, 1), NEG_BIG, F32), jnp.zeros((rc, 1), F32))
                     for _ in range(nc))

        def kv_loop(i, states):
            k0 = pl.multiple_of(i * tq, tq)
            return run(q0, [(c, k0 + t * rc, False) for t in range(nc) for c in range(nc)],
                       states)

        states = lax.fori_loop(0, qi, kv_loop, init)
        states = run(q0, [(c, q0 + t * rc, t == c) for t in range(nc) for c in range(t, nc)],
                     states)
        for c, (_, l) in enumerate(states):
            o_ref[pl.ds(q0 + c * rc, rc), :] = _rms(acc_scr[c] / l, gn_ref[...]).astype(BF16)
        return 0

    lax.fori_loop(0, seq // tq, q_loop, 0)


def mla_attention(q, k, v, head_gain, *, batch, seq, nc=2, rc=512):
    t = q.shape[0]
    return pl.pallas_call(
        functools.partial(_mla_attn_body, seq=seq, nc=nc, rc=rc),
        grid=(batch, MLA_HEADS),
        in_specs=[
            pl.BlockSpec((seq, MLA_QK_PAD), lambda b, h: (b, h)),
            pl.BlockSpec((seq, MLA_QK_PAD), lambda b, h: (b, h)),
            pl.BlockSpec((seq, HEAD_DIM), lambda b, h: (b, h)),
            pl.BlockSpec((1, HEAD_DIM), lambda b, h: (0, h)),
        ],
        out_specs=pl.BlockSpec((seq, HEAD_DIM), lambda b, h: (b, h)),
        out_shape=jax.ShapeDtypeStruct((t, MLA_HEADS * HEAD_DIM), BF16),
        scratch_shapes=[pltpu.VMEM((nc, rc, HEAD_DIM), F32)],
        compiler_params=_cparams(2),
        name="mla_attention",
    )(q, k, v, head_gain.reshape(1, -1))


def _sb_attn_body(q_ref, k_ref, v_ref, gn_ref, o_ref, qb_scr, kb_scr, vb_scr, acc_scr,
                  *, seq, nc, rc, scale):
    tq = nc * rc
    tk = SB_KEYS
    sub_per_q = tq // tk
    qb_scr[...] = (q_ref[...] * scale).astype(BF16)
    kb_scr[...] = k_ref[...].astype(BF16)
    vb_scr[...] = v_ref[...].astype(BF16)
    row = lax.broadcasted_iota(jnp.int32, (rc, tk), 0)
    col = lax.broadcasted_iota(jnp.int32, (rc, tk), 1)
    urow = lax.broadcasted_iota(jnp.int32, (tk, tk), 0)
    ucol = lax.broadcasted_iota(jnp.int32, (tk, tk), 1)
    later_keys = (urow > ucol).astype(BF16)

    def run(q0, jobs, rests):
        starts = [pl.multiple_of(ks * tk, tk) for _, ks, _ in jobs]
        zs = [_dot_nt(qb_scr[pl.ds(q0 + c * rc, rc), :], kb_scr[pl.ds(k0, tk), :])
              for (c, _, _), k0 in zip(jobs, starts)]
        parts = []
        for (c, _, shift), z in zip(jobs, zs):
            hi = jnp.maximum(z, 0.0)
            lo = jnp.minimum(z, 0.0)
            t = jnp.log(1.0 + jnp.exp2(lo - hi)) * LOG2E
            sp = hi + t
            ls = lo - t
            strict = None if shift is None else col < row + shift
            if strict is not None:
                sp = jnp.where(strict, sp, 0.0)
            parts.append((ls, sp[:, 0:1], _dot(sp.astype(BF16), later_keys), strict))
        rests = list(rests)
        for (c, _, _), k0, (ls, sp_first, later, strict) in zip(jobs, starts, parts):
            a = jnp.exp2(ls - later - rests[c])
            if strict is not None:
                a = jnp.where(strict, a, 0.0)
            acc_scr[c] += _dot(a.astype(BF16), vb_scr[pl.ds(k0, tk), :])
            rests[c] = rests[c] + later[:, 0:1] + sp_first
        return tuple(rests)

    def q_loop(qi, _):
        q0 = pl.multiple_of(qi * tq, tq)
        for c in range(nc):
            acc_scr[c] = jnp.zeros((rc, HEAD_DIM), F32)
        diag_jobs = []
        for j in range(sub_per_q - 1, -1, -1):
            for c in range(nc):
                shift = c * rc - j * tk
                if shift + rc > 0:
                    diag_jobs.append((c, qi * sub_per_q + j, shift if shift < tk else None))
        rests = tuple(jnp.zeros((rc, 1), F32) for _ in range(nc))
        for c in range(nc):
            for job in diag_jobs:
                if job[0] == c:
                    rests = run(q0, [job], rests)

        def kv_loop(i, rests):
            ks = qi * sub_per_q - 1 - SB_SUBS * i
            for c in range(nc):
                for t in range(SB_SUBS):
                    rests = run(q0, [(c, ks - t, None)], rests)
            return rests

        lax.fori_loop(0, qi * (sub_per_q // SB_SUBS), kv_loop, rests)
        for c in range(nc):
            o_ref[pl.ds(q0 + c * rc, rc), :] = _rms(acc_scr[c], gn_ref[...]).astype(BF16)
        return 0

    lax.fori_loop(0, seq // tq, q_loop, 0)


def sb_attention(proj, head_gain, *, batch, seq, nc=2, rc=512):
    assert (nc * rc // SB_KEYS) % SB_SUBS == 0
    t = proj.shape[0]
    scale = float(HEAD_DIM ** -0.5) * LOG2E
    gain_col0 = MLA_HEADS
    return pl.pallas_call(
        functools.partial(_sb_attn_body, seq=seq, nc=nc, rc=rc, scale=scale),
        grid=(batch, SB_HEADS),
        in_specs=[
            pl.BlockSpec((seq, HEAD_DIM), lambda b, h: (b, SB_COL0 + h)),
            pl.BlockSpec((seq, HEAD_DIM), lambda b, h: (b, SB_COL0 + SB_HEADS + h)),
            pl.BlockSpec((seq, HEAD_DIM), lambda b, h: (b, SB_COL0 + 2 * SB_HEADS + h)),
            pl.BlockSpec((1, HEAD_DIM), lambda b, h: (0, gain_col0 + h)),
        ],
        out_specs=pl.BlockSpec((seq, HEAD_DIM), lambda b, h: (b, h)),
        out_shape=jax.ShapeDtypeStruct((t, SB_W), BF16),
        scratch_shapes=[pltpu.VMEM((seq, HEAD_DIM), BF16),
                        pltpu.VMEM((seq, HEAD_DIM), BF16),
                        pltpu.VMEM((seq, HEAD_DIM), BF16),
                        pltpu.VMEM((nc, rc, HEAD_DIM), F32)],
        compiler_params=_cparams(2),
        name="sb_attention",
    )(proj, proj, proj, head_gain.reshape(1, -1))


def _dil_attn_body(q_ref, k_ref, v_ref, slope_ref, gn_ref, o_ref, *scr, seq, scale):
    blk = DIL_BACK
    npat = len(DIL_PATTERNS)
    lse_scr, out_scr = scr[:npat], scr[npat:]
    row = lax.broadcasted_iota(jnp.int32, (blk, blk), 0)
    col = lax.broadcasted_iota(jnp.int32, (blk, blk), 1)
    dist_cur = (row - col).astype(F32)
    dist_prev = dist_cur + float(blk)
    ok_cur = row >= col
    ok_prev_band = row <= col
    slope = slope_ref[...]

    for p, (window, d) in enumerate(DIL_PATTERNS):
        assert window // d == blk
        nblk = seq // d // blk
        assert nblk & (nblk - 1) == 0
        bias_cur = -(slope * (float(d) * LOG2E)) * dist_cur
        bias_prev = -(slope * (float(d) * LOG2E)) * dist_prev

        def group(gi, _, p=p, d=d, nblk=nblk, bias_cur=bias_cur, bias_prev=bias_prev):
            def rows(idx, back):
                r = lax.shift_right_logical(idx, nblk.bit_length() - 1)
                nb = lax.bitwise_and(idx, nblk - 1)
                b = jnp.maximum(nb - back, 0)
                start = r + b * (d * blk)
                if d == 1:
                    return pl.ds(pl.multiple_of(start, blk), blk)
                return pl.ds(start, blk, stride=d)

            idxs = [gi * DIL_GROUP + g for g in range(DIL_GROUP)]
            k_own = [k_ref[rows(idx, 0), :].astype(BF16) for idx in idxs]
            k_before = [k_ref[rows(idxs[0], 1), :].astype(BF16)] + k_own[:-1]
            scores = []
            for idx, kc, kp in zip(idxs, k_own, k_before):
                q = (q_ref[rows(idx, 0), :] * scale).astype(BF16)
                scores.append((_dot_nt(q, kc), _dot_nt(q, kp)))
            probs = []
            for idx, (s_cur, s_prev) in zip(idxs, scores):
                nb = lax.bitwise_and(idx, nblk - 1)
                s_cur = jnp.where(ok_cur, s_cur + bias_cur, NEG_BIG)
                no_prev = jnp.where(nb > 0, 0.0, NEG_BIG)
                s_prev = jnp.where(ok_prev_band, s_prev + bias_prev + no_prev, NEG_BIG)
                m_blk = jnp.max(jnp.maximum(s_cur, s_prev), axis=-1, keepdims=True)
                p_cur = jnp.exp2(s_cur - m_blk)
                p_prev = jnp.exp2(s_prev - m_blk)
                l_blk = jnp.broadcast_to(jnp.sum(p_cur + p_prev, axis=-1, keepdims=True),
                                         (blk, HEAD_DIM))
                lse_scr[p][rows(idx, 0), :] = m_blk + jnp.log(l_blk) * LOG2E
                probs.append((p_cur.astype(BF16), p_prev.astype(BF16), l_blk))
            v_own = [v_ref[rows(idx, 0), :].astype(BF16) for idx in idxs]
            v_before = [v_ref[rows(idxs[0], 1), :].astype(BF16)] + v_own[:-1]
            for idx, (p_cur, p_prev, l_blk), vc, vp in zip(idxs, probs, v_own, v_before):
                out_scr[p][rows(idx, 0), :] = (_dot(p_cur, vc) + _dot(p_prev, vp)) / l_blk
            return 0

        lax.fori_loop(0, d * nblk // DIL_GROUP, group, 0)

    def finish(i, _):
        rs = pl.ds(pl.multiple_of(i * blk, blk), blk)
        lses = [lse_scr[p][rs, :] for p in range(npat)]
        top = functools.reduce(jnp.maximum, lses)
        den = jnp.zeros((blk, HEAD_DIM), F32)
        num = jnp.zeros((blk, HEAD_DIM), F32)
        for p in range(npat):
            w = jnp.exp2(lses[p] - top)
            den = den + w
            num = num + out_scr[p][rs, :] * w
        o_ref[rs, :] = _rms(num / den, gn_ref[...]).astype(BF16)
        return 0

    lax.fori_loop(0, seq // blk, finish, 0, unroll=2)


def dil_attention(proj, slopes, head_gain, *, batch, seq):
    t = proj.shape[0]
    scale = float(HEAD_DIM ** -0.5) * LOG2E
    gain_col0 = MLA_HEADS + SB_HEADS
    return pl.pallas_call(
        functools.partial(_dil_attn_body, seq=seq, scale=scale),
        grid=(batch, DIL_HEADS),
        in_specs=[
            pl.BlockSpec((seq, HEAD_DIM), lambda b, h: (b, DIL_COL0 + h)),
            pl.BlockSpec((seq, HEAD_DIM), lambda b, h: (b, DIL_COL0 + DIL_HEADS + h)),
            pl.BlockSpec((seq, HEAD_DIM), lambda b, h: (b, DIL_COL0 + 2 * DIL_HEADS + h)),
            pl.BlockSpec((None, 1, LANES), lambda b, h: (h, 0, 0)),
            pl.BlockSpec((1, HEAD_DIM), lambda b, h: (0, gain_col0 + h)),
        ],
        out_specs=pl.BlockSpec((seq, HEAD_DIM), lambda b, h: (b, h)),
        out_shape=jax.ShapeDtypeStruct((t, DIL_W), BF16),
        scratch_shapes=[pltpu.VMEM((seq, HEAD_DIM), F32)] * (2 * len(DIL_PATTERNS)),
        compiler_params=_cparams(2),
        name="dil_attention",
    )(proj, proj, proj, slopes, head_gain.reshape(1, -1))


def _outproj_body(x_ref, o1_ref, o2_ref, o3_ref, w1_ref, w2_ref, w3_ref, out_ref):
    acc = _dot(o1_ref[...], w1_ref[...])
    acc = acc + _dot(o2_ref[...], w2_ref[...])
    acc = acc + _dot(o3_ref[...], w3_ref[...])
    out_ref[...] = x_ref[...] + acc


def out_proj(x, o_mla, o_sb, o_dil, w1, w2, w3, *, tm=512):
    t, d = x.shape
    full = lambda a: pl.BlockSpec(a.shape, lambda i: (0, 0))
    rows = lambda a: pl.BlockSpec((tm, a.shape[1]), lambda i: (i, 0))
    return pl.pallas_call(
        _outproj_body,
        grid=(t // tm,),
        in_specs=[rows(x), rows(o_mla), rows(o_sb), rows(o_dil), full(w1), full(w2), full(w3)],
        out_specs=rows(x),
        out_shape=jax.ShapeDtypeStruct((t, d), F32),
        compiler_params=_cparams(1),
        name="out_proj",
    )(x, o_mla, o_sb, o_dil, w1, w2, w3)


def _rope_tables(seq):
    inv = ROPE_THETA ** (-jnp.arange(0, MLA_ROPE, 2, dtype=F32) / MLA_ROPE)
    ang = jnp.arange(seq, dtype=F32)[:, None] * inv[None, :]
    cos, sin = jnp.cos(ang), jnp.sin(ang)
    reps = LANES // MLA_ROPE
    cos_t = jnp.tile(jnp.concatenate([cos, cos], axis=1), (1, reps))
    sin_t = jnp.tile(jnp.concatenate([-sin, sin], axis=1), (1, reps))
    return cos_t, sin_t


def _prep_w_in(w_in):
    d = w_in.shape[0]
    z = lambda n: jnp.zeros((d, n), w_in.dtype)
    return jnp.concatenate(
        [w_in[:, :MLA_IN], z(MLA_IN_PAD - MLA_IN), w_in[:, MLA_IN:], z(LANES)], axis=1).astype(BF16)


def _prep_w_uq(w):
    r = w.shape[0]
    w = w.reshape(r, MLA_HEADS, MLA_NOPE + MLA_ROPE)
    w = jnp.pad(w, ((0, 0), (0, 0), (0, MLA_QK_PAD - MLA_NOPE - MLA_ROPE)))
    return w.reshape(r, MLA_HEADS * MLA_QK_PAD).astype(BF16)


def _prep_w_ukv(w):
    r = w.shape[0]
    w = w.reshape(r, MLA_HEADS, 2, HEAD_DIM).transpose(0, 2, 1, 3)
    return w.reshape(r, 2 * MLA_HEADS * HEAD_DIM).astype(BF16)


def kernel(x, ffn1_norm, ffn1_w_gu, ffn1_w_down, mix_norm, w_in, mla_q_norm, w_mla_uq, mla_kv_norm, w_mla_ukv, head_out_norm, w_out, ffn2_norm, ffn2_w_gu, ffn2_w_down, final_norm_g):
    batch, seq, d = x.shape
    depth = ffn1_norm.shape[0]
    cos_t, sin_t = _rope_tables(seq)
    slopes = 2.0 ** (-ALIBI_MAX_EXP * jnp.arange(1, DIL_HEADS + 1, dtype=F32) / DIL_HEADS)
    slopes = jnp.broadcast_to(slopes[:, None, None], (DIL_HEADS, 1, LANES))
    w_mla = MLA_HEADS * HEAD_DIM

    xt = x.reshape(batch * seq, d)
    for l in range(depth):
        xt = ffn_block(xt, ffn1_norm[l], ffn1_w_gu[l].astype(BF16),
                       (0.5 * ffn1_w_down[l]).astype(BF16))

        proj = norm_proj(xt, mix_norm[l], _prep_w_in(w_in[l]))
        q, k, v = mla_prep(proj, mla_q_norm[l], mla_kv_norm[l], _prep_w_uq(w_mla_uq[l]),
                           _prep_w_ukv(w_mla_ukv[l]), cos_t, sin_t, seq=seq)
        o_mla = mla_attention(q, k, v, head_out_norm[l], batch=batch, seq=seq)
        o_sb = sb_attention(proj, head_out_norm[l], batch=batch, seq=seq)
        o_dil = dil_attention(proj, slopes, head_out_norm[l], batch=batch, seq=seq)
        wo = w_out[l].astype(BF16)
        xt = out_proj(xt, o_mla, o_sb, o_dil, wo[:w_mla], wo[w_mla:w_mla + SB_W], wo[w_mla + SB_W:])

        xt = ffn_block(xt, ffn2_norm[l], ffn2_w_gu[l].astype(BF16),
                       (0.5 * ffn2_w_down[l]).astype(BF16),
                       final_norm_g if l == depth - 1 else None)
    return xt.reshape(batch, seq, d)
```

```python
import functools

import jax
import jax.numpy as jnp
from jax import lax
from jax.experimental import pallas as pl
from jax.experimental.pallas import tpu as pltpu

F32 = jnp.float32
BF16 = jnp.bfloat16

EPS = 1e-6
HEAD_DIM = 128
LANES = 128
MLA_HEADS = 6
MLA_Q_LORA = 512
MLA_KV_LORA = 256
MLA_NOPE = 128
MLA_ROPE = 64
MLA_QK_PAD = 256
ROPE_THETA = 10000.0
SB_HEADS = 4
DIL_HEADS = 6
DIL_PATTERNS = ((128, 1), (512, 4), (2048, 16))
DIL_BACK = 128
DIL_GROUP = 8
ALIBI_MAX_EXP = 8.0
NEG_BIG = -1e30
LOG2E = 1.4426950408889634
SB_KEYS = 256
SB_SUBS = 2
SB_DEAD_BITS = 160.0

MLA_IN = MLA_Q_LORA + MLA_KV_LORA + MLA_ROPE
MLA_IN_PAD = 896
SB_W = SB_HEADS * HEAD_DIM
DIL_W = DIL_HEADS * HEAD_DIM
PROJ_HALF = MLA_IN_PAD + 3 * SB_W
assert PROJ_HALF == 3 * DIL_W + LANES
SB_COL0 = MLA_IN_PAD // LANES
DIL_COL0 = PROJ_HALF // LANES

VMEM_LIMIT = 56 * 1024 * 1024


def _cparams(grid_rank):
    return pltpu.CompilerParams(dimension_semantics=("arbitrary",) * grid_rank,
                                vmem_limit_bytes=VMEM_LIMIT)


def _rms(x, g):
    return x * lax.rsqrt(jnp.mean(x * x, axis=-1, keepdims=True) + EPS) * g


def _dot(a, b):
    return jnp.dot(a, b, preferred_element_type=F32)


def _dot_nt(a, b):
    return lax.dot_general(a, b, (((1,), (1,)), ((), ())), preferred_element_type=F32)


def _ffn_body(x_ref, g_ref, wg_ref, wu_ref, wd_ref, *rest, final_norm):
    fg_ref, o_ref, h_scr = rest if final_norm else (None,) + rest

    @pl.when(pl.program_id(1) == 0)
    def _():
        h_scr[...] = _rms(x_ref[...], g_ref[...]).astype(BF16)
        o_ref[...] = x_ref[...]

    h = h_scr[...]
    g = _dot(h, wg_ref[...])
    u = _dot(h, wu_ref[...])
    a = (g * (1.0 / (1.0 + jnp.exp(-g))) * u).astype(BF16)
    o_ref[...] += _dot(a, wd_ref[...])

    if final_norm:
        @pl.when(pl.program_id(1) == pl.num_programs(1) - 1)
        def _():
            o_ref[...] = _rms(o_ref[...], fg_ref[...])


def ffn_block(x, gain, w_gu, w_down_half, final_gain=None, *, tm=512, tf=512):
    t, d = x.shape
    f = w_down_half.shape[0]
    nf = f // tf
    final_norm = final_gain is not None
    row = pl.BlockSpec((1, d), lambda i, j: (0, 0))
    return pl.pallas_call(
        functools.partial(_ffn_body, final_norm=final_norm),
        grid=(t // tm, nf),
        in_specs=[
            pl.BlockSpec((tm, d), lambda i, j: (i, 0)),
            row,
            pl.BlockSpec((d, tf), lambda i, j: (0, j)),
            pl.BlockSpec((d, tf), lambda i, j: (0, j + nf)),
            pl.BlockSpec((tf, d), lambda i, j: (j, 0)),
        ] + ([row] if final_norm else []),
        out_specs=pl.BlockSpec((tm, d), lambda i, j: (i, 0)),
        out_shape=jax.ShapeDtypeStruct((t, d), F32),
        scratch_shapes=[pltpu.VMEM((tm, d), BF16)],
        compiler_params=_cparams(2),
        name="ffn_block",
    )(x, gain.reshape(1, d), w_gu, w_gu, w_down_half,
      *([final_gain.reshape(1, d)] if final_norm else []))


def _normproj_body(x_ref, g_ref, w_ref, o_ref):
    h = _rms(x_ref[...], g_ref[...]).astype(BF16)
    o_ref[...] = _dot(h, w_ref[...])


def norm_proj(x, gain, w, *, tm=512):
    t, d = x.shape
    n = w.shape[1]
    tn = PROJ_HALF
    return pl.pallas_call(
        _normproj_body,
        grid=(n // tn, t // tm),
        in_specs=[
            pl.BlockSpec((tm, d), lambda c, i: (i, 0)),
            pl.BlockSpec((1, d), lambda c, i: (0, 0)),
            pl.BlockSpec((d, tn), lambda c, i: (0, c)),
        ],
        out_specs=pl.BlockSpec((tm, tn), lambda c, i: (i, c)),
        out_shape=jax.ShapeDtypeStruct((t, n), F32),
        compiler_params=_cparams(2),
        name="norm_proj",
    )(x, gain.reshape(1, d), w)


def _mla_prep_body(a_ref, gq_ref, gkv_ref, wuq_ref, wukv_ref, cos_ref, sin_ref,
                   q_ref, k_ref, v_ref, *, scale):
    a = a_ref[...]
    qn = _rms(a[:, :MLA_Q_LORA], gq_ref[...]).astype(BF16)
    q = _dot(qn, wuq_ref[...]) * scale
    kvn = _rms(a[:, MLA_Q_LORA:MLA_Q_LORA + MLA_KV_LORA], gkv_ref[...]).astype(BF16)
    kv = _dot(kvn, wukv_ref[...])

    cos_t = cos_ref[...]
    sin_t = sin_ref[...]
    lane = lax.broadcasted_iota(jnp.int32, cos_t.shape, 1)
    first_half = (lane & (MLA_ROPE // 2)) == 0

    def rope(x):
        swapped = jnp.where(first_half,
                            pltpu.roll(x, LANES - MLA_ROPE // 2, 1),
                            pltpu.roll(x, MLA_ROPE // 2, 1))
        return x * cos_t + swapped * sin_t

    k_rope = rope(a[:, MLA_Q_LORA + MLA_KV_LORA:]).astype(BF16)
    nk = MLA_HEADS * MLA_NOPE
    for h in range(MLA_HEADS):
        c0 = h * MLA_QK_PAD
        q_ref[:, c0:c0 + LANES] = q[:, c0:c0 + LANES].astype(BF16)
        q_ref[:, c0 + LANES:c0 + 2 * LANES] = rope(q[:, c0 + LANES:c0 + 2 * LANES]).astype(BF16)
        k_ref[:, c0:c0 + LANES] = kv[:, h * MLA_NOPE:(h + 1) * MLA_NOPE].astype(BF16)
        k_ref[:, c0 + LANES:c0 + 2 * LANES] = k_rope
    v_ref[...] = kv[:, nk:].astype(BF16)


def mla_prep(proj, gq, gkv, wuq, wukv, cos_t, sin_t, *, seq, tm=512):
    t = proj.shape[0]
    nseq = seq // tm
    wq = MLA_HEADS * MLA_QK_PAD
    wv = MLA_HEADS * HEAD_DIM
    scale = float((MLA_NOPE + MLA_ROPE) ** -0.5) * LOG2E
    return pl.pallas_call(
        functools.partial(_mla_prep_body, scale=scale),
        grid=(t // tm,),
        in_specs=[
            pl.BlockSpec((tm, MLA_IN_PAD), lambda i: (i, 0)),
            pl.BlockSpec((1, MLA_Q_LORA), lambda i: (0, 0)),
            pl.BlockSpec((1, MLA_KV_LORA), lambda i: (0, 0)),
            pl.BlockSpec((MLA_Q_LORA, wq), lambda i: (0, 0)),
            pl.BlockSpec((MLA_KV_LORA, 2 * wv), lambda i: (0, 0)),
            pl.BlockSpec((tm, LANES), lambda i: (i % nseq, 0)),
            pl.BlockSpec((tm, LANES), lambda i: (i % nseq, 0)),
        ],
        out_specs=[
            pl.BlockSpec((tm, wq), lambda i: (i, 0)),
            pl.BlockSpec((tm, wq), lambda i: (i, 0)),
            pl.BlockSpec((tm, wv), lambda i: (i, 0)),
        ],
        out_shape=[
            jax.ShapeDtypeStruct((t, wq), BF16),
            jax.ShapeDtypeStruct((t, wq), BF16),
            jax.ShapeDtypeStruct((t, wv), BF16),
        ],
        compiler_params=_cparams(1),
        name="mla_prep",
    )(proj, gq.reshape(1, -1), gkv.reshape(1, -1), wuq, wukv, cos_t, sin_t)


def _mla_attn_body(q_ref, k_ref, v_ref, gn_ref, o_ref, acc_scr, *, seq, nc, rc):
    tq = nc * rc
    row = lax.broadcasted_iota(jnp.int32, (rc, rc), 0)
    col = lax.broadcasted_iota(jnp.int32, (rc, rc), 1)
    causal = col <= row

    def run(q0, jobs, states):
        scores = [_dot_nt(q_ref[pl.ds(q0 + c * rc, rc), :], k_ref[pl.ds(k0, rc), :])
                  for c, k0, _ in jobs]
        states = list(states)
        for (c, k0, diag), s in zip(jobs, scores):
            if diag:
                s = jnp.where(causal, s, NEG_BIG)
            m, l = states[c]
            m_new = jnp.maximum(m, jnp.max(s, axis=1, keepdims=True))
            alpha = jnp.exp2(m - m_new)
            p = jnp.exp2(s - m_new)
            l = alpha * l + jnp.sum(p, axis=1, keepdims=True)
            acc_scr[c] = alpha * acc_scr[c] + _dot(p.astype(BF16), v_ref[pl.ds(k0, rc), :])
            states[c] = (m_new, l)
        return tuple(states)

    def q_loop(qi, _):
        q0 = pl.multiple_of(qi * tq, tq)
        for c in range(nc):
            acc_scr[c] = jnp.zeros((rc, HEAD_DIM), F32)
        init = tuple((jnp.full((rc, 1), NEG_BIG, F32), jnp.zeros((rc, 1), F32))
                     for _ in range(nc))

        def kv_loop(i, states):
            k0 = pl.multiple_of(i * tq, tq)
            return run(q0, [(c, k0 + t * rc, False) for t in range(nc) for c in range(nc)],
                       states)

        states = lax.fori_loop(0, qi, kv_loop, init)
        states = run(q0, [(c, q0 + t * rc, t == c) for t in range(nc) for c in range(t, nc)],
                     states)
        for c, (_, l) in enumerate(states):
            o_ref[pl.ds(q0 + c * rc, rc), :] = _rms(acc_scr[c] / l, gn_ref[...]).astype(BF16)
        return 0

    lax.fori_loop(0, seq // tq, q_loop, 0)


def mla_attention(q, k, v, head_gain, *, batch, seq, nc=2, rc=512):
    t = q.shape[0]
    return pl.pallas_call(
        functools.partial(_mla_attn_body, seq=seq, nc=nc, rc=rc),
        grid=(batch, MLA_HEADS),
        in_specs=[
            pl.BlockSpec((seq, MLA_QK_PAD), lambda b, h: (b, h)),
            pl.BlockSpec((seq, MLA_QK_PAD), lambda b, h: (b, h)),
            pl.BlockSpec((seq, HEAD_DIM), lambda b, h: (b, h)),
            pl.BlockSpec((1, HEAD_DIM), lambda b, h: (0, h)),
        ],
        out_specs=pl.BlockSpec((seq, HEAD_DIM), lambda b, h: (b, h)),
        out_shape=jax.ShapeDtypeStruct((t, MLA_HEADS * HEAD_DIM), BF16),
        scratch_shapes=[pltpu.VMEM((nc, rc, HEAD_DIM), F32)],
        compiler_params=_cparams(2),
        name="mla_attention",
    )(q, k, v, head_gain.reshape(1, -1))


def _sb_attn_body(q_ref, k_ref, v_ref, gn_ref, o_ref, qb_scr, kb_scr, vb_scr, acc_scr,
                  *, seq, rc, scale):
    tk = SB_KEYS
    own = rc // tk
    qb_scr[...] = (q_ref[...] * scale).astype(BF16)
    kb_scr[...] = k_ref[...].astype(BF16)
    vb_scr[...] = v_ref[...].astype(BF16)
    row = lax.broadcasted_iota(jnp.int32, (rc, tk), 0)
    col = lax.broadcasted_iota(jnp.int32, (rc, tk), 1)
    urow = lax.broadcasted_iota(jnp.int32, (tk, tk), 0)
    ucol = lax.broadcasted_iota(jnp.int32, (tk, tk), 1)
    later_keys = (urow > ucol).astype(BF16)

    def absorb(q0, ks, rest, shift):
        k0 = pl.multiple_of(ks * tk, tk)
        z = _dot_nt(qb_scr[pl.ds(q0, rc), :], kb_scr[pl.ds(k0, tk), :])
        hi = jnp.maximum(z, 0.0)
        lo = jnp.minimum(z, 0.0)
        t = jnp.log(1.0 + jnp.exp2(lo - hi)) * LOG2E
        sp = hi + t
        ls = lo - t
        strict = None if shift is None else col < row + shift
        if strict is not None:
            sp = jnp.where(strict, sp, 0.0)
        later = _dot(sp.astype(BF16), later_keys)
        a = jnp.exp2(ls - later - rest)
        if strict is not None:
            a = jnp.where(strict, a, 0.0)
        acc_scr[...] += _dot(a.astype(BF16), vb_scr[pl.ds(k0, tk), :])
        return rest + later[:, 0:1] + sp[:, 0:1]

    def chunk(ci, _):
        q0 = pl.multiple_of(ci * rc, rc)
        acc_scr[...] = jnp.zeros((rc, HEAD_DIM), F32)
        rest = jnp.zeros((rc, 1), F32)
        for j in range(own - 1, -1, -1):
            rest = absorb(q0, ci * own + j, rest, -j * tk)

        def alive(state):
            ks, rest = state
            return jnp.logical_and(ks >= 0, jnp.min(rest) < SB_DEAD_BITS)

        def walk(state):
            ks, rest = state
            for t in range(SB_SUBS):
                rest = absorb(q0, ks - t, rest, None)
            return ks - SB_SUBS, rest

        lax.while_loop(alive, walk, (ci * own - 1, rest))
        o_ref[pl.ds(q0, rc), :] = _rms(acc_scr[...], gn_ref[...]).astype(BF16)
        return 0

    lax.fori_loop(0, seq // rc, chunk, 0)


def sb_attention(proj, head_gain, *, batch, seq, rc=512):
    assert (rc // SB_KEYS) % SB_SUBS == 0
    t = proj.shape[0]
    scale = float(HEAD_DIM ** -0.5) * LOG2E
    gain_col0 = MLA_HEADS
    return pl.pallas_call(
        functools.partial(_sb_attn_body, seq=seq, rc=rc, scale=scale),
        grid=(batch, SB_HEADS),
        in_specs=[
            pl.BlockSpec((seq, HEAD_DIM), lambda b, h: (b, SB_COL0 + h)),
            pl.BlockSpec((seq, HEAD_DIM), lambda b, h: (b, SB_COL0 + SB_HEADS + h)),
            pl.BlockSpec((seq, HEAD_DIM), lambda b, h: (b, SB_COL0 + 2 * SB_HEADS + h)),
            pl.BlockSpec((1, HEAD_DIM), lambda b, h: (0, gain_col0 + h)),
        ],
        out_specs=pl.BlockSpec((seq, HEAD_DIM), lambda b, h: (b, h)),
        out_shape=jax.ShapeDtypeStruct((t, SB_W), BF16),
        scratch_shapes=[pltpu.VMEM((seq, HEAD_DIM), BF16),
                        pltpu.VMEM((seq, HEAD_DIM), BF16),
                        pltpu.VMEM((seq, HEAD_DIM), BF16),
                        pltpu.VMEM((rc, HEAD_DIM), F32)],
        compiler_params=_cparams(2),
        name="sb_attention",
    )(proj, proj, proj, head_gain.reshape(1, -1))


def _dil_attn_body(q_ref, k_ref, v_ref, slope_ref, gn_ref, o_ref, *scr, seq, scale):
    blk = DIL_BACK
    npat = len(DIL_PATTERNS)
    lse_scr, out_scr = scr[:npat], scr[npat:]
    row = lax.broadcasted_iota(jnp.int32, (blk, blk), 0)
    col = lax.broadcasted_iota(jnp.int32, (blk, blk), 1)
    dist_cur = (row - col).astype(F32)
    dist_prev = dist_cur + float(blk)
    ok_cur = row >= col
    ok_prev_band = row <= col
    slope = slope_ref[...]

    for p, (window, d) in enumerate(DIL_PATTERNS):
        assert window // d == blk
        nblk = seq // d // blk
        assert nblk & (nblk - 1) == 0
        bias_cur = -(slope * (float(d) * LOG2E)) * dist_cur
        bias_prev = -(slope * (float(d) * LOG2E)) * dist_prev

        def group(gi, _, p=p, d=d, nblk=nblk, bias_cur=bias_cur, bias_prev=bias_prev):
            def rows(idx, back):
                r = lax.shift_right_logical(idx, nblk.bit_length() - 1)
                nb = lax.bitwise_and(idx, nblk - 1)
                b = jnp.maximum(nb - back, 0)
                start = r + b * (d * blk)
                if d == 1:
                    return pl.ds(pl.multiple_of(start, blk), blk)
                return pl.ds(start, blk, stride=d)

            idxs = [gi * DIL_GROUP + g for g in range(DIL_GROUP)]
            k_own = [k_ref[rows(idx, 0), :].astype(BF16) for idx in idxs]
            k_before = [k_ref[rows(idxs[0], 1), :].astype(BF16)] + k_own[:-1]
            scores = []
            for idx, kc, kp in zip(idxs, k_own, k_before):
                q = (q_ref[rows(idx, 0), :] * scale).astype(BF16)
                scores.append((_dot_nt(q, kc), _dot_nt(q, kp)))
            probs = []
            for idx, (s_cur, s_prev) in zip(idxs, scores):
                nb = lax.bitwise_and(idx, nblk - 1)
                s_cur = jnp.where(ok_cur, s_cur + bias_cur, NEG_BIG)
                no_prev = jnp.where(nb > 0, 0.0, NEG_BIG)
                s_prev = jnp.where(ok_prev_band, s_prev + bias_prev + no_prev, NEG_BIG)
                m_blk = jnp.max(jnp.maximum(s_cur, s_prev), axis=-1, keepdims=True)
                p_cur = jnp.exp2(s_cur - m_blk)
                p_prev = jnp.exp2(s_prev - m_blk)
                l_blk = jnp.broadcast_to(jnp.sum(p_cur + p_prev, axis=-1, keepdims=True),
                                         (blk, HEAD_DIM))
                lse_scr[p][rows(idx, 0), :] = m_blk + jnp.log(l_blk) * LOG2E
                probs.append((p_cur.astype(BF16), p_prev.astype(BF16), l_blk))
            v_own = [v_ref[rows(idx, 0), :].astype(BF16) for idx in idxs]
            v_before = [v_ref[rows(idxs[0], 1), :].astype(BF16)] + v_own[:-1]
            for idx, (p_cur, p_prev, l_blk), vc, vp in zip(idxs, probs, v_own, v_before):
                out_scr[p][rows(idx, 0), :] = (_dot(p_cur, vc) + _dot(p_prev, vp)) / l_blk
            return 0

        lax.fori_loop(0, d * nblk // DIL_GROUP, group, 0)

    def finish(i, _):
        rs = pl.ds(pl.multiple_of(i * blk, blk), blk)
        lses = [lse_scr[p][rs, :] for p in range(npat)]
        top = functools.reduce(jnp.maximum, lses)
        den = jnp.zeros((blk, HEAD_DIM), F32)
        num = jnp.zeros((blk, HEAD_DIM), F32)
        for p in range(npat):
            w = jnp.exp2(lses[p] - top)
            den = den + w
            num = num + out_scr[p][rs, :] * w
        o_ref[rs, :] = _rms(num / den, gn_ref[...]).astype(BF16)
        return 0

    lax.fori_loop(0, seq // blk, finish, 0, unroll=2)


def dil_attention(proj, slopes, head_gain, *, batch, seq):
    t = proj.shape[0]
    scale = float(HEAD_DIM ** -0.5) * LOG2E
    gain_col0 = MLA_HEADS + SB_HEADS
    return pl.pallas_call(
        functools.partial(_dil_attn_body, seq=seq, scale=scale),
        grid=(batch, DIL_HEADS),
        in_specs=[
            pl.BlockSpec((seq, HEAD_DIM), lambda b, h: (b, DIL_COL0 + h)),
            pl.BlockSpec((seq, HEAD_DIM), lambda b, h: (b, DIL_COL0 + DIL_HEADS + h)),
            pl.BlockSpec((seq, HEAD_DIM), lambda b, h: (b, DIL_COL0 + 2 * DIL_HEADS + h)),
            pl.BlockSpec((None, 1, LANES), lambda b, h: (h, 0, 0)),
            pl.BlockSpec((1, HEAD_DIM), lambda b, h: (0, gain_col0 + h)),
        ],
        out_specs=pl.BlockSpec((seq, HEAD_DIM), lambda b, h: (b, h)),
        out_shape=jax.ShapeDtypeStruct((t, DIL_W), BF16),
        scratch_shapes=[pltpu.VMEM((seq, HEAD_DIM), F32)] * (2 * len(DIL_PATTERNS)),
        compiler_params=_cparams(2),
        name="dil_attention",
    )(proj, proj, proj, slopes, head_gain.reshape(1, -1))


def _outproj_body(x_ref, o1_ref, o2_ref, o3_ref, w1_ref, w2_ref, w3_ref, out_ref):
    acc = _dot(o1_ref[...], w1_ref[...])
    acc = acc + _dot(o2_ref[...], w2_ref[...])
    acc = acc + _dot(o3_ref[...], w3_ref[...])
    out_ref[...] = x_ref[...] + acc


def out_proj(x, o_mla, o_sb, o_dil, w1, w2, w3, *, tm=512):
    t, d = x.shape
    full = lambda a: pl.BlockSpec(a.shape, lambda i: (0, 0))
    rows = lambda a: pl.BlockSpec((tm, a.shape[1]), lambda i: (i, 0))
    return pl.pallas_call(
        _outproj_body,
        grid=(t // tm,),
        in_specs=[rows(x), rows(o_mla), rows(o_sb), rows(o_dil), full(w1), full(w2), full(w3)],
        out_specs=rows(x),
        out_shape=jax.ShapeDtypeStruct((t, d), F32),
        compiler_params=_cparams(1),
        name="out_proj",
    )(x, o_mla, o_sb, o_dil, w1, w2, w3)


def _rope_tables(seq):
    inv = ROPE_THETA ** (-jnp.arange(0, MLA_ROPE, 2, dtype=F32) / MLA_ROPE)
    ang = jnp.arange(seq, dtype=F32)[:, None] * inv[None, :]
    cos, sin = jnp.cos(ang), jnp.sin(ang)
    reps = LANES // MLA_ROPE
    cos_t = jnp.tile(jnp.concatenate([cos, cos], axis=1), (1, reps))
    sin_t = jnp.tile(jnp.concatenate([-sin, sin], axis=1), (1, reps))
    return cos_t, sin_t


def _prep_w_in(w_in):
    d = w_in.shape[0]
    z = lambda n: jnp.zeros((d, n), w_in.dtype)
    return jnp.concatenate(
        [w_in[:, :MLA_IN], z(MLA_IN_PAD - MLA_IN), w_in[:, MLA_IN:], z(LANES)], axis=1).astype(BF16)


def _prep_w_uq(w):
    r = w.shape[0]
    w = w.reshape(r, MLA_HEADS, MLA_NOPE + MLA_ROPE)
    w = jnp.pad(w, ((0, 0), (0, 0), (0, MLA_QK_PAD - MLA_NOPE - MLA_ROPE)))
    return w.reshape(r, MLA_HEADS * MLA_QK_PAD).astype(BF16)


def _prep_w_ukv(w):
    r = w.shape[0]
    w = w.reshape(r, MLA_HEADS, 2, HEAD_DIM).transpose(0, 2, 1, 3)
    return w.reshape(r, 2 * MLA_HEADS * HEAD_DIM).astype(BF16)


def kernel(x, ffn1_norm, ffn1_w_gu, ffn1_w_down, mix_norm, w_in, mla_q_norm, w_mla_uq, mla_kv_norm, w_mla_ukv, head_out_norm, w_out, ffn2_norm, ffn2_w_gu, ffn2_w_down, final_norm_g):
    batch, seq, d = x.shape
    depth = ffn1_norm.shape[0]
    cos_t, sin_t = _rope_tables(seq)
    slopes = 2.0 ** (-ALIBI_MAX_EXP * jnp.arange(1, DIL_HEADS + 1, dtype=F32) / DIL_HEADS)
    slopes = jnp.broadcast_to(slopes[:, None, None], (DIL_HEADS, 1, LANES))
    w_mla = MLA_HEADS * HEAD_DIM

    xt = x.reshape(batch * seq, d)
    for l in range(depth):
        xt = ffn_block(xt, ffn1_norm[l], ffn1_w_gu[l].astype(BF16),
                       (0.5 * ffn1_w_down[l]).astype(BF16))

        proj = norm_proj(xt, mix_norm[l], _prep_w_in(w_in[l]))
        q, k, v = mla_prep(proj, mla_q_norm[l], mla_kv_norm[l], _prep_w_uq(w_mla_uq[l]),
                           _prep_w_ukv(w_mla_ukv[l]), cos_t, sin_t, seq=seq)
        o_mla = mla_attention(q, k, v, head_out_norm[l], batch=batch, seq=seq)
        o_sb = sb_attention(proj, head_out_norm[l], batch=batch, seq=seq)
        o_dil = dil_attention(proj, slopes, head_out_norm[l], batch=batch, seq=seq)
        wo = w_out[l].astype(BF16)
        xt = out_proj(xt, o_mla, o_sb, o_dil, wo[:w_mla], wo[w_mla:w_mla + SB_W], wo[w_mla + SB_W:])

        xt = ffn_block(xt, ffn2_norm[l], ffn2_w_gu[l].astype(BF16),
                       (0.5 * ffn2_w_down[l]).astype(BF16),
                       final_norm_g if l == depth - 1 else None)
    return xt.reshape(batch, seq, d)
```

```python
import functools

import jax
import jax.numpy as jnp
from jax import lax
from jax.experimental import pallas as pl
from jax.experimental.pallas import tpu as pltpu

F32 = jnp.float32
BF16 = jnp.bfloat16

EPS = 1e-6
HEAD_DIM = 128
LANES = 128
MLA_HEADS = 6
MLA_Q_LORA = 512
MLA_KV_LORA = 256
MLA_NOPE = 128
MLA_ROPE = 64
MLA_QK_PAD = 256
ROPE_THETA = 10000.0
SB_HEADS = 4
DIL_HEADS = 6
DIL_PATTERNS = ((128, 1), (512, 4), (2048, 16))
DIL_BACK = 128
DIL_GROUP = 8
ALIBI_MAX_EXP = 8.0
NEG_BIG = -1e30
LOG2E = 1.4426950408889634
SB_KEYS = 256
FFN_TF = 512
SB_SUBS = 1
SB_DEAD_BITS = 160.0

MLA_IN = MLA_Q_LORA + MLA_KV_LORA + MLA_ROPE
MLA_IN_PAD = 896
SB_W = SB_HEADS * HEAD_DIM
DIL_W = DIL_HEADS * HEAD_DIM
PROJ_HALF = MLA_IN_PAD + 3 * SB_W
assert PROJ_HALF == 3 * DIL_W + LANES
SB_COL0 = MLA_IN_PAD // LANES
DIL_COL0 = PROJ_HALF // LANES

VMEM_LIMIT = 56 * 1024 * 1024


def _cparams(grid_rank):
    return pltpu.CompilerParams(dimension_semantics=("arbitrary",) * grid_rank,
                                vmem_limit_bytes=VMEM_LIMIT)


def _rms(x, g):
    return x * lax.rsqrt(jnp.mean(x * x, axis=-1, keepdims=True) + EPS) * g


def _dot(a, b):
    return jnp.dot(a, b, preferred_element_type=F32)


def _dot_nt(a, b):
    return lax.dot_general(a, b, (((1,), (1,)), ((), ())), preferred_element_type=F32)


def _ffn_body(x_ref, g_ref, wg_ref, wu_ref, wd_ref, *rest, final_norm):
    fg_ref, o_ref, h_scr = rest if final_norm else (None,) + rest

    @pl.when(pl.program_id(1) == 0)
    def _():
        h_scr[...] = _rms(x_ref[...], g_ref[...]).astype(BF16)
        o_ref[...] = x_ref[...]

    h = h_scr[...]
    g = _dot(h, wg_ref[...])
    u = _dot(h, wu_ref[...])
    a = (g * (1.0 / (1.0 + jnp.exp(-g))) * u).astype(BF16)
    o_ref[...] += _dot(a, wd_ref[...])

    if final_norm:
        @pl.when(pl.program_id(1) == pl.num_programs(1) - 1)
        def _():
            o_ref[...] = _rms(o_ref[...], fg_ref[...])


def ffn_block(x, gain, w_gu_tiles, w_down_half, final_gain=None, *, tm=512):
    t, d = x.shape
    tf = w_gu_tiles.shape[2]
    nf = w_gu_tiles.shape[0] // 2
    final_norm = final_gain is not None
    row = pl.BlockSpec((1, d), lambda i, j: (0, 0))
    return pl.pallas_call(
        functools.partial(_ffn_body, final_norm=final_norm),
        grid=(t // tm, nf),
        in_specs=[
            pl.BlockSpec((tm, d), lambda i, j: (i, 0)),
            row,
            pl.BlockSpec((None, d, tf), lambda i, j: (j, 0, 0)),
            pl.BlockSpec((None, d, tf), lambda i, j: (j + nf, 0, 0)),
            pl.BlockSpec((tf, d), lambda i, j: (j, 0)),
        ] + ([row] if final_norm else []),
        out_specs=pl.BlockSpec((tm, d), lambda i, j: (i, 0)),
        out_shape=jax.ShapeDtypeStruct((t, d), F32),
        scratch_shapes=[pltpu.VMEM((tm, d), BF16)],
        compiler_params=_cparams(2),
        name="ffn_block",
    )(x, gain.reshape(1, d), w_gu_tiles, w_gu_tiles, w_down_half,
      *([final_gain.reshape(1, d)] if final_norm else []))


def _normproj_body(x_ref, g_ref, w_ref, o_ref):
    h = _rms(x_ref[...], g_ref[...]).astype(BF16)
    o_ref[...] = _dot(h, w_ref[...])


def norm_proj(x, gain, w, *, tm=512):
    t, d = x.shape
    n = w.shape[1]
    tn = PROJ_HALF
    return pl.pallas_call(
        _normproj_body,
        grid=(n // tn, t // tm),
        in_specs=[
            pl.BlockSpec((tm, d), lambda c, i: (i, 0)),
            pl.BlockSpec((1, d), lambda c, i: (0, 0)),
            pl.BlockSpec((d, tn), lambda c, i: (0, c)),
        ],
        out_specs=pl.BlockSpec((tm, tn), lambda c, i: (i, c)),
        out_shape=jax.ShapeDtypeStruct((t, n), F32),
        compiler_params=_cparams(2),
        name="norm_proj",
    )(x, gain.reshape(1, d), w)


def _mla_prep_body(a_ref, gq_ref, gkv_ref, wuq_ref, wukv_ref, cos_ref, sin_ref,
                   q_ref, k_ref, v_ref, *, scale):
    a = a_ref[...]
    qn = _rms(a[:, :MLA_Q_LORA], gq_ref[...]).astype(BF16)
    q = _dot(qn, wuq_ref[...]) * scale
    kvn = _rms(a[:, MLA_Q_LORA:MLA_Q_LORA + MLA_KV_LORA], gkv_ref[...]).astype(BF16)
    kv = _dot(kvn, wukv_ref[...])

    cos_t = cos_ref[...]
    sin_t = sin_ref[...]
    lane = lax.broadcasted_iota(jnp.int32, cos_t.shape, 1)
    first_half = (lane & (MLA_ROPE // 2)) == 0

    def rope(x):
        swapped = jnp.where(first_half,
                            pltpu.roll(x, LANES - MLA_ROPE // 2, 1),
                            pltpu.roll(x, MLA_ROPE // 2, 1))
        return x * cos_t + swapped * sin_t

    k_rope = rope(a[:, MLA_Q_LORA + MLA_KV_LORA:]).astype(BF16)
    nk = MLA_HEADS * MLA_NOPE
    for h in range(MLA_HEADS):
        c0 = h * MLA_QK_PAD
        q_ref[:, c0:c0 + LANES] = q[:, c0:c0 + LANES].astype(BF16)
        q_ref[:, c0 + LANES:c0 + 2 * LANES] = rope(q[:, c0 + LANES:c0 + 2 * LANES]).astype(BF16)
        k_ref[:, c0:c0 + LANES] = kv[:, h * MLA_NOPE:(h + 1) * MLA_NOPE].astype(BF16)
        k_ref[:, c0 + LANES:c0 + 2 * LANES] = k_rope
    v_ref[...] = kv[:, nk:].astype(BF16)


def mla_prep(proj, gq, gkv, wuq, wukv, cos_t, sin_t, *, seq, tm=512):
    t = proj.shape[0]
    nseq = seq // tm
    wq = MLA_HEADS * MLA_QK_PAD
    wv = MLA_HEADS * HEAD_DIM
    scale = float((MLA_NOPE + MLA_ROPE) ** -0.5) * LOG2E
    return pl.pallas_call(
        functools.partial(_mla_prep_body, scale=scale),
        grid=(t // tm,),
        in_specs=[
            pl.BlockSpec((tm, MLA_IN_PAD), lambda i: (i, 0)),
            pl.BlockSpec((1, MLA_Q_LORA), lambda i: (0, 0)),
            pl.BlockSpec((1, MLA_KV_LORA), lambda i: (0, 0)),
            pl.BlockSpec((MLA_Q_LORA, wq), lambda i: (0, 0)),
            pl.BlockSpec((MLA_KV_LORA, 2 * wv), lambda i: (0, 0)),
            pl.BlockSpec((tm, LANES), lambda i: (i % nseq, 0)),
            pl.BlockSpec((tm, LANES), lambda i: (i % nseq, 0)),
        ],
        out_specs=[
            pl.BlockSpec((tm, wq), lambda i: (i, 0)),
            pl.BlockSpec((tm, wq), lambda i: (i, 0)),
            pl.BlockSpec((tm, wv), lambda i: (i, 0)),
        ],
        out_shape=[
            jax.ShapeDtypeStruct((t, wq), BF16),
            jax.ShapeDtypeStruct((t, wq), BF16),
            jax.ShapeDtypeStruct((t, wv), BF16),
        ],
        compiler_params=_cparams(1),
        name="mla_prep",
    )(proj, gq.reshape(1, -1), gkv.reshape(1, -1), wuq, wukv, cos_t, sin_t)


def _mla_attn_body(q_ref, k_ref, v_ref, gn_ref, o_ref, acc_scr, *, seq, nc, rc):
    tq = nc * rc
    row = lax.broadcasted_iota(jnp.int32, (rc, rc), 0)
    col = lax.broadcasted_iota(jnp.int32, (rc, rc), 1)
    causal = col <= row

    def run(q0, jobs, states):
        scores = [_dot_nt(q_ref[pl.ds(q0 + c * rc, rc), :], k_ref[pl.ds(k0, rc), :])
                  for c, k0, _ in jobs]
        states = list(states)
        for (c, k0, diag), s in zip(jobs, scores):
            if diag:
                s = jnp.where(causal, s, NEG_BIG)
            m, l = states[c]
            m_new = jnp.maximum(m, jnp.max(s, axis=1, keepdims=True))
            alpha = jnp.exp2(m - m_new)
            p = jnp.exp2(s - m_new)
            l = alpha * l + jnp.sum(p, axis=1, keepdims=True)
            acc_scr[c] = alpha * acc_scr[c] + _dot(p.astype(BF16), v_ref[pl.ds(k0, rc), :])
            states[c] = (m_new, l)
        return tuple(states)

    def q_loop(qi, _):
        q0 = pl.multiple_of(qi * tq, tq)
        for c in range(nc):
            acc_scr[c] = jnp.zeros((rc, HEAD_DIM), F32)
        init = tuple((jnp.full((rc, 1), NEG_BIG, F32), jnp.zeros((rc, 1), F32))
                     for _ in range(nc))

        def kv_loop(i, states):
            k0 = pl.multiple_of(i * tq, tq)
            return run(q0, [(c, k0 + t * rc, False) for t in range(nc) for c in range(nc)],
                       states)

        states = lax.fori_loop(0, qi, kv_loop, init)
        states = run(q0, [(c, q0 + t * rc, t == c) for t in range(nc) for c in range(t, nc)],
                     states)
        for c, (_, l) in enumerate(states):
            o_ref[pl.ds(q0 + c * rc, rc), :] = _rms(acc_scr[c] / l, gn_ref[...]).astype(BF16)
        return 0

    lax.fori_loop(0, seq // tq, q_loop, 0)


def mla_attention(q, k, v, head_gain, *, batch, seq, nc=2, rc=512):
    t = q.shape[0]
    return pl.pallas_call(
        functools.partial(_mla_attn_body, seq=seq, nc=nc, rc=rc),
        grid=(batch, MLA_HEADS),
        in_specs=[
            pl.BlockSpec((seq, MLA_QK_PAD), lambda b, h: (b, h)),
            pl.BlockSpec((seq, MLA_QK_PAD), lambda b, h: (b, h)),
            pl.BlockSpec((seq, HEAD_DIM), lambda b, h: (b, h)),
            pl.BlockSpec((1, HEAD_DIM), lambda b, h: (0, h)),
        ],
        out_specs=pl.BlockSpec((seq, HEAD_DIM), lambda b, h: (b, h)),
        out_shape=jax.ShapeDtypeStruct((t, MLA_HEADS * HEAD_DIM), BF16),
        scratch_shapes=[pltpu.VMEM((nc, rc, HEAD_DIM), F32)],
        compiler_params=_cparams(2),
        name="mla_attention",
    )(q, k, v, head_gain.reshape(1, -1))


def _sb_attn_body(q_ref, k_ref, v_ref, gn_ref, o_ref, qb_scr, kb_scr, vb_scr, acc_scr,
                  *, seq, rc, scale):
    tk = SB_KEYS
    own = rc // tk
    qb_scr[...] = (q_ref[...] * scale).astype(BF16)
    kb_scr[...] = k_ref[...].astype(BF16)
    vb_scr[...] = v_ref[...].astype(BF16)
    row = lax.broadcasted_iota(jnp.int32, (rc, tk), 0)
    col = lax.broadcasted_iota(jnp.int32, (rc, tk), 1)
    urow = lax.broadcasted_iota(jnp.int32, (tk, tk), 0)
    ucol = lax.broadcasted_iota(jnp.int32, (tk, tk), 1)
    later_keys = (urow > ucol).astype(BF16)

    def absorb(q0, ks, rest, shift):
        k0 = pl.multiple_of(ks * tk, tk)
        z = _dot_nt(qb_scr[pl.ds(q0, rc), :], kb_scr[pl.ds(k0, tk), :])
        hi = jnp.maximum(z, 0.0)
        lo = jnp.minimum(z, 0.0)
        t = jnp.log(1.0 + jnp.exp2(lo - hi)) * LOG2E
        sp = hi + t
        ls = lo - t
        strict = None if shift is None else col < row + shift
        if strict is not None:
            sp = jnp.where(strict, sp, 0.0)
        later = _dot(sp.astype(BF16), later_keys)
        a = jnp.exp2(ls - later - rest)
        if strict is not None:
            a = jnp.where(strict, a, 0.0)
        acc_scr[...] += _dot(a.astype(BF16), vb_scr[pl.ds(k0, tk), :])
        return rest + later[:, 0:1] + sp[:, 0:1]

    def chunk(ci, _):
        q0 = pl.multiple_of(ci * rc, rc)
        acc_scr[...] = jnp.zeros((rc, HEAD_DIM), F32)
        rest = jnp.zeros((rc, 1), F32)
        for j in range(own - 1, -1, -1):
            rest = absorb(q0, ci * own + j, rest, -j * tk)

        def alive(state):
            ks, rest = state
            return jnp.logical_and(ks >= 0, jnp.min(rest) < SB_DEAD_BITS)

        def walk(state):
            ks, rest = state
            for t in range(SB_SUBS):
                rest = absorb(q0, ks - t, rest, None)
            return ks - SB_SUBS, rest

        lax.while_loop(alive, walk, (ci * own - 1, rest))
        o_ref[pl.ds(q0, rc), :] = _rms(acc_scr[...], gn_ref[...]).astype(BF16)
        return 0

    lax.fori_loop(0, seq // rc, chunk, 0)


def sb_attention(proj, head_gain, *, batch, seq, rc=512):
    assert (rc // SB_KEYS) % SB_SUBS == 0
    t = proj.shape[0]
    scale = float(HEAD_DIM ** -0.5) * LOG2E
    gain_col0 = MLA_HEADS
    return pl.pallas_call(
        functools.partial(_sb_attn_body, seq=seq, rc=rc, scale=scale),
        grid=(batch, SB_HEADS),
        in_specs=[
            pl.BlockSpec((seq, HEAD_DIM), lambda b, h: (b, SB_COL0 + h)),
            pl.BlockSpec((seq, HEAD_DIM), lambda b, h: (b, SB_COL0 + SB_HEADS + h)),
            pl.BlockSpec((seq, HEAD_DIM), lambda b, h: (b, SB_COL0 + 2 * SB_HEADS + h)),
            pl.BlockSpec((1, HEAD_DIM), lambda b, h: (0, gain_col0 + h)),
        ],
        out_specs=pl.BlockSpec((seq, HEAD_DIM), lambda b, h: (b, h)),
        out_shape=jax.ShapeDtypeStruct((t, SB_W), BF16),
        scratch_shapes=[pltpu.VMEM((seq, HEAD_DIM), BF16),
                        pltpu.VMEM((seq, HEAD_DIM), BF16),
                        pltpu.VMEM((seq, HEAD_DIM), BF16),
                        pltpu.VMEM((rc, HEAD_DIM), F32)],
        compiler_params=_cparams(2),
        name="sb_attention",
    )(proj, proj, proj, head_gain.reshape(1, -1))


def _dil_attn_body(q_ref, k_ref, v_ref, slope_ref, gn_ref, o_ref, *scr, seq, scale):
    blk = DIL_BACK
    npat = len(DIL_PATTERNS)
    lse_scr, out_scr = scr[:npat], scr[npat:]
    row = lax.broadcasted_iota(jnp.int32, (blk, blk), 0)
    col = lax.broadcasted_iota(jnp.int32, (blk, blk), 1)
    dist_cur = (row - col).astype(F32)
    dist_prev = dist_cur + float(blk)
    ok_cur = row >= col
    ok_prev_band = row <= col
    slope = slope_ref[...]

    for p, (window, d) in enumerate(DIL_PATTERNS):
        assert window // d == blk
        nblk = seq // d // blk
        assert nblk & (nblk - 1) == 0
        bias_cur = -(slope * (float(d) * LOG2E)) * dist_cur
        bias_prev = -(slope * (float(d) * LOG2E)) * dist_prev

        def group(gi, _, p=p, d=d, nblk=nblk, bias_cur=bias_cur, bias_prev=bias_prev):
            def rows(idx, back):
                r = lax.shift_right_logical(idx, nblk.bit_length() - 1)
                nb = lax.bitwise_and(idx, nblk - 1)
                b = jnp.maximum(nb - back, 0)
                start = r + b * (d * blk)
                if d == 1:
                    return pl.ds(pl.multiple_of(start, blk), blk)
                return pl.ds(start, blk, stride=d)

            idxs = [gi * DIL_GROUP + g for g in range(DIL_GROUP)]
            k_own = [k_ref[rows(idx, 0), :].astype(BF16) for idx in idxs]
            k_before = [k_ref[rows(idxs[0], 1), :].astype(BF16)] + k_own[:-1]
            scores = []
            for idx, kc, kp in zip(idxs, k_own, k_before):
                q = (q_ref[rows(idx, 0), :] * scale).astype(BF16)
                scores.append((_dot_nt(q, kc), _dot_nt(q, kp)))
            probs = []
            for idx, (s_cur, s_prev) in zip(idxs, scores):
                nb = lax.bitwise_and(idx, nblk - 1)
                s_cur = jnp.where(ok_cur, s_cur + bias_cur, NEG_BIG)
                no_prev = jnp.where(nb > 0, 0.0, NEG_BIG)
                s_prev = jnp.where(ok_prev_band, s_prev + bias_prev + no_prev, NEG_BIG)
                m_blk = jnp.max(jnp.maximum(s_cur, s_prev), axis=-1, keepdims=True)
                p_cur = jnp.exp2(s_cur - m_blk)
                p_prev = jnp.exp2(s_prev - m_blk)
                l_blk = jnp.broadcast_to(jnp.sum(p_cur + p_prev, axis=-1, keepdims=True),
                                         (blk, HEAD_DIM))
                lse_scr[p][rows(idx, 0), :] = m_blk + jnp.log(l_blk) * LOG2E
                probs.append((p_cur.astype(BF16), p_prev.astype(BF16), l_blk))
            v_own = [v_ref[rows(idx, 0), :].astype(BF16) for idx in idxs]
            v_before = [v_ref[rows(idxs[0], 1), :].astype(BF16)] + v_own[:-1]
            for idx, (p_cur, p_prev, l_blk), vc, vp in zip(idxs, probs, v_own, v_before):
                out_scr[p][rows(idx, 0), :] = (_dot(p_cur, vc) + _dot(p_prev, vp)) / l_blk
            return 0

        lax.fori_loop(0, d * nblk // DIL_GROUP, group, 0)

    def finish(i, _):
        rs = pl.ds(pl.multiple_of(i * blk, blk), blk)
        lses = [lse_scr[p][rs, :] for p in range(npat)]
        top = functools.reduce(jnp.maximum, lses)
        den = jnp.zeros((blk, HEAD_DIM), F32)
        num = jnp.zeros((blk, HEAD_DIM), F32)
        for p in range(npat):
            w = jnp.exp2(lses[p] - top)
            den = den + w
            num = num + out_scr[p][rs, :] * w
        o_ref[rs, :] = _rms(num / den, gn_ref[...]).astype(BF16)
        return 0

    lax.fori_loop(0, seq // blk, finish, 0, unroll=2)


def dil_attention(proj, slopes, head_gain, *, batch, seq):
    t = proj.shape[0]
    scale = float(HEAD_DIM ** -0.5) * LOG2E
    gain_col0 = MLA_HEADS + SB_HEADS
    return pl.pallas_call(
        functools.partial(_dil_attn_body, seq=seq, scale=scale),
        grid=(batch, DIL_HEADS),
        in_specs=[
            pl.BlockSpec((seq, HEAD_DIM), lambda b, h: (b, DIL_COL0 + h)),
            pl.BlockSpec((seq, HEAD_DIM), lambda b, h: (b, DIL_COL0 + DIL_HEADS + h)),
            pl.BlockSpec((seq, HEAD_DIM), lambda b, h: (b, DIL_COL0 + 2 * DIL_HEADS + h)),
            pl.BlockSpec((None, 1, LANES), lambda b, h: (h, 0, 0)),
            pl.BlockSpec((1, HEAD_DIM), lambda b, h: (0, gain_col0 + h)),
        ],
        out_specs=pl.BlockSpec((seq, HEAD_DIM), lambda b, h: (b, h)),
        out_shape=jax.ShapeDtypeStruct((t, DIL_W), BF16),
        scratch_shapes=[pltpu.VMEM((seq, HEAD_DIM), F32)] * (2 * len(DIL_PATTERNS)),
        compiler_params=_cparams(2),
        name="dil_attention",
    )(proj, proj, proj, slopes, head_gain.reshape(1, -1))


def _outproj_body(x_ref, o1_ref, o2_ref, o3_ref, w1_ref, w2_ref, w3_ref, out_ref):
    acc = _dot(o1_ref[...], w1_ref[...])
    acc = acc + _dot(o2_ref[...], w2_ref[...])
    acc = acc + _dot(o3_ref[...], w3_ref[...])
    out_ref[...] = x_ref[...] + acc


def out_proj(x, o_mla, o_sb, o_dil, w1, w2, w3, *, tm=512):
    t, d = x.shape
    full = lambda a: pl.BlockSpec(a.shape, lambda i: (0, 0))
    rows = lambda a: pl.BlockSpec((tm, a.shape[1]), lambda i: (i, 0))
    return pl.pallas_call(
        _outproj_body,
        grid=(t // tm,),
        in_specs=[rows(x), rows(o_mla), rows(o_sb), rows(o_dil), full(w1), full(w2), full(w3)],
        out_specs=rows(x),
        out_shape=jax.ShapeDtypeStruct((t, d), F32),
        compiler_params=_cparams(1),
        name="out_proj",
    )(x, o_mla, o_sb, o_dil, w1, w2, w3)


def _rope_tables(seq):
    inv = ROPE_THETA ** (-jnp.arange(0, MLA_ROPE, 2, dtype=F32) / MLA_ROPE)
    ang = jnp.arange(seq, dtype=F32)[:, None] * inv[None, :]
    cos, sin = jnp.cos(ang), jnp.sin(ang)
    reps = LANES // MLA_ROPE
    cos_t = jnp.tile(jnp.concatenate([cos, cos], axis=1), (1, reps))
    sin_t = jnp.tile(jnp.concatenate([-sin, sin], axis=1), (1, reps))
    return cos_t, sin_t


def _prep_w_gu(w):
    d = w.shape[0]
    return w.astype(BF16).reshape(d, -1, FFN_TF).transpose(1, 0, 2)


def _prep_w_in(w_in):
    d = w_in.shape[0]
    z = lambda n: jnp.zeros((d, n), w_in.dtype)
    return jnp.concatenate(
        [w_in[:, :MLA_IN], z(MLA_IN_PAD - MLA_IN), w_in[:, MLA_IN:], z(LANES)], axis=1).astype(BF16)


def _prep_w_uq(w):
    r = w.shape[0]
    w = w.reshape(r, MLA_HEADS, MLA_NOPE + MLA_ROPE)
    w = jnp.pad(w, ((0, 0), (0, 0), (0, MLA_QK_PAD - MLA_NOPE - MLA_ROPE)))
    return w.reshape(r, MLA_HEADS * MLA_QK_PAD).astype(BF16)


def _prep_w_ukv(w):
    r = w.shape[0]
    w = w.reshape(r, MLA_HEADS, 2, HEAD_DIM).transpose(0, 2, 1, 3)
    return w.reshape(r, 2 * MLA_HEADS * HEAD_DIM).astype(BF16)


def kernel(x, ffn1_norm, ffn1_w_gu, ffn1_w_down, mix_norm, w_in, mla_q_norm, w_mla_uq, mla_kv_norm, w_mla_ukv, head_out_norm, w_out, ffn2_norm, ffn2_w_gu, ffn2_w_down, final_norm_g):
    batch, seq, d = x.shape
    depth = ffn1_norm.shape[0]
    cos_t, sin_t = _rope_tables(seq)
    slopes = 2.0 ** (-ALIBI_MAX_EXP * jnp.arange(1, DIL_HEADS + 1, dtype=F32) / DIL_HEADS)
    slopes = jnp.broadcast_to(slopes[:, None, None], (DIL_HEADS, 1, LANES))
    w_mla = MLA_HEADS * HEAD_DIM

    xt = x.reshape(batch * seq, d)
    for l in range(depth):
        xt = ffn_block(xt, ffn1_norm[l], _prep_w_gu(ffn1_w_gu[l]),
                       (0.5 * ffn1_w_down[l]).astype(BF16))

        proj = norm_proj(xt, mix_norm[l], _prep_w_in(w_in[l]))
        q, k, v = mla_prep(proj, mla_q_norm[l], mla_kv_norm[l], _prep_w_uq(w_mla_uq[l]),
                           _prep_w_ukv(w_mla_ukv[l]), cos_t, sin_t, seq=seq)
        o_mla = mla_attention(q, k, v, head_out_norm[l], batch=batch, seq=seq)
        o_sb = sb_attention(proj, head_out_norm[l], batch=batch, seq=seq)
        o_dil = dil_attention(proj, slopes, head_out_norm[l], batch=batch, seq=seq)
        wo = w_out[l].astype(BF16)
        xt = out_proj(xt, o_mla, o_sb, o_dil, wo[:w_mla], wo[w_mla:w_mla + SB_W], wo[w_mla + SB_W:])

        xt = ffn_block(xt, ffn2_norm[l], _prep_w_gu(ffn2_w_gu[l]),
                       (0.5 * ffn2_w_down[l]).astype(BF16),
                       final_norm_g if l == depth - 1 else None)
    return xt.reshape(batch, seq, d)
```

```python
import functools

import jax
import jax.numpy as jnp
from jax import lax
from jax.experimental import pallas as pl
from jax.experimental.pallas import tpu as pltpu

F32 = jnp.float32
BF16 = jnp.bfloat16

EPS = 1e-6
HEAD_DIM = 128
LANES = 128
MLA_HEADS = 6
MLA_Q_LORA = 512
MLA_KV_LORA = 256
MLA_NOPE = 128
MLA_ROPE = 64
MLA_QK_PAD = 256
ROPE_THETA = 10000.0
SB_HEADS = 4
DIL_HEADS = 6
DIL_PATTERNS = ((128, 1), (512, 4), (2048, 16))
DIL_BACK = 128
DIL_GROUP = 8
ALIBI_MAX_EXP = 8.0
NEG_BIG = -1e30
LOG2E = 1.4426950408889634
SB_KEYS = 256
FFN_TM = 1024
FFN_TF = 256
SB_SUBS = 1
SB_DEAD_BITS = 160.0

MLA_IN = MLA_Q_LORA + MLA_KV_LORA + MLA_ROPE
MLA_IN_PAD = 896
SB_W = SB_HEADS * HEAD_DIM
DIL_W = DIL_HEADS * HEAD_DIM
PROJ_HALF = MLA_IN_PAD + 3 * SB_W
assert PROJ_HALF == 3 * DIL_W + LANES
SB_COL0 = MLA_IN_PAD // LANES
DIL_COL0 = PROJ_HALF // LANES

VMEM_LIMIT = 56 * 1024 * 1024
VMEM_LIMIT_FFN = 60 * 1024 * 1024


def _cparams(grid_rank, vmem_limit=VMEM_LIMIT):
    return pltpu.CompilerParams(dimension_semantics=("arbitrary",) * grid_rank,
                                vmem_limit_bytes=vmem_limit)


def _rms(x, g):
    return x * lax.rsqrt(jnp.mean(x * x, axis=-1, keepdims=True) + EPS) * g


def _dot(a, b):
    return jnp.dot(a, b, preferred_element_type=F32)


def _dot_nt(a, b):
    return lax.dot_general(a, b, (((1,), (1,)), ((), ())), preferred_element_type=F32)


def _ffn_body(x_ref, g_ref, wg_ref, wu_ref, wd_ref, *rest, final_norm):
    fg_ref, o_ref, h_scr = rest if final_norm else (None,) + rest

    @pl.when(pl.program_id(1) == 0)
    def _():
        h_scr[...] = _rms(x_ref[...], g_ref[...]).astype(BF16)
        o_ref[...] = x_ref[...]

    h = h_scr[...]
    g = _dot(h, wg_ref[...])
    u = _dot(h, wu_ref[...])
    a = (g * (1.0 / (1.0 + jnp.exp(-g))) * u).astype(BF16)
    o_ref[...] += _dot(a, wd_ref[...])

    if final_norm:
        @pl.when(pl.program_id(1) == pl.num_programs(1) - 1)
        def _():
            o_ref[...] = _rms(o_ref[...], fg_ref[...])


def ffn_block(x, gain, w_gu, w_down_half, final_gain=None, *, tm=FFN_TM, tf=FFN_TF):
    t, d = x.shape
    f = w_down_half.shape[0]
    nf = f // tf
    final_norm = final_gain is not None
    row = pl.BlockSpec((1, d), lambda i, j: (0, 0))
    return pl.pallas_call(
        functools.partial(_ffn_body, final_norm=final_norm),
        grid=(t // tm, nf),
        in_specs=[
            pl.BlockSpec((tm, d), lambda i, j: (i, 0)),
            row,
            pl.BlockSpec((d, tf), lambda i, j: (0, j)),
            pl.BlockSpec((d, tf), lambda i, j: (0, j + nf)),
            pl.BlockSpec((tf, d), lambda i, j: (j, 0)),
        ] + ([row] if final_norm else []),
        out_specs=pl.BlockSpec((tm, d), lambda i, j: (i, 0)),
        out_shape=jax.ShapeDtypeStruct((t, d), F32),
        scratch_shapes=[pltpu.VMEM((tm, d), BF16)],
        compiler_params=_cparams(2, VMEM_LIMIT_FFN),
        name="ffn_block",
    )(x, gain.reshape(1, d), w_gu, w_gu, w_down_half,
      *([final_gain.reshape(1, d)] if final_norm else []))


def _normproj_body(x_ref, g_ref, w_ref, o_ref):
    h = _rms(x_ref[...], g_ref[...]).astype(BF16)
    o_ref[...] = _dot(h, w_ref[...])


def norm_proj(x, gain, w, *, tm=512):
    t, d = x.shape
    n = w.shape[1]
    tn = PROJ_HALF
    return pl.pallas_call(
        _normproj_body,
        grid=(n // tn, t // tm),
        in_specs=[
            pl.BlockSpec((tm, d), lambda c, i: (i, 0)),
            pl.BlockSpec((1, d), lambda c, i: (0, 0)),
            pl.BlockSpec((d, tn), lambda c, i: (0, c)),
        ],
        out_specs=pl.BlockSpec((tm, tn), lambda c, i: (i, c)),
        out_shape=jax.ShapeDtypeStruct((t, n), F32),
        compiler_params=_cparams(2),
        name="norm_proj",
    )(x, gain.reshape(1, d), w)


def _mla_prep_body(a_ref, gq_ref, gkv_ref, wuq_ref, wukv_ref, cos_ref, sin_ref,
                   q_ref, k_ref, v_ref, *, scale):
    a = a_ref[...]
    qn = _rms(a[:, :MLA_Q_LORA], gq_ref[...]).astype(BF16)
    q = _dot(qn, wuq_ref[...]) * scale
    kvn = _rms(a[:, MLA_Q_LORA:MLA_Q_LORA + MLA_KV_LORA], gkv_ref[...]).astype(BF16)
    kv = _dot(kvn, wukv_ref[...])

    cos_t = cos_ref[...]
    sin_t = sin_ref[...]
    lane = lax.broadcasted_iota(jnp.int32, cos_t.shape, 1)
    first_half = (lane & (MLA_ROPE // 2)) == 0

    def rope(x):
        swapped = jnp.where(first_half,
                            pltpu.roll(x, LANES - MLA_ROPE // 2, 1),
                            pltpu.roll(x, MLA_ROPE // 2, 1))
        return x * cos_t + swapped * sin_t

    k_rope = rope(a[:, MLA_Q_LORA + MLA_KV_LORA:]).astype(BF16)
    nk = MLA_HEADS * MLA_NOPE
    for h in range(MLA_HEADS):
        c0 = h * MLA_QK_PAD
        q_ref[:, c0:c0 + LANES] = q[:, c0:c0 + LANES].astype(BF16)
        q_ref[:, c0 + LANES:c0 + 2 * LANES] = rope(q[:, c0 + LANES:c0 + 2 * LANES]).astype(BF16)
        k_ref[:, c0:c0 + LANES] = kv[:, h * MLA_NOPE:(h + 1) * MLA_NOPE].astype(BF16)
        k_ref[:, c0 + LANES:c0 + 2 * LANES] = k_rope
    v_ref[...] = kv[:, nk:].astype(BF16)


def mla_prep(proj, gq, gkv, wuq, wukv, cos_t, sin_t, *, seq, tm=512):
    t = proj.shape[0]
    nseq = seq // tm
    wq = MLA_HEADS * MLA_QK_PAD
    wv = MLA_HEADS * HEAD_DIM
    scale = float((MLA_NOPE + MLA_ROPE) ** -0.5) * LOG2E
    return pl.pallas_call(
        functools.partial(_mla_prep_body, scale=scale),
        grid=(t // tm,),
        in_specs=[
            pl.BlockSpec((tm, MLA_IN_PAD), lambda i: (i, 0)),
            pl.BlockSpec((1, MLA_Q_LORA), lambda i: (0, 0)),
            pl.BlockSpec((1, MLA_KV_LORA), lambda i: (0, 0)),
            pl.BlockSpec((MLA_Q_LORA, wq), lambda i: (0, 0)),
            pl.BlockSpec((MLA_KV_LORA, 2 * wv), lambda i: (0, 0)),
            pl.BlockSpec((tm, LANES), lambda i: (i % nseq, 0)),
            pl.BlockSpec((tm, LANES), lambda i: (i % nseq, 0)),
        ],
        out_specs=[
            pl.BlockSpec((tm, wq), lambda i: (i, 0)),
            pl.BlockSpec((tm, wq), lambda i: (i, 0)),
            pl.BlockSpec((tm, wv), lambda i: (i, 0)),
        ],
        out_shape=[
            jax.ShapeDtypeStruct((t, wq), BF16),
            jax.ShapeDtypeStruct((t, wq), BF16),
            jax.ShapeDtypeStruct((t, wv), BF16),
        ],
        compiler_params=_cparams(1),
        name="mla_prep",
    )(proj, gq.reshape(1, -1), gkv.reshape(1, -1), wuq, wukv, cos_t, sin_t)


def _mla_attn_body(q_ref, k_ref, v_ref, gn_ref, o_ref, acc_scr, *, seq, nc, rc):
    tq = nc * rc
    row = lax.broadcasted_iota(jnp.int32, (rc, rc), 0)
    col = lax.broadcasted_iota(jnp.int32, (rc, rc), 1)
    causal = col <= row

    def run(q0, jobs, states):
        scores = [_dot_nt(q_ref[pl.ds(q0 + c * rc, rc), :], k_ref[pl.ds(k0, rc), :])
                  for c, k0, _ in jobs]
        states = list(states)
        for (c, k0, diag), s in zip(jobs, scores):
            if diag:
                s = jnp.where(causal, s, NEG_BIG)
            m, l = states[c]
            m_new = jnp.maximum(m, jnp.max(s, axis=1, keepdims=True))
            alpha = jnp.exp2(m - m_new)
            p = jnp.exp2(s - m_new)
            l = alpha * l + jnp.sum(p, axis=1, keepdims=True)
            acc_scr[c] = alpha * acc_scr[c] + _dot(p.astype(BF16), v_ref[pl.ds(k0, rc), :])
            states[c] = (m_new, l)
        return tuple(states)

    def q_loop(qi, _):
        q0 = pl.multiple_of(qi * tq, tq)
        for c in range(nc):
            acc_scr[c] = jnp.zeros((rc, HEAD_DIM), F32)
        init = tuple((jnp.full((rc, 1), NEG_BIG, F32), jnp.zeros((rc, 1), F32))
                     for _ in range(nc))

        def kv_loop(i, states):
            k0 = pl.multiple_of(i * tq, tq)
            return run(q0, [(c, k0 + t * rc, False) for t in range(nc) for c in range(nc)],
                       states)

        states = lax.fori_loop(0, qi, kv_loop, init)
        states = run(q0, [(c, q0 + t * rc, t == c) for t in range(nc) for c in range(t, nc)],
                     states)
        for c, (_, l) in enumerate(states):
            o_ref[pl.ds(q0 + c * rc, rc), :] = _rms(acc_scr[c] / l, gn_ref[...]).astype(BF16)
        return 0

    lax.fori_loop(0, seq // tq, q_loop, 0)


def mla_attention(q, k, v, head_gain, *, batch, seq, nc=2, rc=512):
    t = q.shape[0]
    return pl.pallas_call(
        functools.partial(_mla_attn_body, seq=seq, nc=nc, rc=rc),
        grid=(batch, MLA_HEADS),
        in_specs=[
            pl.BlockSpec((seq, MLA_QK_PAD), lambda b, h: (b, h)),
            pl.BlockSpec((seq, MLA_QK_PAD), lambda b, h: (b, h)),
            pl.BlockSpec((seq, HEAD_DIM), lambda b, h: (b, h)),
            pl.BlockSpec((1, HEAD_DIM), lambda b, h: (0, h)),
        ],
        out_specs=pl.BlockSpec((seq, HEAD_DIM), lambda b, h: (b, h)),
        out_shape=jax.ShapeDtypeStruct((t, MLA_HEADS * HEAD_DIM), BF16),
        scratch_shapes=[pltpu.VMEM((nc, rc, HEAD_DIM), F32)],
        compiler_params=_cparams(2),
        name="mla_attention",
    )(q, k, v, head_gain.reshape(1, -1))


def _sb_attn_body(q_ref, k_ref, v_ref, gn_ref, o_ref, qb_scr, kb_scr, vb_scr, acc_scr,
                  *, seq, rc, scale):
    tk = SB_KEYS
    own = rc // tk
    qb_scr[...] = (q_ref[...] * scale).astype(BF16)
    kb_scr[...] = k_ref[...].astype(BF16)
    vb_scr[...] = v_ref[...].astype(BF16)
    row = lax.broadcasted_iota(jnp.int32, (rc, tk), 0)
    col = lax.broadcasted_iota(jnp.int32, (rc, tk), 1)
    urow = lax.broadcasted_iota(jnp.int32, (tk, tk), 0)
    ucol = lax.broadcasted_iota(jnp.int32, (tk, tk), 1)
    later_keys = (urow > ucol).astype(BF16)

    def absorb(q0, ks, rest, shift):
        k0 = pl.multiple_of(ks * tk, tk)
        z = _dot_nt(qb_scr[pl.ds(q0, rc), :], kb_scr[pl.ds(k0, tk), :])
        hi = jnp.maximum(z, 0.0)
        lo = jnp.minimum(z, 0.0)
        t = jnp.log(1.0 + jnp.exp2(lo - hi)) * LOG2E
        sp = hi + t
        ls = lo - t
        strict = None if shift is None else col < row + shift
        if strict is not None:
            sp = jnp.where(strict, sp, 0.0)
        later = _dot(sp.astype(BF16), later_keys)
        a = jnp.exp2(ls - later - rest)
        if strict is not None:
            a = jnp.where(strict, a, 0.0)
        acc_scr[...] += _dot(a.astype(BF16), vb_scr[pl.ds(k0, tk), :])
        return rest + later[:, 0:1] + sp[:, 0:1]

    def chunk(ci, _):
        q0 = pl.multiple_of(ci * rc, rc)
        acc_scr[...] = jnp.zeros((rc, HEAD_DIM), F32)
        rest = jnp.zeros((rc, 1), F32)
        for j in range(own - 1, -1, -1):
            rest = absorb(q0, ci * own + j, rest, -j * tk)

        def alive(state):
            ks, rest = state
            return jnp.logical_and(ks >= 0, jnp.min(rest) < SB_DEAD_BITS)

        def walk(state):
            ks, rest = state
            for t in range(SB_SUBS):
                rest = absorb(q0, ks - t, rest, None)
            return ks - SB_SUBS, rest

        lax.while_loop(alive, walk, (ci * own - 1, rest))
        o_ref[pl.ds(q0, rc), :] = _rms(acc_scr[...], gn_ref[...]).astype(BF16)
        return 0

    lax.fori_loop(0, seq // rc, chunk, 0)


def sb_attention(proj, head_gain, *, batch, seq, rc=512):
    assert (rc // SB_KEYS) % SB_SUBS == 0
    t = proj.shape[0]
    scale = float(HEAD_DIM ** -0.5) * LOG2E
    gain_col0 = MLA_HEADS
    return pl.pallas_call(
        functools.partial(_sb_attn_body, seq=seq, rc=rc, scale=scale),
        grid=(batch, SB_HEADS),
        in_specs=[
            pl.BlockSpec((seq, HEAD_DIM), lambda b, h: (b, SB_COL0 + h)),
            pl.BlockSpec((seq, HEAD_DIM), lambda b, h: (b, SB_COL0 + SB_HEADS + h)),
            pl.BlockSpec((seq, HEAD_DIM), lambda b, h: (b, SB_COL0 + 2 * SB_HEADS + h)),
            pl.BlockSpec((1, HEAD_DIM), lambda b, h: (0, gain_col0 + h)),
        ],
        out_specs=pl.BlockSpec((seq, HEAD_DIM), lambda b, h: (b, h)),
        out_shape=jax.ShapeDtypeStruct((t, SB_W), BF16),
        scratch_shapes=[pltpu.VMEM((seq, HEAD_DIM), BF16),
                        pltpu.VMEM((seq, HEAD_DIM), BF16),
                        pltpu.VMEM((seq, HEAD_DIM), BF16),
                        pltpu.VMEM((rc, HEAD_DIM), F32)],
        compiler_params=_cparams(2),
        name="sb_attention",
    )(proj, proj, proj, head_gain.reshape(1, -1))


def _dil_attn_body(q_ref, k_ref, v_ref, slope_ref, gn_ref, o_ref, *scr, seq, scale):
    blk = DIL_BACK
    npat = len(DIL_PATTERNS)
    lse_scr, out_scr = scr[:npat], scr[npat:]
    row = lax.broadcasted_iota(jnp.int32, (blk, blk), 0)
    col = lax.broadcasted_iota(jnp.int32, (blk, blk), 1)
    dist_cur = (row - col).astype(F32)
    dist_prev = dist_cur + float(blk)
    ok_cur = row >= col
    ok_prev_band = row <= col
    slope = slope_ref[...]

    for p, (window, d) in enumerate(DIL_PATTERNS):
        assert window // d == blk
        nblk = seq // d // blk
        assert nblk & (nblk - 1) == 0
        bias_cur = -(slope * (float(d) * LOG2E)) * dist_cur
        bias_prev = -(slope * (float(d) * LOG2E)) * dist_prev

        def group(gi, _, p=p, d=d, nblk=nblk, bias_cur=bias_cur, bias_prev=bias_prev):
            def rows(idx, back):
                r = lax.shift_right_logical(idx, nblk.bit_length() - 1)
                nb = lax.bitwise_and(idx, nblk - 1)
                b = jnp.maximum(nb - back, 0)
                start = r + b * (d * blk)
                if d == 1:
                    return pl.ds(pl.multiple_of(start, blk), blk)
                return pl.ds(start, blk, stride=d)

            idxs = [gi * DIL_GROUP + g for g in range(DIL_GROUP)]
            k_own = [k_ref[rows(idx, 0), :].astype(BF16) for idx in idxs]
            k_before = [k_ref[rows(idxs[0], 1), :].astype(BF16)] + k_own[:-1]
            scores = []
            for idx, kc, kp in zip(idxs, k_own, k_before):
                q = (q_ref[rows(idx, 0), :] * scale).astype(BF16)
                scores.append((_dot_nt(q, kc), _dot_nt(q, kp)))
            probs = []
            for idx, (s_cur, s_prev) in zip(idxs, scores):
                nb = lax.bitwise_and(idx, nblk - 1)
                s_cur = jnp.where(ok_cur, s_cur + bias_cur, NEG_BIG)
                no_prev = jnp.where(nb > 0, 0.0, NEG_BIG)
                s_prev = jnp.where(ok_prev_band, s_prev + bias_prev + no_prev, NEG_BIG)
                m_blk = jnp.max(jnp.maximum(s_cur, s_prev), axis=-1, keepdims=True)
                p_cur = jnp.exp2(s_cur - m_blk)
                p_prev = jnp.exp2(s_prev - m_blk)
                l_blk = jnp.broadcast_to(jnp.sum(p_cur + p_prev, axis=-1, keepdims=True),
                                         (blk, HEAD_DIM))
                lse_scr[p][rows(idx, 0), :] = m_blk + jnp.log(l_blk) * LOG2E
                probs.append((p_cur.astype(BF16), p_prev.astype(BF16), l_blk))
            v_own = [v_ref[rows(idx, 0), :].astype(BF16) for idx in idxs]
            v_before = [v_ref[rows(idxs[0], 1), :].astype(BF16)] + v_own[:-1]
            for idx, (p_cur, p_prev, l_blk), vc, vp in zip(idxs, probs, v_own, v_before):
                out_scr[p][rows(idx, 0), :] = (_dot(p_cur, vc) + _dot(p_prev, vp)) / l_blk
            return 0

        lax.fori_loop(0, d * nblk // DIL_GROUP, group, 0)

    def finish(i, _):
        rs = pl.ds(pl.multiple_of(i * blk, blk), blk)
        lses = [lse_scr[p][rs, :] for p in range(npat)]
        top = functools.reduce(jnp.maximum, lses)
        den = jnp.zeros((blk, HEAD_DIM), F32)
        num = jnp.zeros((blk, HEAD_DIM), F32)
        for p in range(npat):
            w = jnp.exp2(lses[p] - top)
            den = den + w
            num = num + out_scr[p][rs, :] * w
        o_ref[rs, :] = _rms(num / den, gn_ref[...]).astype(BF16)
        return 0

    lax.fori_loop(0, seq // blk, finish, 0, unroll=2)


def dil_attention(proj, slopes, head_gain, *, batch, seq):
    t = proj.shape[0]
    scale = float(HEAD_DIM ** -0.5) * LOG2E
    gain_col0 = MLA_HEADS + SB_HEADS
    return pl.pallas_call(
        functools.partial(_dil_attn_body, seq=seq, scale=scale),
        grid=(batch, DIL_HEADS),
        in_specs=[
            pl.BlockSpec((seq, HEAD_DIM), lambda b, h: (b, DIL_COL0 + h)),
            pl.BlockSpec((seq, HEAD_DIM), lambda b, h: (b, DIL_COL0 + DIL_HEADS + h)),
            pl.BlockSpec((seq, HEAD_DIM), lambda b, h: (b, DIL_COL0 + 2 * DIL_HEADS + h)),
            pl.BlockSpec((None, 1, LANES), lambda b, h: (h, 0, 0)),
            pl.BlockSpec((1, HEAD_DIM), lambda b, h: (0, gain_col0 + h)),
        ],
        out_specs=pl.BlockSpec((seq, HEAD_DIM), lambda b, h: (b, h)),
        out_shape=jax.ShapeDtypeStruct((t, DIL_W), BF16),
        scratch_shapes=[pltpu.VMEM((seq, HEAD_DIM), F32)] * (2 * len(DIL_PATTERNS)),
        compiler_params=_cparams(2),
        name="dil_attention",
    )(proj, proj, proj, slopes, head_gain.reshape(1, -1))


def _outproj_body(x_ref, o1_ref, o2_ref, o3_ref, w1_ref, w2_ref, w3_ref, out_ref):
    acc = _dot(o1_ref[...], w1_ref[...])
    acc = acc + _dot(o2_ref[...], w2_ref[...])
    acc = acc + _dot(o3_ref[...], w3_ref[...])
    out_ref[...] = x_ref[...] + acc


def out_proj(x, o_mla, o_sb, o_dil, w1, w2, w3, *, tm=512):
    t, d = x.shape
    full = lambda a: pl.BlockSpec(a.shape, lambda i: (0, 0))
    rows = lambda a: pl.BlockSpec((tm, a.shape[1]), lambda i: (i, 0))
    return pl.pallas_call(
        _outproj_body,
        grid=(t // tm,),
        in_specs=[rows(x), rows(o_mla), rows(o_sb), rows(o_dil), full(w1), full(w2), full(w3)],
        out_specs=rows(x),
        out_shape=jax.ShapeDtypeStruct((t, d), F32),
        compiler_params=_cparams(1),
        name="out_proj",
    )(x, o_mla, o_sb, o_dil, w1, w2, w3)


def _rope_tables(seq):
    inv = ROPE_THETA ** (-jnp.arange(0, MLA_ROPE, 2, dtype=F32) / MLA_ROPE)
    ang = jnp.arange(seq, dtype=F32)[:, None] * inv[None, :]
    cos, sin = jnp.cos(ang), jnp.sin(ang)
    reps = LANES // MLA_ROPE
    cos_t = jnp.tile(jnp.concatenate([cos, cos], axis=1), (1, reps))
    sin_t = jnp.tile(jnp.concatenate([-sin, sin], axis=1), (1, reps))
    return cos_t, sin_t


def _prep_w_in(w_in):
    d = w_in.shape[0]
    z = lambda n: jnp.zeros((d, n), w_in.dtype)
    return jnp.concatenate(
        [w_in[:, :MLA_IN], z(MLA_IN_PAD - MLA_IN), w_in[:, MLA_IN:], z(LANES)], axis=1).astype(BF16)


def _prep_w_uq(w):
    r = w.shape[0]
    w = w.reshape(r, MLA_HEADS, MLA_NOPE + MLA_ROPE)
    w = jnp.pad(w, ((0, 0), (0, 0), (0, MLA_QK_PAD - MLA_NOPE - MLA_ROPE)))
    return w.reshape(r, MLA_HEADS * MLA_QK_PAD).astype(BF16)


def _prep_w_ukv(w):
    r = w.shape[0]
    w = w.reshape(r, MLA_HEADS, 2, HEAD_DIM).transpose(0, 2, 1, 3)
    return w.reshape(r, 2 * MLA_HEADS * HEAD_DIM).astype(BF16)


def kernel(x, ffn1_norm, ffn1_w_gu, ffn1_w_down, mix_norm, w_in, mla_q_norm, w_mla_uq, mla_kv_norm, w_mla_ukv, head_out_norm, w_out, ffn2_norm, ffn2_w_gu, ffn2_w_down, final_norm_g):
    batch, seq, d = x.shape
    depth = ffn1_norm.shape[0]
    cos_t, sin_t = _rope_tables(seq)
    slopes = 2.0 ** (-ALIBI_MAX_EXP * jnp.arange(1, DIL_HEADS + 1, dtype=F32) / DIL_HEADS)
    slopes = jnp.broadcast_to(slopes[:, None, None], (DIL_HEADS, 1, LANES))
    w_mla = MLA_HEADS * HEAD_DIM

    xt = x.reshape(batch * seq, d)
    for l in range(depth):
        xt = ffn_block(xt, ffn1_norm[l], ffn1_w_gu[l].astype(BF16),
                       (0.5 * ffn1_w_down[l]).astype(BF16))

        proj = norm_proj(xt, mix_norm[l], _prep_w_in(w_in[l]))
        q, k, v = mla_prep(proj, mla_q_norm[l], mla_kv_norm[l], _prep_w_uq(w_mla_uq[l]),
                           _prep_w_ukv(w_mla_ukv[l]), cos_t, sin_t, seq=seq)
        o_mla = mla_attention(q, k, v, head_out_norm[l], batch=batch, seq=seq)
        o_sb = sb_attention(proj, head_out_norm[l], batch=batch, seq=seq)
        o_dil = dil_attention(proj, slopes, head_out_norm[l], batch=batch, seq=seq)
        wo = w_out[l].astype(BF16)
        xt = out_proj(xt, o_mla, o_sb, o_dil, wo[:w_mla], wo[w_mla:w_mla + SB_W], wo[w_mla + SB_W:])

        xt = ffn_block(xt, ffn2_norm[l], ffn2_w_gu[l].astype(BF16),
                       (0.5 * ffn2_w_down[l]).astype(BF16),
                       final_norm_g if l == depth - 1 else None)
    return xt.reshape(batch, seq, d)
```

```python
import functools

import jax
import jax.numpy as jnp
from jax import lax
from jax.experimental import pallas as pl
from jax.experimental.pallas import tpu as pltpu

F32 = jnp.float32
BF16 = jnp.bfloat16

EPS = 1e-6
HEAD_DIM = 128
LANES = 128
MLA_HEADS = 6
MLA_Q_LORA = 512
MLA_KV_LORA = 256
MLA_NOPE = 128
MLA_ROPE = 64
MLA_QK_PAD = 256
ROPE_THETA = 10000.0
SB_HEADS = 4
DIL_HEADS = 6
DIL_PATTERNS = ((128, 1), (512, 4), (2048, 16))
DIL_BACK = 128
DIL_GROUP = 8
ALIBI_MAX_EXP = 8.0
NEG_BIG = -1e30
LOG2E = 1.4426950408889634
SB_KEYS = 256
FFN_TM = 1024
FFN_ROWS = 512
FFN_TF = 512
SB_SUBS = 1
SB_DEAD_BITS = 160.0

MLA_IN = MLA_Q_LORA + MLA_KV_LORA + MLA_ROPE
MLA_IN_PAD = 896
SB_W = SB_HEADS * HEAD_DIM
DIL_W = DIL_HEADS * HEAD_DIM
PROJ_HALF = MLA_IN_PAD + 3 * SB_W
assert PROJ_HALF == 3 * DIL_W + LANES
SB_COL0 = MLA_IN_PAD // LANES
DIL_COL0 = PROJ_HALF // LANES

VMEM_LIMIT = 56 * 1024 * 1024
VMEM_LIMIT_FFN = 62 * 1024 * 1024


def _cparams(grid_rank, vmem_limit=VMEM_LIMIT):
    return pltpu.CompilerParams(dimension_semantics=("arbitrary",) * grid_rank,
                                vmem_limit_bytes=vmem_limit)


def _rms(x, g):
    return x * lax.rsqrt(jnp.mean(x * x, axis=-1, keepdims=True) + EPS) * g


def _dot(a, b):
    return jnp.dot(a, b, preferred_element_type=F32)


def _dot_nt(a, b):
    return lax.dot_general(a, b, (((1,), (1,)), ((), ())), preferred_element_type=F32)


def _ffn_body(x_ref, g_ref, wg_ref, wu_ref, wd_ref, *rest, final_norm):
    fg_ref, o_ref, h_scr = rest if final_norm else (None,) + rest

    @pl.when(pl.program_id(1) == 0)
    def _():
        h_scr[...] = _rms(x_ref[...], g_ref[...]).astype(BF16)
        o_ref[...] = x_ref[...]

    for r0 in range(0, h_scr.shape[0], FFN_ROWS):
        rows = slice(r0, r0 + FFN_ROWS)
        h = h_scr[rows, :]
        g = _dot(h, wg_ref[...])
        u = _dot(h, wu_ref[...])
        a = (g * (1.0 / (1.0 + jnp.exp(-g))) * u).astype(BF16)
        o_ref[rows, :] += _dot(a, wd_ref[...])

    if final_norm:
        @pl.when(pl.program_id(1) == pl.num_programs(1) - 1)
        def _():
            o_ref[...] = _rms(o_ref[...], fg_ref[...])


def ffn_block(x, gain, w_gu, w_down_half, final_gain=None, *, tm=FFN_TM, tf=FFN_TF):
    t, d = x.shape
    f = w_down_half.shape[0]
    nf = f // tf
    final_norm = final_gain is not None
    row = pl.BlockSpec((1, d), lambda i, j: (0, 0))
    return pl.pallas_call(
        functools.partial(_ffn_body, final_norm=final_norm),
        grid=(t // tm, nf),
        in_specs=[
            pl.BlockSpec((tm, d), lambda i, j: (i, 0)),
            row,
            pl.BlockSpec((d, tf), lambda i, j: (0, j)),
            pl.BlockSpec((d, tf), lambda i, j: (0, j + nf)),
            pl.BlockSpec((tf, d), lambda i, j: (j, 0)),
        ] + ([row] if final_norm else []),
        out_specs=pl.BlockSpec((tm, d), lambda i, j: (i, 0)),
        out_shape=jax.ShapeDtypeStruct((t, d), F32),
        scratch_shapes=[pltpu.VMEM((tm, d), BF16)],
        compiler_params=_cparams(2, VMEM_LIMIT_FFN),
        name="ffn_block",
    )(x, gain.reshape(1, d), w_gu, w_gu, w_down_half,
      *([final_gain.reshape(1, d)] if final_norm else []))


def _normproj_body(x_ref, g_ref, w_ref, o_ref):
    h = _rms(x_ref[...], g_ref[...]).astype(BF16)
    o_ref[...] = _dot(h, w_ref[...])


def norm_proj(x, gain, w, *, tm=512):
    t, d = x.shape
    n = w.shape[1]
    tn = PROJ_HALF
    return pl.pallas_call(
        _normproj_body,
        grid=(n // tn, t // tm),
        in_specs=[
            pl.BlockSpec((tm, d), lambda c, i: (i, 0)),
            pl.BlockSpec((1, d), lambda c, i: (0, 0)),
            pl.BlockSpec((d, tn), lambda c, i: (0, c)),
        ],
        out_specs=pl.BlockSpec((tm, tn), lambda c, i: (i, c)),
        out_shape=jax.ShapeDtypeStruct((t, n), F32),
        compiler_params=_cparams(2),
        name="norm_proj",
    )(x, gain.reshape(1, d), w)


def _mla_prep_body(a_ref, gq_ref, gkv_ref, wuq_ref, wukv_ref, cos_ref, sin_ref,
                   q_ref, k_ref, v_ref, *, scale):
    a = a_ref[...]
    qn = _rms(a[:, :MLA_Q_LORA], gq_ref[...]).astype(BF16)
    q = _dot(qn, wuq_ref[...]) * scale
    kvn = _rms(a[:, MLA_Q_LORA:MLA_Q_LORA + MLA_KV_LORA], gkv_ref[...]).astype(BF16)
    kv = _dot(kvn, wukv_ref[...])

    cos_t = cos_ref[...]
    sin_t = sin_ref[...]
    lane = lax.broadcasted_iota(jnp.int32, cos_t.shape, 1)
    first_half = (lane & (MLA_ROPE // 2)) == 0

    def rope(x):
        swapped = jnp.where(first_half,
                            pltpu.roll(x, LANES - MLA_ROPE // 2, 1),
                            pltpu.roll(x, MLA_ROPE // 2, 1))
        return x * cos_t + swapped * sin_t

    k_rope = rope(a[:, MLA_Q_LORA + MLA_KV_LORA:]).astype(BF16)
    nk = MLA_HEADS * MLA_NOPE
    for h in range(MLA_HEADS):
        c0 = h * MLA_QK_PAD
        q_ref[:, c0:c0 + LANES] = q[:, c0:c0 + LANES].astype(BF16)
        q_ref[:, c0 + LANES:c0 + 2 * LANES] = rope(q[:, c0 + LANES:c0 + 2 * LANES]).astype(BF16)
        k_ref[:, c0:c0 + LANES] = kv[:, h * MLA_NOPE:(h + 1) * MLA_NOPE].astype(BF16)
        k_ref[:, c0 + LANES:c0 + 2 * LANES] = k_rope
    v_ref[...] = kv[:, nk:].astype(BF16)


def mla_prep(proj, gq, gkv, wuq, wukv, cos_t, sin_t, *, seq, tm=512):
    t = proj.shape[0]
    nseq = seq // tm
    wq = MLA_HEADS * MLA_QK_PAD
    wv = MLA_HEADS * HEAD_DIM
    scale = float((MLA_NOPE + MLA_ROPE) ** -0.5) * LOG2E
    return pl.pallas_call(
        functools.partial(_mla_prep_body, scale=scale),
        grid=(t // tm,),
        in_specs=[
            pl.BlockSpec((tm, MLA_IN_PAD), lambda i: (i, 0)),
            pl.BlockSpec((1, MLA_Q_LORA), lambda i: (0, 0)),
            pl.BlockSpec((1, MLA_KV_LORA), lambda i: (0, 0)),
            pl.BlockSpec((MLA_Q_LORA, wq), lambda i: (0, 0)),
            pl.BlockSpec((MLA_KV_LORA, 2 * wv), lambda i: (0, 0)),
            pl.BlockSpec((tm, LANES), lambda i: (i % nseq, 0)),
            pl.BlockSpec((tm, LANES), lambda i: (i % nseq, 0)),
        ],
        out_specs=[
            pl.BlockSpec((tm, wq), lambda i: (i, 0)),
            pl.BlockSpec((tm, wq), lambda i: (i, 0)),
            pl.BlockSpec((tm, wv), lambda i: (i, 0)),
        ],
        out_shape=[
            jax.ShapeDtypeStruct((t, wq), BF16),
            jax.ShapeDtypeStruct((t, wq), BF16),
            jax.ShapeDtypeStruct((t, wv), BF16),
        ],
        compiler_params=_cparams(1),
        name="mla_prep",
    )(proj, gq.reshape(1, -1), gkv.reshape(1, -1), wuq, wukv, cos_t, sin_t)


def _mla_attn_body(q_ref, k_ref, v_ref, gn_ref, o_ref, acc_scr, *, seq, nc, rc):
    tq = nc * rc
    row = lax.broadcasted_iota(jnp.int32, (rc, rc), 0)
    col = lax.broadcasted_iota(jnp.int32, (rc, rc), 1)
    causal = col <= row

    def run(q0, jobs, states):
        scores = [_dot_nt(q_ref[pl.ds(q0 + c * rc, rc), :], k_ref[pl.ds(k0, rc), :])
                  for c, k0, _ in jobs]
        states = list(states)
        for (c, k0, diag), s in zip(jobs, scores):
            if diag:
                s = jnp.where(causal, s, NEG_BIG)
            m, l = states[c]
            m_new = jnp.maximum(m, jnp.max(s, axis=1, keepdims=True))
            alpha = jnp.exp2(m - m_new)
            p = jnp.exp2(s - m_new)
            l = alpha * l + jnp.sum(p, axis=1, keepdims=True)
            acc_scr[c] = alpha * acc_scr[c] + _dot(p.astype(BF16), v_ref[pl.ds(k0, rc), :])
            states[c] = (m_new, l)
        return tuple(states)

    def q_loop(qi, _):
        q0 = pl.multiple_of(qi * tq, tq)
        for c in range(nc):
            acc_scr[c] = jnp.zeros((rc, HEAD_DIM), F32)
        init = tuple((jnp.full((rc, 1), NEG_BIG, F32), jnp.zeros((rc, 1), F32))
                     for _ in range(nc))

        def kv_loop(i, states):
            k0 = pl.multiple_of(i * tq, tq)
            return run(q0, [(c, k0 + t * rc, False) for t in range(nc) for c in range(nc)],
                       states)

        states = lax.fori_loop(0, qi, kv_loop, init)
        states = run(q0, [(c, q0 + t * rc, t == c) for t in range(nc) for c in range(t, nc)],
                     states)
        for c, (_, l) in enumerate(states):
            o_ref[pl.ds(q0 + c * rc, rc), :] = _rms(acc_scr[c] / l, gn_ref[...]).astype(BF16)
        return 0

    lax.fori_loop(0, seq // tq, q_loop, 0)


def mla_attention(q, k, v, head_gain, *, batch, seq, nc=2, rc=512):
    t = q.shape[0]
    return pl.pallas_call(
        functools.partial(_mla_attn_body, seq=seq, nc=nc, rc=rc),
        grid=(batch, MLA_HEADS),
        in_specs=[
            pl.BlockSpec((seq, MLA_QK_PAD), lambda b, h: (b, h)),
            pl.BlockSpec((seq, MLA_QK_PAD), lambda b, h: (b, h)),
            pl.BlockSpec((seq, HEAD_DIM), lambda b, h: (b, h)),
            pl.BlockSpec((1, HEAD_DIM), lambda b, h: (0, h)),
        ],
        out_specs=pl.BlockSpec((seq, HEAD_DIM), lambda b, h: (b, h)),
        out_shape=jax.ShapeDtypeStruct((t, MLA_HEADS * HEAD_DIM), BF16),
        scratch_shapes=[pltpu.VMEM((nc, rc, HEAD_DIM), F32)],
        compiler_params=_cparams(2),
        name="mla_attention",
    )(q, k, v, head_gain.reshape(1, -1))


def _sb_attn_body(q_ref, k_ref, v_ref, gn_ref, o_ref, qb_scr, kb_scr, vb_scr, acc_scr,
                  *, seq, rc, scale):
    tk = SB_KEYS
    own = rc // tk
    qb_scr[...] = (q_ref[...] * scale).astype(BF16)
    kb_scr[...] = k_ref[...].astype(BF16)
    vb_scr[...] = v_ref[...].astype(BF16)
    row = lax.broadcasted_iota(jnp.int32, (rc, tk), 0)
    col = lax.broadcasted_iota(jnp.int32, (rc, tk), 1)
    urow = lax.broadcasted_iota(jnp.int32, (tk, tk), 0)
    ucol = lax.broadcasted_iota(jnp.int32, (tk, tk), 1)
    later_keys = (urow > ucol).astype(BF16)

    def absorb(q0, ks, rest, shift):
        k0 = pl.multiple_of(ks * tk, tk)
        z = _dot_nt(qb_scr[pl.ds(q0, rc), :], kb_scr[pl.ds(k0, tk), :])
        hi = jnp.maximum(z, 0.0)
        lo = jnp.minimum(z, 0.0)
        t = jnp.log(1.0 + jnp.exp2(lo - hi)) * LOG2E
        sp = hi + t
        ls = lo - t
        strict = None if shift is None else col < row + shift
        if strict is not None:
            sp = jnp.where(strict, sp, 0.0)
        later = _dot(sp.astype(BF16), later_keys)
        a = jnp.exp2(ls - later - rest)
        if strict is not None:
            a = jnp.where(strict, a, 0.0)
        acc_scr[...] += _dot(a.astype(BF16), vb_scr[pl.ds(k0, tk), :])
        return rest + later[:, 0:1] + sp[:, 0:1]

    def chunk(ci, _):
        q0 = pl.multiple_of(ci * rc, rc)
        acc_scr[...] = jnp.zeros((rc, HEAD_DIM), F32)
        rest = jnp.zeros((rc, 1), F32)
        for j in range(own - 1, -1, -1):
            rest = absorb(q0, ci * own + j, rest, -j * tk)

        def alive(state):
            ks, rest = state
            return jnp.logical_and(ks >= 0, jnp.min(rest) < SB_DEAD_BITS)

        def walk(state):
            ks, rest = state
            for t in range(SB_SUBS):
                rest = absorb(q0, ks - t, rest, None)
            return ks - SB_SUBS, rest

        lax.while_loop(alive, walk, (ci * own - 1, rest))
        o_ref[pl.ds(q0, rc), :] = _rms(acc_scr[...], gn_ref[...]).astype(BF16)
        return 0

    lax.fori_loop(0, seq // rc, chunk, 0)


def sb_attention(proj, head_gain, *, batch, seq, rc=512):
    assert (rc // SB_KEYS) % SB_SUBS == 0
    t = proj.shape[0]
    scale = float(HEAD_DIM ** -0.5) * LOG2E
    gain_col0 = MLA_HEADS
    return pl.pallas_call(
        functools.partial(_sb_attn_body, seq=seq, rc=rc, scale=scale),
        grid=(batch, SB_HEADS),
        in_specs=[
            pl.BlockSpec((seq, HEAD_DIM), lambda b, h: (b, SB_COL0 + h)),
            pl.BlockSpec((seq, HEAD_DIM), lambda b, h: (b, SB_COL0 + SB_HEADS + h)),
            pl.BlockSpec((seq, HEAD_DIM), lambda b, h: (b, SB_COL0 + 2 * SB_HEADS + h)),
            pl.BlockSpec((1, HEAD_DIM), lambda b, h: (0, gain_col0 + h)),
        ],
        out_specs=pl.BlockSpec((seq, HEAD_DIM), lambda b, h: (b, h)),
        out_shape=jax.ShapeDtypeStruct((t, SB_W), BF16),
        scratch_shapes=[pltpu.VMEM((seq, HEAD_DIM), BF16),
                        pltpu.VMEM((seq, HEAD_DIM), BF16),
                        pltpu.VMEM((seq, HEAD_DIM), BF16),
                        pltpu.VMEM((rc, HEAD_DIM), F32)],
        compiler_params=_cparams(2),
        name="sb_attention",
    )(proj, proj, proj, head_gain.reshape(1, -1))


def _dil_attn_body(q_ref, k_ref, v_ref, slope_ref, gn_ref, o_ref, *scr, seq, scale):
    blk = DIL_BACK
    npat = len(DIL_PATTERNS)
    lse_scr, out_scr = scr[:npat], scr[npat:]
    row = lax.broadcasted_iota(jnp.int32, (blk, blk), 0)
    col = lax.broadcasted_iota(jnp.int32, (blk, blk), 1)
    dist_cur = (row - col).astype(F32)
    dist_prev = dist_cur + float(blk)
    ok_cur = row >= col
    ok_prev_band = row <= col
    slope = slope_ref[...]

    for p, (window, d) in enumerate(DIL_PATTERNS):
        assert window // d == blk
        nblk = seq // d // blk
        assert nblk & (nblk - 1) == 0
        bias_cur = -(slope * (float(d) * LOG2E)) * dist_cur
        bias_prev = -(slope * (float(d) * LOG2E)) * dist_prev

        def group(gi, _, p=p, d=d, nblk=nblk, bias_cur=bias_cur, bias_prev=bias_prev):
            def rows(idx, back):
                r = lax.shift_right_logical(idx, nblk.bit_length() - 1)
                nb = lax.bitwise_and(idx, nblk - 1)
                b = jnp.maximum(nb - back, 0)
                start = r + b * (d * blk)
                if d == 1:
                    return pl.ds(pl.multiple_of(start, blk), blk)
                return pl.ds(start, blk, stride=d)

            idxs = [gi * DIL_GROUP + g for g in range(DIL_GROUP)]
            k_own = [k_ref[rows(idx, 0), :].astype(BF16) for idx in idxs]
            k_before = [k_ref[rows(idxs[0], 1), :].astype(BF16)] + k_own[:-1]
            scores = []
            for idx, kc, kp in zip(idxs, k_own, k_before):
                q = (q_ref[rows(idx, 0), :] * scale).astype(BF16)
                scores.append((_dot_nt(q, kc), _dot_nt(q, kp)))
            probs = []
            for idx, (s_cur, s_prev) in zip(idxs, scores):
                nb = lax.bitwise_and(idx, nblk - 1)
                s_cur = jnp.where(ok_cur, s_cur + bias_cur, NEG_BIG)
                no_prev = jnp.where(nb > 0, 0.0, NEG_BIG)
                s_prev = jnp.where(ok_prev_band, s_prev + bias_prev + no_prev, NEG_BIG)
                m_blk = jnp.max(jnp.maximum(s_cur, s_prev), axis=-1, keepdims=True)
                p_cur = jnp.exp2(s_cur - m_blk)
                p_prev = jnp.exp2(s_prev - m_blk)
                l_blk = jnp.broadcast_to(jnp.sum(p_cur + p_prev, axis=-1, keepdims=True),
                                         (blk, HEAD_DIM))
                lse_scr[p][rows(idx, 0), :] = m_blk + jnp.log(l_blk) * LOG2E
                probs.append((p_cur.astype(BF16), p_prev.astype(BF16), l_blk))
            v_own = [v_ref[rows(idx, 0), :].astype(BF16) for idx in idxs]
            v_before = [v_ref[rows(idxs[0], 1), :].astype(BF16)] + v_own[:-1]
            for idx, (p_cur, p_prev, l_blk), vc, vp in zip(idxs, probs, v_own, v_before):
                out_scr[p][rows(idx, 0), :] = (_dot(p_cur, vc) + _dot(p_prev, vp)) / l_blk
            return 0

        lax.fori_loop(0, d * nblk // DIL_GROUP, group, 0)

    def finish(i, _):
        rs = pl.ds(pl.multiple_of(i * blk, blk), blk)
        lses = [lse_scr[p][rs, :] for p in range(npat)]
        top = functools.reduce(jnp.maximum, lses)
        den = jnp.zeros((blk, HEAD_DIM), F32)
        num = jnp.zeros((blk, HEAD_DIM), F32)
        for p in range(npat):
            w = jnp.exp2(lses[p] - top)
            den = den + w
            num = num + out_scr[p][rs, :] * w
        o_ref[rs, :] = _rms(num / den, gn_ref[...]).astype(BF16)
        return 0

    lax.fori_loop(0, seq // blk, finish, 0, unroll=2)


def dil_attention(proj, slopes, head_gain, *, batch, seq):
    t = proj.shape[0]
    scale = float(HEAD_DIM ** -0.5) * LOG2E
    gain_col0 = MLA_HEADS + SB_HEADS
    return pl.pallas_call(
        functools.partial(_dil_attn_body, seq=seq, scale=scale),
        grid=(batch, DIL_HEADS),
        in_specs=[
            pl.BlockSpec((seq, HEAD_DIM), lambda b, h: (b, DIL_COL0 + h)),
            pl.BlockSpec((seq, HEAD_DIM), lambda b, h: (b, DIL_COL0 + DIL_HEADS + h)),
            pl.BlockSpec((seq, HEAD_DIM), lambda b, h: (b, DIL_COL0 + 2 * DIL_HEADS + h)),
            pl.BlockSpec((None, 1, LANES), lambda b, h: (h, 0, 0)),
            pl.BlockSpec((1, HEAD_DIM), lambda b, h: (0, gain_col0 + h)),
        ],
        out_specs=pl.BlockSpec((seq, HEAD_DIM), lambda b, h: (b, h)),
        out_shape=jax.ShapeDtypeStruct((t, DIL_W), BF16),
        scratch_shapes=[pltpu.VMEM((seq, HEAD_DIM), F32)] * (2 * len(DIL_PATTERNS)),
        compiler_params=_cparams(2),
        name="dil_attention",
    )(proj, proj, proj, slopes, head_gain.reshape(1, -1))


def _outproj_body(x_ref, o1_ref, o2_ref, o3_ref, w1_ref, w2_ref, w3_ref, out_ref):
    acc = _dot(o1_ref[...], w1_ref[...])
    acc = acc + _dot(o2_ref[...], w2_ref[...])
    acc = acc + _dot(o3_ref[...], w3_ref[...])
    out_ref[...] = x_ref[...] + acc


def out_proj(x, o_mla, o_sb, o_dil, w1, w2, w3, *, tm=512):
    t, d = x.shape
    full = lambda a: pl.BlockSpec(a.shape, lambda i: (0, 0))
    rows = lambda a: pl.BlockSpec((tm, a.shape[1]), lambda i: (i, 0))
    return pl.pallas_call(
        _outproj_body,
        grid=(t // tm,),
        in_specs=[rows(x), rows(o_mla), rows(o_sb), rows(o_dil), full(w1), full(w2), full(w3)],
        out_specs=rows(x),
        out_shape=jax.ShapeDtypeStruct((t, d), F32),
        compiler_params=_cparams(1),
        name="out_proj",
    )(x, o_mla, o_sb, o_dil, w1, w2, w3)


def _rope_tables(seq):
    inv = ROPE_THETA ** (-jnp.arange(0, MLA_ROPE, 2, dtype=F32) / MLA_ROPE)
    ang = jnp.arange(seq, dtype=F32)[:, None] * inv[None, :]
    cos, sin = jnp.cos(ang), jnp.sin(ang)
    reps = LANES // MLA_ROPE
    cos_t = jnp.tile(jnp.concatenate([cos, cos], axis=1), (1, reps))
    sin_t = jnp.tile(jnp.concatenate([-sin, sin], axis=1), (1, reps))
    return cos_t, sin_t


def _prep_w_in(w_in):
    d = w_in.shape[0]
    z = lambda n: jnp.zeros((d, n), w_in.dtype)
    return jnp.concatenate(
        [w_in[:, :MLA_IN], z(MLA_IN_PAD - MLA_IN), w_in[:, MLA_IN:], z(LANES)], axis=1).astype(BF16)


def _prep_w_uq(w):
    r = w.shape[0]
    w = w.reshape(r, MLA_HEADS, MLA_NOPE + MLA_ROPE)
    w = jnp.pad(w, ((0, 0), (0, 0), (0, MLA_QK_PAD - MLA_NOPE - MLA_ROPE)))
    return w.reshape(r, MLA_HEADS * MLA_QK_PAD).astype(BF16)


def _prep_w_ukv(w):
    r = w.shape[0]
    w = w.reshape(r, MLA_HEADS, 2, HEAD_DIM).transpose(0, 2, 1, 3)
    return w.reshape(r, 2 * MLA_HEADS * HEAD_DIM).astype(BF16)


def kernel(x, ffn1_norm, ffn1_w_gu, ffn1_w_down, mix_norm, w_in, mla_q_norm, w_mla_uq, mla_kv_norm, w_mla_ukv, head_out_norm, w_out, ffn2_norm, ffn2_w_gu, ffn2_w_down, final_norm_g):
    batch, seq, d = x.shape
    depth = ffn1_norm.shape[0]
    cos_t, sin_t = _rope_tables(seq)
    slopes = 2.0 ** (-ALIBI_MAX_EXP * jnp.arange(1, DIL_HEADS + 1, dtype=F32) / DIL_HEADS)
    slopes = jnp.broadcast_to(slopes[:, None, None], (DIL_HEADS, 1, LANES))
    w_mla = MLA_HEADS * HEAD_DIM

    xt = x.reshape(batch * seq, d)
    for l in range(depth):
        xt = ffn_block(xt, ffn1_norm[l], ffn1_w_gu[l].astype(BF16),
                       (0.5 * ffn1_w_down[l]).astype(BF16))

        proj = norm_proj(xt, mix_norm[l], _prep_w_in(w_in[l]))
        q, k, v = mla_prep(proj, mla_q_norm[l], mla_kv_norm[l], _prep_w_uq(w_mla_uq[l]),
                           _prep_w_ukv(w_mla_ukv[l]), cos_t, sin_t, seq=seq)
        o_mla = mla_attention(q, k, v, head_out_norm[l], batch=batch, seq=seq)
        o_sb = sb_attention(proj, head_out_norm[l], batch=batch, seq=seq)
        o_dil = dil_attention(proj, slopes, head_out_norm[l], batch=batch, seq=seq)
        wo = w_out[l].astype(BF16)
        xt = out_proj(xt, o_mla, o_sb, o_dil, wo[:w_mla], wo[w_mla:w_mla + SB_W], wo[w_mla + SB_W:])

        xt = ffn_block(xt, ffn2_norm[l], ffn2_w_gu[l].astype(BF16),
                       (0.5 * ffn2_w_down[l]).astype(BF16),
                       final_norm_g if l == depth - 1 else None)
    return xt.reshape(batch, seq, d)
```

```python
import functools

import jax
import jax.numpy as jnp
from jax import lax
from jax.experimental import pallas as pl
from jax.experimental.pallas import tpu as pltpu

F32 = jnp.float32
BF16 = jnp.bfloat16

EPS = 1e-6
HEAD_DIM = 128
LANES = 128
MLA_HEADS = 6
MLA_Q_LORA = 512
MLA_KV_LORA = 256
MLA_NOPE = 128
MLA_ROPE = 64
MLA_QK_PAD = 256
ROPE_THETA = 10000.0
SB_HEADS = 4
DIL_HEADS = 6
DIL_PATTERNS = ((128, 1), (512, 4), (2048, 16))
DIL_BACK = 128
DIL_GROUP = 8
ALIBI_MAX_EXP = 8.0
NEG_BIG = -1e30
LOG2E = 1.4426950408889634
SB_KEYS = 256
FFN_TM = 1024
FFN_ROWS = 512
FFN_TF = 512
SB_SUBS = 1
SB_DEAD_BITS = 160.0

MLA_IN = MLA_Q_LORA + MLA_KV_LORA + MLA_ROPE
MLA_IN_PAD = 896
SB_W = SB_HEADS * HEAD_DIM
DIL_W = DIL_HEADS * HEAD_DIM
PROJ_HALF = MLA_IN_PAD + 3 * SB_W
assert PROJ_HALF == 3 * DIL_W + LANES
SB_COL0 = MLA_IN_PAD // LANES
DIL_COL0 = PROJ_HALF // LANES

VMEM_LIMIT = 56 * 1024 * 1024
VMEM_LIMIT_FFN = 62 * 1024 * 1024


def _cparams(grid_rank, vmem_limit=VMEM_LIMIT):
    return pltpu.CompilerParams(dimension_semantics=("arbitrary",) * grid_rank,
                                vmem_limit_bytes=vmem_limit)


def _rms(x, g):
    return x * lax.rsqrt(jnp.mean(x * x, axis=-1, keepdims=True) + EPS) * g


def _dot(a, b):
    return jnp.dot(a, b, preferred_element_type=F32)


def _dot_nt(a, b):
    return lax.dot_general(a, b, (((1,), (1,)), ((), ())), preferred_element_type=F32)


def _ffn_body(x_ref, g_ref, wg_ref, wu_ref, wd_ref, *rest, final_norm):
    fg_ref, o_ref, h_scr = rest if final_norm else (None,) + rest

    @pl.when(pl.program_id(1) == 0)
    def _():
        h_scr[...] = _rms(x_ref[...], g_ref[...]).astype(BF16)
        o_ref[...] = x_ref[...]

    for r0 in range(0, h_scr.shape[0], FFN_ROWS):
        rows = slice(r0, r0 + FFN_ROWS)
        h = h_scr[rows, :]
        g = _dot(h, wg_ref[...])
        u = _dot(h, wu_ref[...])
        a = (g * (1.0 / (1.0 + jnp.exp(-g))) * u).astype(BF16)
        o_ref[rows, :] += _dot(a, wd_ref[...])

    if final_norm:
        @pl.when(pl.program_id(1) == pl.num_programs(1) - 1)
        def _():
            o_ref[...] = _rms(o_ref[...], fg_ref[...])


def ffn_block(x, gain, w_gu, w_down_half, final_gain=None, *, tm=FFN_TM, tf=FFN_TF):
    t, d = x.shape
    f = w_down_half.shape[0]
    nf = f // tf
    final_norm = final_gain is not None
    row = pl.BlockSpec((1, d), lambda i, j: (0, 0))
    return pl.pallas_call(
        functools.partial(_ffn_body, final_norm=final_norm),
        grid=(t // tm, nf),
        in_specs=[
            pl.BlockSpec((tm, d), lambda i, j: (i, 0)),
            row,
            pl.BlockSpec((d, tf), lambda i, j: (0, j)),
            pl.BlockSpec((d, tf), lambda i, j: (0, j + nf)),
            pl.BlockSpec((tf, d), lambda i, j: (j, 0)),
        ] + ([row] if final_norm else []),
        out_specs=pl.BlockSpec((tm, d), lambda i, j: (i, 0)),
        out_shape=jax.ShapeDtypeStruct((t, d), F32),
        scratch_shapes=[pltpu.VMEM((tm, d), BF16)],
        compiler_params=_cparams(2, VMEM_LIMIT_FFN),
        name="ffn_block",
    )(x, gain.reshape(1, d), w_gu, w_gu, w_down_half,
      *([final_gain.reshape(1, d)] if final_norm else []))


def _normproj_body(x_ref, g_ref, w_ref, o_ref):
    h = _rms(x_ref[...], g_ref[...]).astype(BF16)
    o_ref[...] = _dot(h, w_ref[...])


def norm_proj(x, gain, w, *, tm=1024):
    t, d = x.shape
    n = w.shape[1]
    tn = PROJ_HALF
    return pl.pallas_call(
        _normproj_body,
        grid=(n // tn, t // tm),
        in_specs=[
            pl.BlockSpec((tm, d), lambda c, i: (i, 0)),
            pl.BlockSpec((1, d), lambda c, i: (0, 0)),
            pl.BlockSpec((d, tn), lambda c, i: (0, c), pipeline_mode=pl.Buffered(1)),
        ],
        out_specs=pl.BlockSpec((tm, tn), lambda c, i: (i, c)),
        out_shape=jax.ShapeDtypeStruct((t, n), F32),
        compiler_params=_cparams(2),
        name="norm_proj",
    )(x, gain.reshape(1, d), w)


def _mla_prep_body(a_ref, gq_ref, gkv_ref, wuq_ref, wukv_ref, cos_ref, sin_ref,
                   q_ref, k_ref, v_ref, *, scale):
    a = a_ref[...]
    qn = _rms(a[:, :MLA_Q_LORA], gq_ref[...]).astype(BF16)
    q = _dot(qn, wuq_ref[...]) * scale
    kvn = _rms(a[:, MLA_Q_LORA:MLA_Q_LORA + MLA_KV_LORA], gkv_ref[...]).astype(BF16)
    kv = _dot(kvn, wukv_ref[...])

    cos_t = cos_ref[...]
    sin_t = sin_ref[...]
    lane = lax.broadcasted_iota(jnp.int32, cos_t.shape, 1)
    first_half = (lane & (MLA_ROPE // 2)) == 0

    def rope(x):
        swapped = jnp.where(first_half,
                            pltpu.roll(x, LANES - MLA_ROPE // 2, 1),
                            pltpu.roll(x, MLA_ROPE // 2, 1))
        return x * cos_t + swapped * sin_t

    k_rope = rope(a[:, MLA_Q_LORA + MLA_KV_LORA:]).astype(BF16)
    nk = MLA_HEADS * MLA_NOPE
    for h in range(MLA_HEADS):
        c0 = h * MLA_QK_PAD
        q_ref[:, c0:c0 + LANES] = q[:, c0:c0 + LANES].astype(BF16)
        q_ref[:, c0 + LANES:c0 + 2 * LANES] = rope(q[:, c0 + LANES:c0 + 2 * LANES]).astype(BF16)
        k_ref[:, c0:c0 + LANES] = kv[:, h * MLA_NOPE:(h + 1) * MLA_NOPE].astype(BF16)
        k_ref[:, c0 + LANES:c0 + 2 * LANES] = k_rope
    v_ref[...] = kv[:, nk:].astype(BF16)


def mla_prep(proj, gq, gkv, wuq, wukv, cos_t, sin_t, *, seq, tm=1024):
    t = proj.shape[0]
    nseq = seq // tm
    wq = MLA_HEADS * MLA_QK_PAD
    wv = MLA_HEADS * HEAD_DIM
    scale = float((MLA_NOPE + MLA_ROPE) ** -0.5) * LOG2E
    return pl.pallas_call(
        functools.partial(_mla_prep_body, scale=scale),
        grid=(t // tm,),
        in_specs=[
            pl.BlockSpec((tm, MLA_IN_PAD), lambda i: (i, 0)),
            pl.BlockSpec((1, MLA_Q_LORA), lambda i: (0, 0)),
            pl.BlockSpec((1, MLA_KV_LORA), lambda i: (0, 0)),
            pl.BlockSpec((MLA_Q_LORA, wq), lambda i: (0, 0)),
            pl.BlockSpec((MLA_KV_LORA, 2 * wv), lambda i: (0, 0)),
            pl.BlockSpec((tm, LANES), lambda i: (i % nseq, 0)),
            pl.BlockSpec((tm, LANES), lambda i: (i % nseq, 0)),
        ],
        out_specs=[
            pl.BlockSpec((tm, wq), lambda i: (i, 0)),
            pl.BlockSpec((tm, wq), lambda i: (i, 0)),
            pl.BlockSpec((tm, wv), lambda i: (i, 0)),
        ],
        out_shape=[
            jax.ShapeDtypeStruct((t, wq), BF16),
            jax.ShapeDtypeStruct((t, wq), BF16),
            jax.ShapeDtypeStruct((t, wv), BF16),
        ],
        compiler_params=_cparams(1),
        name="mla_prep",
    )(proj, gq.reshape(1, -1), gkv.reshape(1, -1), wuq, wukv, cos_t, sin_t)


def _mla_attn_body(q_ref, k_ref, v_ref, gn_ref, o_ref, acc_scr, *, seq, nc, rc):
    tq = nc * rc
    row = lax.broadcasted_iota(jnp.int32, (rc, rc), 0)
    col = lax.broadcasted_iota(jnp.int32, (rc, rc), 1)
    causal = col <= row

    def run(q0, jobs, states):
        scores = [_dot_nt(q_ref[pl.ds(q0 + c * rc, rc), :], k_ref[pl.ds(k0, rc), :])
                  for c, k0, _ in jobs]
        states = list(states)
        for (c, k0, diag), s in zip(jobs, scores):
            if diag:
                s = jnp.where(causal, s, NEG_BIG)
            m, l = states[c]
            m_new = jnp.maximum(m, jnp.max(s, axis=1, keepdims=True))
            alpha = jnp.exp2(m - m_new)
            p = jnp.exp2(s - m_new)
            l = alpha * l + jnp.sum(p, axis=1, keepdims=True)
            acc_scr[c] = alpha * acc_scr[c] + _dot(p.astype(BF16), v_ref[pl.ds(k0, rc), :])
            states[c] = (m_new, l)
        return tuple(states)

    def q_loop(qi, _):
        q0 = pl.multiple_of(qi * tq, tq)
        for c in range(nc):
            acc_scr[c] = jnp.zeros((rc, HEAD_DIM), F32)
        init = tuple((jnp.full((rc, 1), NEG_BIG, F32), jnp.zeros((rc, 1), F32))
                     for _ in range(nc))

        def kv_loop(i, states):
            k0 = pl.multiple_of(i * tq, tq)
            return run(q0, [(c, k0 + t * rc, False) for t in range(nc) for c in range(nc)],
                       states)

        states = lax.fori_loop(0, qi, kv_loop, init)
        states = run(q0, [(c, q0 + t * rc, t == c) for t in range(nc) for c in range(t, nc)],
                     states)
        for c, (_, l) in enumerate(states):
            o_ref[pl.ds(q0 + c * rc, rc), :] = _rms(acc_scr[c] / l, gn_ref[...]).astype(BF16)
        return 0

    lax.fori_loop(0, seq // tq, q_loop, 0)


def mla_attention(q, k, v, head_gain, *, batch, seq, nc=2, rc=512):
    t = q.shape[0]
    return pl.pallas_call(
        functools.partial(_mla_attn_body, seq=seq, nc=nc, rc=rc),
        grid=(batch, MLA_HEADS),
        in_specs=[
            pl.BlockSpec((seq, MLA_QK_PAD), lambda b, h: (b, h)),
            pl.BlockSpec((seq, MLA_QK_PAD), lambda b, h: (b, h)),
            pl.BlockSpec((seq, HEAD_DIM), lambda b, h: (b, h)),
            pl.BlockSpec((1, HEAD_DIM), lambda b, h: (0, h)),
        ],
        out_specs=pl.BlockSpec((seq, HEAD_DIM), lambda b, h: (b, h)),
        out_shape=jax.ShapeDtypeStruct((t, MLA_HEADS * HEAD_DIM), BF16),
        scratch_shapes=[pltpu.VMEM((nc, rc, HEAD_DIM), F32)],
        compiler_params=_cparams(2),
        name="mla_attention",
    )(q, k, v, head_gain.reshape(1, -1))


def _sb_attn_body(q_ref, k_ref, v_ref, gn_ref, o_ref, qb_scr, kb_scr, vb_scr, acc_scr,
                  *, seq, rc, scale):
    tk = SB_KEYS
    own = rc // tk
    qb_scr[...] = (q_ref[...] * scale).astype(BF16)
    kb_scr[...] = k_ref[...].astype(BF16)
    vb_scr[...] = v_ref[...].astype(BF16)
    urow = lax.broadcasted_iota(jnp.int32, (tk, tk), 0)
    ucol = lax.broadcasted_iota(jnp.int32, (tk, tk), 1)
    later_keys = (urow > ucol).astype(BF16)

    def absorb(q0, r0, ks, rest, diag):
        n = rc - r0
        k0 = pl.multiple_of(ks * tk, tk)
        z = _dot_nt(qb_scr[pl.ds(q0 + r0, n), :], kb_scr[pl.ds(k0, tk), :])
        hi = jnp.maximum(z, 0.0)
        lo = jnp.minimum(z, 0.0)
        t = jnp.log(1.0 + jnp.exp2(lo - hi)) * LOG2E
        sp = hi + t
        ls = lo - t
        if diag:
            strict = (lax.broadcasted_iota(jnp.int32, (n, tk), 1)
                      < lax.broadcasted_iota(jnp.int32, (n, tk), 0))
            sp = jnp.where(strict, sp, 0.0)
        later = _dot(sp.astype(BF16), later_keys)
        a = jnp.exp2(ls - later - rest)
        if diag:
            a = jnp.where(strict, a, 0.0)
        acc_scr[r0:, :] += _dot(a.astype(BF16), vb_scr[pl.ds(k0, tk), :])
        return rest + later[:, 0:1] + sp[:, 0:1]

    def chunk(ci, _):
        q0 = pl.multiple_of(ci * rc, rc)
        acc_scr[...] = jnp.zeros((rc, HEAD_DIM), F32)
        rest = jnp.zeros((tk, 1), F32)
        for j in range(own - 1, -1, -1):
            rest = absorb(q0, j * tk, ci * own + j, rest, True)
            if j > 0:
                rest = jnp.concatenate([jnp.zeros((tk, 1), F32), rest], axis=0)

        def alive(state):
            ks, rest = state
            return jnp.logical_and(ks >= 0, jnp.min(rest) < SB_DEAD_BITS)

        def walk(state):
            ks, rest = state
            for t in range(SB_SUBS):
                rest = absorb(q0, 0, ks - t, rest, False)
            return ks - SB_SUBS, rest

        lax.while_loop(alive, walk, (ci * own - 1, rest))
        o_ref[pl.ds(q0, rc), :] = _rms(acc_scr[...], gn_ref[...]).astype(BF16)
        return 0

    lax.fori_loop(0, seq // rc, chunk, 0)


def sb_attention(proj, head_gain, *, batch, seq, rc=512):
    assert (rc // SB_KEYS) % SB_SUBS == 0
    t = proj.shape[0]
    scale = float(HEAD_DIM ** -0.5) * LOG2E
    gain_col0 = MLA_HEADS
    return pl.pallas_call(
        functools.partial(_sb_attn_body, seq=seq, rc=rc, scale=scale),
        grid=(batch, SB_HEADS),
        in_specs=[
            pl.BlockSpec((seq, HEAD_DIM), lambda b, h: (b, SB_COL0 + h)),
            pl.BlockSpec((seq, HEAD_DIM), lambda b, h: (b, SB_COL0 + SB_HEADS + h)),
            pl.BlockSpec((seq, HEAD_DIM), lambda b, h: (b, SB_COL0 + 2 * SB_HEADS + h)),
            pl.BlockSpec((1, HEAD_DIM), lambda b, h: (0, gain_col0 + h)),
        ],
        out_specs=pl.BlockSpec((seq, HEAD_DIM), lambda b, h: (b, h)),
        out_shape=jax.ShapeDtypeStruct((t, SB_W), BF16),
        scratch_shapes=[pltpu.VMEM((seq, HEAD_DIM), BF16),
                        pltpu.VMEM((seq, HEAD_DIM), BF16),
                        pltpu.VMEM((seq, HEAD_DIM), BF16),
                        pltpu.VMEM((rc, HEAD_DIM), F32)],
        compiler_params=_cparams(2),
        name="sb_attention",
    )(proj, proj, proj, head_gain.reshape(1, -1))


def _dil_attn_body(q_ref, k_ref, v_ref, slope_ref, gn_ref, o_ref, *scr, seq, scale):
    blk = DIL_BACK
    npat = len(DIL_PATTERNS)
    lse_scr, out_scr = scr[:npat], scr[npat:]
    row = lax.broadcasted_iota(jnp.int32, (blk, blk), 0)
    col = lax.broadcasted_iota(jnp.int32, (blk, blk), 1)
    dist_cur = (row - col).astype(F32)
    dist_prev = dist_cur + float(blk)
    ok_cur = row >= col
    ok_prev_band = row <= col
    slope = slope_ref[...]

    for p, (window, d) in enumerate(DIL_PATTERNS):
        assert window // d == blk
        nblk = seq // d // blk
        assert nblk & (nblk - 1) == 0
        bias_cur = -(slope * (float(d) * LOG2E)) * dist_cur
        bias_prev = -(slope * (float(d) * LOG2E)) * dist_prev

        def group(gi, _, p=p, d=d, nblk=nblk, bias_cur=bias_cur, bias_prev=bias_prev):
            def rows(idx, back):
                r = lax.shift_right_logical(idx, nblk.bit_length() - 1)
                nb = lax.bitwise_and(idx, nblk - 1)
                b = jnp.maximum(nb - back, 0)
                start = r + b * (d * blk)
                if d == 1:
                    return pl.ds(pl.multiple_of(start, blk), blk)
                return pl.ds(start, blk, stride=d)

            idxs = [gi * DIL_GROUP + g for g in range(DIL_GROUP)]
            k_own = [k_ref[rows(idx, 0), :].astype(BF16) for idx in idxs]
            k_before = [k_ref[rows(idxs[0], 1), :].astype(BF16)] + k_own[:-1]
            scores = []
            for idx, kc, kp in zip(idxs, k_own, k_before):
                q = (q_ref[rows(idx, 0), :] * scale).astype(BF16)
                scores.append((_dot_nt(q, kc), _dot_nt(q, kp)))
            probs = []
            for idx, (s_cur, s_prev) in zip(idxs, scores):
                nb = lax.bitwise_and(idx, nblk - 1)
                s_cur = jnp.where(ok_cur, s_cur + bias_cur, NEG_BIG)
                no_prev = jnp.where(nb > 0, 0.0, NEG_BIG)
                s_prev = jnp.where(ok_prev_band, s_prev + bias_prev + no_prev, NEG_BIG)
                m_blk = jnp.max(jnp.maximum(s_cur, s_prev), axis=-1, keepdims=True)
                p_cur = jnp.exp2(s_cur - m_blk)
                p_prev = jnp.exp2(s_prev - m_blk)
                l_blk = jnp.broadcast_to(jnp.sum(p_cur + p_prev, axis=-1, keepdims=True),
                                         (blk, HEAD_DIM))
                lse_scr[p][rows(idx, 0), :] = m_blk + jnp.log(l_blk) * LOG2E
                probs.append((p_cur.astype(BF16), p_prev.astype(BF16), l_blk))
            v_own = [v_ref[rows(idx, 0), :].astype(BF16) for idx in idxs]
            v_before = [v_ref[rows(idxs[0], 1), :].astype(BF16)] + v_own[:-1]
            for idx, (p_cur, p_prev, l_blk), vc, vp in zip(idxs, probs, v_own, v_before):
                out_scr[p][rows(idx, 0), :] = (_dot(p_cur, vc) + _dot(p_prev, vp)) / l_blk
            return 0

        lax.fori_loop(0, d * nblk // DIL_GROUP, group, 0)

    def finish(i, _):
        rs = pl.ds(pl.multiple_of(i * blk, blk), blk)
        lses = [lse_scr[p][rs, :] for p in range(npat)]
        top = functools.reduce(jnp.maximum, lses)
        den = jnp.zeros((blk, HEAD_DIM), F32)
        num = jnp.zeros((blk, HEAD_DIM), F32)
        for p in range(npat):
            w = jnp.exp2(lses[p] - top)
            den = den + w
            num = num + out_scr[p][rs, :] * w
        o_ref[rs, :] = _rms(num / den, gn_ref[...]).astype(BF16)
        return 0

    lax.fori_loop(0, seq // blk, finish, 0, unroll=2)


def dil_attention(proj, slopes, head_gain, *, batch, seq):
    t = proj.shape[0]
    scale = float(HEAD_DIM ** -0.5) * LOG2E
    gain_col0 = MLA_HEADS + SB_HEADS
    return pl.pallas_call(
        functools.partial(_dil_attn_body, seq=seq, scale=scale),
        grid=(batch, DIL_HEADS),
        in_specs=[
            pl.BlockSpec((seq, HEAD_DIM), lambda b, h: (b, DIL_COL0 + h)),
            pl.BlockSpec((seq, HEAD_DIM), lambda b, h: (b, DIL_COL0 + DIL_HEADS + h)),
            pl.BlockSpec((seq, HEAD_DIM), lambda b, h: (b, DIL_COL0 + 2 * DIL_HEADS + h)),
            pl.BlockSpec((None, 1, LANES), lambda b, h: (h, 0, 0)),
            pl.BlockSpec((1, HEAD_DIM), lambda b, h: (0, gain_col0 + h)),
        ],
        out_specs=pl.BlockSpec((seq, HEAD_DIM), lambda b, h: (b, h)),
        out_shape=jax.ShapeDtypeStruct((t, DIL_W), BF16),
        scratch_shapes=[pltpu.VMEM((seq, HEAD_DIM), F32)] * (2 * len(DIL_PATTERNS)),
        compiler_params=_cparams(2),
        name="dil_attention",
    )(proj, proj, proj, slopes, head_gain.reshape(1, -1))


def _outproj_body(x_ref, o1_ref, o2_ref, o3_ref, w1_ref, w2_ref, w3_ref, out_ref):
    acc = _dot(o1_ref[...], w1_ref[...])
    acc = acc + _dot(o2_ref[...], w2_ref[...])
    acc = acc + _dot(o3_ref[...], w3_ref[...])
    out_ref[...] = x_ref[...] + acc


def out_proj(x, o_mla, o_sb, o_dil, w1, w2, w3, *, tm=512):
    t, d = x.shape
    full = lambda a: pl.BlockSpec(a.shape, lambda i: (0, 0), pipeline_mode=pl.Buffered(1))
    rows = lambda a: pl.BlockSpec((tm, a.shape[1]), lambda i: (i, 0))
    return pl.pallas_call(
        _outproj_body,
        grid=(t // tm,),
        in_specs=[rows(x), rows(o_mla), rows(o_sb), rows(o_dil), full(w1), full(w2), full(w3)],
        out_specs=rows(x),
        out_shape=jax.ShapeDtypeStruct((t, d), F32),
        compiler_params=_cparams(1),
        name="out_proj",
    )(x, o_mla, o_sb, o_dil, w1, w2, w3)


def _rope_tables(seq):
    inv = ROPE_THETA ** (-jnp.arange(0, MLA_ROPE, 2, dtype=F32) / MLA_ROPE)
    ang = jnp.arange(seq, dtype=F32)[:, None] * inv[None, :]
    cos, sin = jnp.cos(ang), jnp.sin(ang)
    reps = LANES // MLA_ROPE
    cos_t = jnp.tile(jnp.concatenate([cos, cos], axis=1), (1, reps))
    sin_t = jnp.tile(jnp.concatenate([-sin, sin], axis=1), (1, reps))
    return cos_t, sin_t


def _prep_w_in(w_in):
    d = w_in.shape[0]
    w = w_in.astype(BF16)
    z = lambda n: jnp.zeros((d, n), BF16)
    return jnp.concatenate(
        [w[:, :MLA_IN], z(MLA_IN_PAD - MLA_IN), w[:, MLA_IN:], z(LANES)], axis=1)


def _prep_w_uq(w):
    r = w.shape[0]
    w = w.reshape(r, MLA_HEADS, MLA_NOPE + MLA_ROPE)
    w = jnp.pad(w, ((0, 0), (0, 0), (0, MLA_QK_PAD - MLA_NOPE - MLA_ROPE)))
    return w.reshape(r, MLA_HEADS * MLA_QK_PAD).astype(BF16)


def _prep_w_ukv(w):
    r = w.shape[0]
    w = w.reshape(r, MLA_HEADS, 2, HEAD_DIM).transpose(0, 2, 1, 3)
    return w.reshape(r, 2 * MLA_HEADS * HEAD_DIM).astype(BF16)


def kernel(x, ffn1_norm, ffn1_w_gu, ffn1_w_down, mix_norm, w_in, mla_q_norm, w_mla_uq, mla_kv_norm, w_mla_ukv, head_out_norm, w_out, ffn2_norm, ffn2_w_gu, ffn2_w_down, final_norm_g):
    batch, seq, d = x.shape
    depth = ffn1_norm.shape[0]
    cos_t, sin_t = _rope_tables(seq)
    slopes = 2.0 ** (-ALIBI_MAX_EXP * jnp.arange(1, DIL_HEADS + 1, dtype=F32) / DIL_HEADS)
    slopes = jnp.broadcast_to(slopes[:, None, None], (DIL_HEADS, 1, LANES))
    w_mla = MLA_HEADS * HEAD_DIM

    xt = x.reshape(batch * seq, d)
    for l in range(depth):
        xt = ffn_block(xt, ffn1_norm[l], ffn1_w_gu[l].astype(BF16),
                       (0.5 * ffn1_w_down[l]).astype(BF16))

        proj = norm_proj(xt, mix_norm[l], _prep_w_in(w_in[l]))
        q, k, v = mla_prep(proj, mla_q_norm[l], mla_kv_norm[l], _prep_w_uq(w_mla_uq[l]),
                           _prep_w_ukv(w_mla_ukv[l]), cos_t, sin_t, seq=seq)
        o_mla = mla_attention(q, k, v, head_out_norm[l], batch=batch, seq=seq)
        o_sb = sb_attention(proj, head_out_norm[l], batch=batch, seq=seq)
        o_dil = dil_attention(proj, slopes, head_out_norm[l], batch=batch, seq=seq)
        wo = w_out[l].astype(BF16)
        xt = out_proj(xt, o_mla, o_sb, o_dil, wo[:w_mla], wo[w_mla:w_mla + SB_W], wo[w_mla + SB_W:])

        xt = ffn_block(xt, ffn2_norm[l], ffn2_w_gu[l].astype(BF16),
                       (0.5 * ffn2_w_down[l]).astype(BF16),
                       final_norm_g if l == depth - 1 else None)
    return xt.reshape(batch, seq, d)
```

```python
import functools

import jax
import jax.numpy as jnp
from jax import lax
from jax.experimental import pallas as pl
from jax.experimental.pallas import tpu as pltpu

F32 = jnp.float32
BF16 = jnp.bfloat16

EPS = 1e-6
HEAD_DIM = 128
LANES = 128
MLA_HEADS = 6
MLA_Q_LORA = 512
MLA_KV_LORA = 256
MLA_NOPE = 128
MLA_ROPE = 64
MLA_QK_PAD = 256
ROPE_THETA = 10000.0
SB_HEADS = 4
DIL_HEADS = 6
DIL_PATTERNS = ((128, 1), (512, 4), (2048, 16))
DIL_BACK = 128
DIL_GROUP = 8
ALIBI_MAX_EXP = 8.0
NEG_BIG = -1e30
LOG2E = 1.4426950408889634
SB_KEYS = 256
FFN_TM = 1024
FFN_ROWS = 512
FFN_TF = 512
SB_SUBS = 1
SB_DEAD_BITS = 160.0

MLA_IN = MLA_Q_LORA + MLA_KV_LORA + MLA_ROPE
MLA_IN_PAD = 896
SB_W = SB_HEADS * HEAD_DIM
DIL_W = DIL_HEADS * HEAD_DIM
PROJ_HALF = MLA_IN_PAD + 3 * SB_W
assert PROJ_HALF == 3 * DIL_W + LANES
SB_COL0 = MLA_IN_PAD // LANES
DIL_COL0 = PROJ_HALF // LANES

VMEM_LIMIT = 56 * 1024 * 1024
VMEM_LIMIT_FFN = 62 * 1024 * 1024


def _cparams(grid_rank, vmem_limit=VMEM_LIMIT):
    return pltpu.CompilerParams(dimension_semantics=("arbitrary",) * grid_rank,
                                vmem_limit_bytes=vmem_limit)


def _rms(x, g):
    return x * lax.rsqrt(jnp.mean(x * x, axis=-1, keepdims=True) + EPS) * g


def _dot(a, b):
    return jnp.dot(a, b, preferred_element_type=F32)


def _dot_nt(a, b):
    return lax.dot_general(a, b, (((1,), (1,)), ((), ())), preferred_element_type=F32)


def _ffn_body(x_ref, g_ref, wg_ref, wu_ref, wd_ref, *rest, final_norm):
    fg_ref, o_ref, h_scr = rest if final_norm else (None,) + rest

    def hidden_tile(first):
        for r0 in range(0, h_scr.shape[0], FFN_ROWS):
            rows = slice(r0, r0 + FFN_ROWS)
            if first:
                h = _rms(x_ref[rows, :], g_ref[...]).astype(BF16)
                h_scr[rows, :] = h
            else:
                h = h_scr[rows, :]
            g = _dot(h, wg_ref[...])
            u = _dot(h, wu_ref[...])
            a = (g * (1.0 / (1.0 + jnp.exp(-g))) * u).astype(BF16)
            base = x_ref[rows, :] if first else o_ref[rows, :]
            o_ref[rows, :] = base + _dot(a, wd_ref[...])

    @pl.when(pl.program_id(1) == 0)
    def _():
        hidden_tile(True)

    @pl.when(pl.program_id(1) > 0)
    def _():
        hidden_tile(False)

    if final_norm:
        @pl.when(pl.program_id(1) == pl.num_programs(1) - 1)
        def _():
            o_ref[...] = _rms(o_ref[...], fg_ref[...])


def ffn_block(x, gain, w_gu, w_down_half, final_gain=None, *, tm=FFN_TM, tf=FFN_TF):
    t, d = x.shape
    f = w_down_half.shape[0]
    nf = f // tf
    final_norm = final_gain is not None
    row = pl.BlockSpec((1, d), lambda i, j: (0, 0))
    return pl.pallas_call(
        functools.partial(_ffn_body, final_norm=final_norm),
        grid=(t // tm, nf),
        in_specs=[
            pl.BlockSpec((tm, d), lambda i, j: (i, 0)),
            row,
            pl.BlockSpec((d, tf), lambda i, j: (0, j)),
            pl.BlockSpec((d, tf), lambda i, j: (0, j + nf)),
            pl.BlockSpec((tf, d), lambda i, j: (j, 0)),
        ] + ([row] if final_norm else []),
        out_specs=pl.BlockSpec((tm, d), lambda i, j: (i, 0)),
        out_shape=jax.ShapeDtypeStruct((t, d), F32),
        scratch_shapes=[pltpu.VMEM((tm, d), BF16)],
        compiler_params=_cparams(2, VMEM_LIMIT_FFN),
        name="ffn_block",
    )(x, gain.reshape(1, d), w_gu, w_gu, w_down_half,
      *([final_gain.reshape(1, d)] if final_norm else []))


def _normproj_body(x_ref, g_ref, w_ref, o_ref):
    for r0 in range(0, x_ref.shape[0], FFN_ROWS):
        rows = slice(r0, r0 + FFN_ROWS)
        h = _rms(x_ref[rows, :], g_ref[...]).astype(BF16)
        o_ref[rows, :] = _dot(h, w_ref[...])


def norm_proj(x, gain, w, *, tm=1024):
    t, d = x.shape
    n = w.shape[1]
    tn = PROJ_HALF
    return pl.pallas_call(
        _normproj_body,
        grid=(n // tn, t // tm),
        in_specs=[
            pl.BlockSpec((tm, d), lambda c, i: (i, 0)),
            pl.BlockSpec((1, d), lambda c, i: (0, 0)),
            pl.BlockSpec((d, tn), lambda c, i: (0, c), pipeline_mode=pl.Buffered(1)),
        ],
        out_specs=pl.BlockSpec((tm, tn), lambda c, i: (i, c)),
        out_shape=jax.ShapeDtypeStruct((t, n), F32),
        compiler_params=_cparams(2),
        name="norm_proj",
    )(x, gain.reshape(1, d), w)


def _mla_prep_body(a_ref, gq_ref, gkv_ref, wuq_ref, wukv_ref, cos_ref, sin_ref,
                   q_ref, k_ref, v_ref, *, scale):
    a = a_ref[...]
    qn = _rms(a[:, :MLA_Q_LORA], gq_ref[...]).astype(BF16)
    q = _dot(qn, wuq_ref[...]) * scale
    kvn = _rms(a[:, MLA_Q_LORA:MLA_Q_LORA + MLA_KV_LORA], gkv_ref[...]).astype(BF16)
    kv = _dot(kvn, wukv_ref[...])

    cos_t = cos_ref[...]
    sin_t = sin_ref[...]
    lane = lax.broadcasted_iota(jnp.int32, cos_t.shape, 1)
    first_half = (lane & (MLA_ROPE // 2)) == 0

    def rope(x):
        swapped = jnp.where(first_half,
                            pltpu.roll(x, LANES - MLA_ROPE // 2, 1),
                            pltpu.roll(x, MLA_ROPE // 2, 1))
        return x * cos_t + swapped * sin_t

    k_rope = rope(a[:, MLA_Q_LORA + MLA_KV_LORA:]).astype(BF16)
    nk = MLA_HEADS * MLA_NOPE
    for h in range(MLA_HEADS):
        c0 = h * MLA_QK_PAD
        q_ref[:, c0:c0 + LANES] = q[:, c0:c0 + LANES].astype(BF16)
        q_ref[:, c0 + LANES:c0 + 2 * LANES] = rope(q[:, c0 + LANES:c0 + 2 * LANES]).astype(BF16)
        k_ref[:, c0:c0 + LANES] = kv[:, h * MLA_NOPE:(h + 1) * MLA_NOPE].astype(BF16)
        k_ref[:, c0 + LANES:c0 + 2 * LANES] = k_rope
    v_ref[...] = kv[:, nk:].astype(BF16)


def mla_prep(proj, gq, gkv, wuq, wukv, cos_t, sin_t, *, seq, tm=1024):
    t = proj.shape[0]
    nseq = seq // tm
    wq = MLA_HEADS * MLA_QK_PAD
    wv = MLA_HEADS * HEAD_DIM
    scale = float((MLA_NOPE + MLA_ROPE) ** -0.5) * LOG2E
    return pl.pallas_call(
        functools.partial(_mla_prep_body, scale=scale),
        grid=(t // tm,),
        in_specs=[
            pl.BlockSpec((tm, MLA_IN_PAD), lambda i: (i, 0)),
            pl.BlockSpec((1, MLA_Q_LORA), lambda i: (0, 0)),
            pl.BlockSpec((1, MLA_KV_LORA), lambda i: (0, 0)),
            pl.BlockSpec((MLA_Q_LORA, wq), lambda i: (0, 0)),
            pl.BlockSpec((MLA_KV_LORA, 2 * wv), lambda i: (0, 0)),
            pl.BlockSpec((tm, LANES), lambda i: (i % nseq, 0)),
            pl.BlockSpec((tm, LANES), lambda i: (i % nseq, 0)),
        ],
        out_specs=[
            pl.BlockSpec((tm, wq), lambda i: (i, 0)),
            pl.BlockSpec((tm, wq), lambda i: (i, 0)),
            pl.BlockSpec((tm, wv), lambda i: (i, 0)),
        ],
        out_shape=[
            jax.ShapeDtypeStruct((t, wq), BF16),
            jax.ShapeDtypeStruct((t, wq), BF16),
            jax.ShapeDtypeStruct((t, wv), BF16),
        ],
        compiler_params=_cparams(1),
        name="mla_prep",
    )(proj, gq.reshape(1, -1), gkv.reshape(1, -1), wuq, wukv, cos_t, sin_t)


def _mla_attn_body(q_ref, k_ref, v_ref, gn_ref, o_ref, acc_scr, *, seq, nc, rc):
    tq = nc * rc
    row = lax.broadcasted_iota(jnp.int32, (rc, rc), 0)
    col = lax.broadcasted_iota(jnp.int32, (rc, rc), 1)
    causal = col <= row

    def run(q0, jobs, states):
        scores = [_dot_nt(q_ref[pl.ds(q0 + c * rc, rc), :], k_ref[pl.ds(k0, rc), :])
                  for c, k0, _ in jobs]
        states = list(states)
        for (c, k0, diag), s in zip(jobs, scores):
            if diag:
                s = jnp.where(causal, s, NEG_BIG)
            m, l = states[c]
            m_new = jnp.maximum(m, jnp.max(s, axis=1, keepdims=True))
            alpha = jnp.exp2(m - m_new)
            p = jnp.exp2(s - m_new)
            l = alpha * l + jnp.sum(p, axis=1, keepdims=True)
            acc_scr[c] = alpha * acc_scr[c] + _dot(p.astype(BF16), v_ref[pl.ds(k0, rc), :])
            states[c] = (m_new, l)
        return tuple(states)

    def q_loop(qi, _):
        q0 = pl.multiple_of(qi * tq, tq)
        for c in range(nc):
            acc_scr[c] = jnp.zeros((rc, HEAD_DIM), F32)
        init = tuple((jnp.full((rc, 1), NEG_BIG, F32), jnp.zeros((rc, 1), F32))
                     for _ in range(nc))

        def kv_loop(i, states):
            k0 = pl.multiple_of(i * tq, tq)
            return run(q0, [(c, k0 + t * rc, False) for t in range(nc) for c in range(nc)],
                       states)

        states = lax.fori_loop(0, qi, kv_loop, init)
        states = run(q0, [(c, q0 + t * rc, t == c) for t in range(nc) for c in range(t, nc)],
                     states)
        for c, (_, l) in enumerate(states):
            o_ref[pl.ds(q0 + c * rc, rc), :] = _rms(acc_scr[c] / l, gn_ref[...]).astype(BF16)
        return 0

    lax.fori_loop(0, seq // tq, q_loop, 0)


def mla_attention(q, k, v, head_gain, *, batch, seq, nc=2, rc=512):
    t = q.shape[0]
    return pl.pallas_call(
        functools.partial(_mla_attn_body, seq=seq, nc=nc, rc=rc),
        grid=(batch, MLA_HEADS),
        in_specs=[
            pl.BlockSpec((seq, MLA_QK_PAD), lambda b, h: (b, h)),
            pl.BlockSpec((seq, MLA_QK_PAD), lambda b, h: (b, h)),
            pl.BlockSpec((seq, HEAD_DIM), lambda b, h: (b, h)),
            pl.BlockSpec((1, HEAD_DIM), lambda b, h: (0, h)),
        ],
        out_specs=pl.BlockSpec((seq, HEAD_DIM), lambda b, h: (b, h)),
        out_shape=jax.ShapeDtypeStruct((t, MLA_HEADS * HEAD_DIM), BF16),
        scratch_shapes=[pltpu.VMEM((nc, rc, HEAD_DIM), F32)],
        compiler_params=_cparams(2),
        name="mla_attention",
    )(q, k, v, head_gain.reshape(1, -1))


def _sb_attn_body(q_ref, k_ref, v_ref, gn_ref, o_ref, qb_scr, kb_scr, vb_scr, acc_scr,
                  *, seq, rc, scale):
    tk = SB_KEYS
    own = rc // tk
    qb_scr[...] = (q_ref[...] * scale).astype(BF16)
    kb_scr[...] = k_ref[...].astype(BF16)
    vb_scr[...] = v_ref[...].astype(BF16)
    urow = lax.broadcasted_iota(jnp.int32, (tk, tk), 0)
    ucol = lax.broadcasted_iota(jnp.int32, (tk, tk), 1)
    later_keys = (urow > ucol).astype(BF16)

    def absorb(q0, r0, ks, rest, diag):
        n = rc - r0
        k0 = pl.multiple_of(ks * tk, tk)
        z = _dot_nt(qb_scr[pl.ds(q0 + r0, n), :], kb_scr[pl.ds(k0, tk), :])
        hi = jnp.maximum(z, 0.0)
        lo = jnp.minimum(z, 0.0)
        t = jnp.log(1.0 + jnp.exp2(lo - hi)) * LOG2E
        sp = hi + t
        ls = lo - t
        if diag:
            strict = (lax.broadcasted_iota(jnp.int32, (n, tk), 1)
                      < lax.broadcasted_iota(jnp.int32, (n, tk), 0))
            sp = jnp.where(strict, sp, 0.0)
        later = _dot(sp.astype(BF16), later_keys)
        a = jnp.exp2(ls - later - rest)
        if diag:
            a = jnp.where(strict, a, 0.0)
        acc_scr[r0:, :] += _dot(a.astype(BF16), vb_scr[pl.ds(k0, tk), :])
        return rest + later[:, 0:1] + sp[:, 0:1]

    def chunk(ci, _):
        q0 = pl.multiple_of(ci * rc, rc)
        acc_scr[...] = jnp.zeros((rc, HEAD_DIM), F32)
        rest = jnp.zeros((tk, 1), F32)
        for j in range(own - 1, -1, -1):
            rest = absorb(q0, j * tk, ci * own + j, rest, True)
            if j > 0:
                rest = jnp.concatenate([jnp.zeros((tk, 1), F32), rest], axis=0)

        def alive(state):
            ks, rest = state
            return jnp.logical_and(ks >= 0, jnp.min(rest) < SB_DEAD_BITS)

        def walk(state):
            ks, rest = state
            for t in range(SB_SUBS):
                rest = absorb(q0, 0, ks - t, rest, False)
            return ks - SB_SUBS, rest

        lax.while_loop(alive, walk, (ci * own - 1, rest))
        o_ref[pl.ds(q0, rc), :] = _rms(acc_scr[...], gn_ref[...]).astype(BF16)
        return 0

    lax.fori_loop(0, seq // rc, chunk, 0)


def sb_attention(proj, head_gain, *, batch, seq, rc=512):
    assert (rc // SB_KEYS) % SB_SUBS == 0
    t = proj.shape[0]
    scale = float(HEAD_DIM ** -0.5) * LOG2E
    gain_col0 = MLA_HEADS
    return pl.pallas_call(
        functools.partial(_sb_attn_body, seq=seq, rc=rc, scale=scale),
        grid=(batch, SB_HEADS),
        in_specs=[
            pl.BlockSpec((seq, HEAD_DIM), lambda b, h: (b, SB_COL0 + h)),
            pl.BlockSpec((seq, HEAD_DIM), lambda b, h: (b, SB_COL0 + SB_HEADS + h)),
            pl.BlockSpec((seq, HEAD_DIM), lambda b, h: (b, SB_COL0 + 2 * SB_HEADS + h)),
            pl.BlockSpec((1, HEAD_DIM), lambda b, h: (0, gain_col0 + h)),
        ],
        out_specs=pl.BlockSpec((seq, HEAD_DIM), lambda b, h: (b, h)),
        out_shape=jax.ShapeDtypeStruct((t, SB_W), BF16),
        scratch_shapes=[pltpu.VMEM((seq, HEAD_DIM), BF16),
                        pltpu.VMEM((seq, HEAD_DIM), BF16),
                        pltpu.VMEM((seq, HEAD_DIM), BF16),
                        pltpu.VMEM((rc, HEAD_DIM), F32)],
        compiler_params=_cparams(2),
        name="sb_attention",
    )(proj, proj, proj, head_gain.reshape(1, -1))


def _dil_attn_body(q_ref, k_ref, v_ref, slope_ref, gn_ref, o_ref, *scr, seq, scale):
    blk = DIL_BACK
    npat = len(DIL_PATTERNS)
    lse_scr, out_scr = scr[:npat], scr[npat:]
    row = lax.broadcasted_iota(jnp.int32, (blk, blk), 0)
    col = lax.broadcasted_iota(jnp.int32, (blk, blk), 1)
    dist_cur = (row - col).astype(F32)
    dist_prev = dist_cur + float(blk)
    ok_cur = row >= col
    ok_prev_band = row <= col
    slope = slope_ref[...]

    for p, (window, d) in enumerate(DIL_PATTERNS):
        assert window // d == blk
        nblk = seq // d // blk
        assert nblk & (nblk - 1) == 0
        bias_cur = -(slope * (float(d) * LOG2E)) * dist_cur
        bias_prev = -(slope * (float(d) * LOG2E)) * dist_prev

        def group(gi, _, p=p, d=d, nblk=nblk, bias_cur=bias_cur, bias_prev=bias_prev):
            def rows(idx, back):
                r = lax.shift_right_logical(idx, nblk.bit_length() - 1)
                nb = lax.bitwise_and(idx, nblk - 1)
                b = jnp.maximum(nb - back, 0)
                start = r + b * (d * blk)
                if d == 1:
                    return pl.ds(pl.multiple_of(start, blk), blk)
                return pl.ds(start, blk, stride=d)

            idxs = [gi * DIL_GROUP + g for g in range(DIL_GROUP)]
            k_own = [k_ref[rows(idx, 0), :].astype(BF16) for idx in idxs]
            k_before = [k_ref[rows(idxs[0], 1), :].astype(BF16)] + k_own[:-1]
            scores = []
            for idx, kc, kp in zip(idxs, k_own, k_before):
                q = (q_ref[rows(idx, 0), :] * scale).astype(BF16)
                scores.append((_dot_nt(q, kc), _dot_nt(q, kp)))
            probs = []
            for idx, (s_cur, s_prev) in zip(idxs, scores):
                nb = lax.bitwise_and(idx, nblk - 1)
                s_cur = jnp.where(ok_cur, s_cur + bias_cur, NEG_BIG)
                no_prev = jnp.where(nb > 0, 0.0, NEG_BIG)
                s_prev = jnp.where(ok_prev_band, s_prev + bias_prev + no_prev, NEG_BIG)
                m_blk = jnp.max(jnp.maximum(s_cur, s_prev), axis=-1, keepdims=True)
                p_cur = jnp.exp2(s_cur - m_blk)
                p_prev = jnp.exp2(s_prev - m_blk)
                l_blk = jnp.broadcast_to(jnp.sum(p_cur + p_prev, axis=-1, keepdims=True),
                                         (blk, HEAD_DIM))
                lse_scr[p][rows(idx, 0), :] = m_blk + jnp.log(l_blk) * LOG2E
                probs.append((p_cur.astype(BF16), p_prev.astype(BF16), l_blk))
            v_own = [v_ref[rows(idx, 0), :].astype(BF16) for idx in idxs]
            v_before = [v_ref[rows(idxs[0], 1), :].astype(BF16)] + v_own[:-1]
            for idx, (p_cur, p_prev, l_blk), vc, vp in zip(idxs, probs, v_own, v_before):
                out_scr[p][rows(idx, 0), :] = (_dot(p_cur, vc) + _dot(p_prev, vp)) / l_blk
            return 0

        lax.fori_loop(0, d * nblk // DIL_GROUP, group, 0)

    def finish(i, _):
        rs = pl.ds(pl.multiple_of(i * blk, blk), blk)
        lses = [lse_scr[p][rs, :] for p in range(npat)]
        top = functools.reduce(jnp.maximum, lses)
        den = jnp.zeros((blk, HEAD_DIM), F32)
        num = jnp.zeros((blk, HEAD_DIM), F32)
        for p in range(npat):
            w = jnp.exp2(lses[p] - top)
            den = den + w
            num = num + out_scr[p][rs, :] * w
        o_ref[rs, :] = _rms(num / den, gn_ref[...]).astype(BF16)
        return 0

    lax.fori_loop(0, seq // blk, finish, 0, unroll=2)


def dil_attention(proj, slopes, head_gain, *, batch, seq):
    t = proj.shape[0]
    scale = float(HEAD_DIM ** -0.5) * LOG2E
    gain_col0 = MLA_HEADS + SB_HEADS
    return pl.pallas_call(
        functools.partial(_dil_attn_body, seq=seq, scale=scale),
        grid=(batch, DIL_HEADS),
        in_specs=[
            pl.BlockSpec((seq, HEAD_DIM), lambda b, h: (b, DIL_COL0 + h)),
            pl.BlockSpec((seq, HEAD_DIM), lambda b, h: (b, DIL_COL0 + DIL_HEADS + h)),
            pl.BlockSpec((seq, HEAD_DIM), lambda b, h: (b, DIL_COL0 + 2 * DIL_HEADS + h)),
            pl.BlockSpec((None, 1, LANES), lambda b, h: (h, 0, 0)),
            pl.BlockSpec((1, HEAD_DIM), lambda b, h: (0, gain_col0 + h)),
        ],
        out_specs=pl.BlockSpec((seq, HEAD_DIM), lambda b, h: (b, h)),
        out_shape=jax.ShapeDtypeStruct((t, DIL_W), BF16),
        scratch_shapes=[pltpu.VMEM((seq, HEAD_DIM), F32)] * (2 * len(DIL_PATTERNS)),
        compiler_params=_cparams(2),
        name="dil_attention",
    )(proj, proj, proj, slopes, head_gain.reshape(1, -1))


def _outproj_body(x_ref, o1_ref, o2_ref, o3_ref, w1_ref, w2_ref, w3_ref, out_ref):
    acc = _dot(o1_ref[...], w1_ref[...])
    acc = acc + _dot(o2_ref[...], w2_ref[...])
    acc = acc + _dot(o3_ref[...], w3_ref[...])
    out_ref[...] = x_ref[...] + acc


def out_proj(x, o_mla, o_sb, o_dil, w1, w2, w3, *, tm=512):
    t, d = x.shape
    full = lambda a: pl.BlockSpec(a.shape, lambda i: (0, 0), pipeline_mode=pl.Buffered(1))
    rows = lambda a: pl.BlockSpec((tm, a.shape[1]), lambda i: (i, 0))
    return pl.pallas_call(
        _outproj_body,
        grid=(t // tm,),
        in_specs=[rows(x), rows(o_mla), rows(o_sb), rows(o_dil), full(w1), full(w2), full(w3)],
        out_specs=rows(x),
        out_shape=jax.ShapeDtypeStruct((t, d), F32),
        compiler_params=_cparams(1),
        name="out_proj",
    )(x, o_mla, o_sb, o_dil, w1, w2, w3)


def _rope_tables(seq):
    inv = ROPE_THETA ** (-jnp.arange(0, MLA_ROPE, 2, dtype=F32) / MLA_ROPE)
    ang = jnp.arange(seq, dtype=F32)[:, None] * inv[None, :]
    cos, sin = jnp.cos(ang), jnp.sin(ang)
    reps = LANES // MLA_ROPE
    cos_t = jnp.tile(jnp.concatenate([cos, cos], axis=1), (1, reps))
    sin_t = jnp.tile(jnp.concatenate([-sin, sin], axis=1), (1, reps))
    return cos_t, sin_t


def _prep_w_in(w_in):
    d = w_in.shape[0]
    w = w_in.astype(BF16)
    z = lambda n: jnp.zeros((d, n), BF16)
    return jnp.concatenate(
        [w[:, :MLA_IN], z(MLA_IN_PAD - MLA_IN), w[:, MLA_IN:], z(LANES)], axis=1)


def _prep_w_uq(w):
    r = w.shape[0]
    w = w.reshape(r, MLA_HEADS, MLA_NOPE + MLA_ROPE)
    w = jnp.pad(w, ((0, 0), (0, 0), (0, MLA_QK_PAD - MLA_NOPE - MLA_ROPE)))
    return w.reshape(r, MLA_HEADS * MLA_QK_PAD).astype(BF16)


def _prep_w_ukv(w):
    r = w.shape[0]
    w = w.reshape(r, MLA_HEADS, 2, HEAD_DIM).transpose(0, 2, 1, 3)
    return w.reshape(r, 2 * MLA_HEADS * HEAD_DIM).astype(BF16)


def kernel(x, ffn1_norm, ffn1_w_gu, ffn1_w_down, mix_norm, w_in, mla_q_norm, w_mla_uq, mla_kv_norm, w_mla_ukv, head_out_norm, w_out, ffn2_norm, ffn2_w_gu, ffn2_w_down, final_norm_g):
    batch, seq, d = x.shape
    depth = ffn1_norm.shape[0]
    cos_t, sin_t = _rope_tables(seq)
    slopes = 2.0 ** (-ALIBI_MAX_EXP * jnp.arange(1, DIL_HEADS + 1, dtype=F32) / DIL_HEADS)
    slopes = jnp.broadcast_to(slopes[:, None, None], (DIL_HEADS, 1, LANES))
    w_mla = MLA_HEADS * HEAD_DIM

    xt = x.reshape(batch * seq, d)
    for l in range(depth):
        xt = ffn_block(xt, ffn1_norm[l], ffn1_w_gu[l].astype(BF16),
                       (0.5 * ffn1_w_down[l]).astype(BF16))

        proj = norm_proj(xt, mix_norm[l], _prep_w_in(w_in[l]))
        q, k, v = mla_prep(proj, mla_q_norm[l], mla_kv_norm[l], _prep_w_uq(w_mla_uq[l]),
                           _prep_w_ukv(w_mla_ukv[l]), cos_t, sin_t, seq=seq)
        o_mla = mla_attention(q, k, v, head_out_norm[l], batch=batch, seq=seq)
        o_sb = sb_attention(proj, head_out_norm[l], batch=batch, seq=seq)
        o_dil = dil_attention(proj, slopes, head_out_norm[l], batch=batch, seq=seq)
        wo = w_out[l].astype(BF16)
        xt = out_proj(xt, o_mla, o_sb, o_dil, wo[:w_mla], wo[w_mla:w_mla + SB_W], wo[w_mla + SB_W:])

        xt = ffn_block(xt, ffn2_norm[l], ffn2_w_gu[l].astype(BF16),
                       (0.5 * ffn2_w_down[l]).astype(BF16),
                       final_norm_g if l == depth - 1 else None)
    return xt.reshape(batch, seq, d)
```

```python
import functools

import jax
import jax.numpy as jnp
from jax import lax
from jax.experimental import pallas as pl
from jax.experimental.pallas import tpu as pltpu

F32 = jnp.float32
BF16 = jnp.bfloat16

EPS = 1e-6
HEAD_DIM = 128
LANES = 128
MLA_HEADS = 6
MLA_Q_LORA = 512
MLA_KV_LORA = 256
MLA_NOPE = 128
MLA_ROPE = 64
MLA_QK_PAD = 256
ROPE_THETA = 10000.0
SB_HEADS = 4
DIL_HEADS = 6
DIL_PATTERNS = ((128, 1), (512, 4), (2048, 16))
DIL_BACK = 128
DIL_GROUP = 8
ALIBI_MAX_EXP = 8.0
NEG_BIG = -1e30
LOG2E = 1.4426950408889634
SB_KEYS = 256
FFN_TM = 1024
FFN_ROWS = 512
FFN_TF = 512
SB_SUBS = 1
SB_DEAD_BITS = 160.0

MLA_IN = MLA_Q_LORA + MLA_KV_LORA + MLA_ROPE
MLA_IN_PAD = 896
SB_W = SB_HEADS * HEAD_DIM
DIL_W = DIL_HEADS * HEAD_DIM
PROJ_HALF = MLA_IN_PAD + 3 * SB_W
assert PROJ_HALF == 3 * DIL_W + LANES
SB_COL0 = MLA_IN_PAD // LANES
DIL_COL0 = PROJ_HALF // LANES

VMEM_LIMIT = 56 * 1024 * 1024
VMEM_LIMIT_FFN = 62 * 1024 * 1024


def _cparams(grid_rank, vmem_limit=VMEM_LIMIT):
    return pltpu.CompilerParams(dimension_semantics=("arbitrary",) * grid_rank,
                                vmem_limit_bytes=vmem_limit)


def _rms(x, g):
    return x * lax.rsqrt(jnp.mean(x * x, axis=-1, keepdims=True) + EPS) * g


def _dot(a, b):
    return jnp.dot(a, b, preferred_element_type=F32)


def _dot_nt(a, b):
    return lax.dot_general(a, b, (((1,), (1,)), ((), ())), preferred_element_type=F32)


def _ffn_body(x_ref, g_ref, wg_ref, wu_ref, wd_ref, *rest, final_norm):
    fg_ref, o_ref, h_scr = rest if final_norm else (None,) + rest

    def hidden_tile(first):
        for r0 in range(0, h_scr.shape[0], FFN_ROWS):
            rows = slice(r0, r0 + FFN_ROWS)
            if first:
                h = _rms(x_ref[rows, :], g_ref[...]).astype(BF16)
                h_scr[rows, :] = h
            else:
                h = h_scr[rows, :]
            g = _dot(h, wg_ref[...])
            u = _dot(h, wu_ref[...])
            a = (g * (1.0 / (1.0 + jnp.exp(-g))) * u).astype(BF16)
            base = x_ref[rows, :] if first else o_ref[rows, :]
            o_ref[rows, :] = base + _dot(a, wd_ref[...])

    @pl.when(pl.program_id(1) == 0)
    def _():
        hidden_tile(True)

    @pl.when(pl.program_id(1) > 0)
    def _():
        hidden_tile(False)

    if final_norm:
        @pl.when(pl.program_id(1) == pl.num_programs(1) - 1)
        def _():
            o_ref[...] = _rms(o_ref[...], fg_ref[...])


def ffn_block(x, gain, w_gu, w_down_half, layer, final_gain=None, *, tm=FFN_TM, tf=FFN_TF):
    t, d = x.shape
    f = w_down_half.shape[1]
    nf = f // tf
    final_norm = final_gain is not None
    row = pl.BlockSpec((1, d), lambda i, j: (0, 0))
    return pl.pallas_call(
        functools.partial(_ffn_body, final_norm=final_norm),
        grid=(t // tm, nf),
        in_specs=[
            pl.BlockSpec((tm, d), lambda i, j: (i, 0)),
            row,
            pl.BlockSpec((None, d, tf), lambda i, j: (layer, 0, j)),
            pl.BlockSpec((None, d, tf), lambda i, j: (layer, 0, j + nf)),
            pl.BlockSpec((None, tf, d), lambda i, j: (layer, j, 0)),
        ] + ([row] if final_norm else []),
        out_specs=pl.BlockSpec((tm, d), lambda i, j: (i, 0)),
        out_shape=jax.ShapeDtypeStruct((t, d), F32),
        scratch_shapes=[pltpu.VMEM((tm, d), BF16)],
        compiler_params=_cparams(2, VMEM_LIMIT_FFN),
        name="ffn_block",
    )(x, gain.reshape(1, d), w_gu, w_gu, w_down_half,
      *([final_gain.reshape(1, d)] if final_norm else []))


def _normproj_body(x_ref, g_ref, w_ref, o_ref):
    for r0 in range(0, x_ref.shape[0], FFN_ROWS):
        rows = slice(r0, r0 + FFN_ROWS)
        h = _rms(x_ref[rows, :], g_ref[...]).astype(BF16)
        o_ref[rows, :] = _dot(h, w_ref[...])


def norm_proj(x, gain, w, layer, *, tm=1024):
    t, d = x.shape
    n = w.shape[2]
    tn = PROJ_HALF
    return pl.pallas_call(
        _normproj_body,
        grid=(n // tn, t // tm),
        in_specs=[
            pl.BlockSpec((tm, d), lambda c, i: (i, 0)),
            pl.BlockSpec((1, d), lambda c, i: (0, 0)),
            pl.BlockSpec((None, d, tn), lambda c, i: (layer, 0, c),
                         pipeline_mode=pl.Buffered(1)),
        ],
        out_specs=pl.BlockSpec((tm, tn), lambda c, i: (i, c)),
        out_shape=jax.ShapeDtypeStruct((t, n), F32),
        compiler_params=_cparams(2),
        name="norm_proj",
    )(x, gain.reshape(1, d), w)


def _mla_prep_body(a_ref, gq_ref, gkv_ref, wuq_ref, wukv_ref, cos_ref, sin_ref,
                   q_ref, k_ref, v_ref, *, scale):
    a = a_ref[...]
    qn = _rms(a[:, :MLA_Q_LORA], gq_ref[...]).astype(BF16)
    q = _dot(qn, wuq_ref[...]) * scale
    kvn = _rms(a[:, MLA_Q_LORA:MLA_Q_LORA + MLA_KV_LORA], gkv_ref[...]).astype(BF16)
    kv = _dot(kvn, wukv_ref[...])

    cos_t = cos_ref[...]
    sin_t = sin_ref[...]
    lane = lax.broadcasted_iota(jnp.int32, cos_t.shape, 1)
    first_half = (lane & (MLA_ROPE // 2)) == 0

    def rope(x):
        swapped = jnp.where(first_half,
                            pltpu.roll(x, LANES - MLA_ROPE // 2, 1),
                            pltpu.roll(x, MLA_ROPE // 2, 1))
        return x * cos_t + swapped * sin_t

    k_rope = rope(a[:, MLA_Q_LORA + MLA_KV_LORA:]).astype(BF16)
    nk = MLA_HEADS * MLA_NOPE
    for h in range(MLA_HEADS):
        c0 = h * MLA_QK_PAD
        q_ref[:, c0:c0 + LANES] = q[:, c0:c0 + LANES].astype(BF16)
        q_ref[:, c0 + LANES:c0 + 2 * LANES] = rope(q[:, c0 + LANES:c0 + 2 * LANES]).astype(BF16)
        k_ref[:, c0:c0 + LANES] = kv[:, h * MLA_NOPE:(h + 1) * MLA_NOPE].astype(BF16)
        k_ref[:, c0 + LANES:c0 + 2 * LANES] = k_rope
    v_ref[...] = kv[:, nk:].astype(BF16)


def mla_prep(proj, gq, gkv, wuq, wukv, cos_t, sin_t, *, seq, tm=1024):
    t = proj.shape[0]
    nseq = seq // tm
    wq = MLA_HEADS * MLA_QK_PAD
    wv = MLA_HEADS * HEAD_DIM
    scale = float((MLA_NOPE + MLA_ROPE) ** -0.5) * LOG2E
    return pl.pallas_call(
        functools.partial(_mla_prep_body, scale=scale),
        grid=(t // tm,),
        in_specs=[
            pl.BlockSpec((tm, MLA_IN_PAD), lambda i: (i, 0)),
            pl.BlockSpec((1, MLA_Q_LORA), lambda i: (0, 0)),
            pl.BlockSpec((1, MLA_KV_LORA), lambda i: (0, 0)),
            pl.BlockSpec((MLA_Q_LORA, wq), lambda i: (0, 0)),
            pl.BlockSpec((MLA_KV_LORA, 2 * wv), lambda i: (0, 0)),
            pl.BlockSpec((tm, LANES), lambda i: (i % nseq, 0)),
            pl.BlockSpec((tm, LANES), lambda i: (i % nseq, 0)),
        ],
        out_specs=[
            pl.BlockSpec((tm, wq), lambda i: (i, 0)),
            pl.BlockSpec((tm, wq), lambda i: (i, 0)),
            pl.BlockSpec((tm, wv), lambda i: (i, 0)),
        ],
        out_shape=[
            jax.ShapeDtypeStruct((t, wq), BF16),
            jax.ShapeDtypeStruct((t, wq), BF16),
            jax.ShapeDtypeStruct((t, wv), BF16),
        ],
        compiler_params=_cparams(1),
        name="mla_prep",
    )(proj, gq.reshape(1, -1), gkv.reshape(1, -1), wuq, wukv, cos_t, sin_t)


def _mla_attn_body(q_ref, k_ref, v_ref, gn_ref, o_ref, acc_scr, *, seq, nc, rc):
    tq = nc * rc
    row = lax.broadcasted_iota(jnp.int32, (rc, rc), 0)
    col = lax.broadcasted_iota(jnp.int32, (rc, rc), 1)
    causal = col <= row

    def run(q0, jobs, states):
        scores = [_dot_nt(q_ref[pl.ds(q0 + c * rc, rc), :], k_ref[pl.ds(k0, rc), :])
                  for c, k0, _ in jobs]
        states = list(states)
        for (c, k0, diag), s in zip(jobs, scores):
            if diag:
                s = jnp.where(causal, s, NEG_BIG)
            m, l = states[c]
            m_new = jnp.maximum(m, jnp.max(s, axis=1, keepdims=True))
            alpha = jnp.exp2(m - m_new)
            p = jnp.exp2(s - m_new)
            l = alpha * l + jnp.sum(p, axis=1, keepdims=True)
            acc_scr[c] = alpha * acc_scr[c] + _dot(p.astype(BF16), v_ref[pl.ds(k0, rc), :])
            states[c] = (m_new, l)
        return tuple(states)

    def q_loop(qi, _):
        q0 = pl.multiple_of(qi * tq, tq)
        for c in range(nc):
            acc_scr[c] = jnp.zeros((rc, HEAD_DIM), F32)
        init = tuple((jnp.full((rc, 1), NEG_BIG, F32), jnp.zeros((rc, 1), F32))
                     for _ in range(nc))

        def kv_loop(i, states):
            k0 = pl.multiple_of(i * tq, tq)
            return run(q0, [(c, k0 + t * rc, False) for t in range(nc) for c in range(nc)],
                       states)

        states = lax.fori_loop(0, qi, kv_loop, init)
        states = run(q0, [(c, q0 + t * rc, t == c) for t in range(nc) for c in range(t, nc)],
                     states)
        for c, (_, l) in enumerate(states):
            o_ref[pl.ds(q0 + c * rc, rc), :] = _rms(acc_scr[c] / l, gn_ref[...]).astype(BF16)
        return 0

    lax.fori_loop(0, seq // tq, q_loop, 0)


def mla_attention(q, k, v, head_gain, *, batch, seq, nc=2, rc=512):
    t = q.shape[0]
    return pl.pallas_call(
        functools.partial(_mla_attn_body, seq=seq, nc=nc, rc=rc),
        grid=(batch, MLA_HEADS),
        in_specs=[
            pl.BlockSpec((seq, MLA_QK_PAD), lambda b, h: (b, h)),
            pl.BlockSpec((seq, MLA_QK_PAD), lambda b, h: (b, h)),
            pl.BlockSpec((seq, HEAD_DIM), lambda b, h: (b, h)),
            pl.BlockSpec((1, HEAD_DIM), lambda b, h: (0, h)),
        ],
        out_specs=pl.BlockSpec((seq, HEAD_DIM), lambda b, h: (b, h)),
        out_shape=jax.ShapeDtypeStruct((t, MLA_HEADS * HEAD_DIM), BF16),
        scratch_shapes=[pltpu.VMEM((nc, rc, HEAD_DIM), F32)],
        compiler_params=_cparams(2),
        name="mla_attention",
    )(q, k, v, head_gain.reshape(1, -1))


def _sb_attn_body(q_ref, k_ref, v_ref, gn_ref, o_ref, qb_scr, kb_scr, vb_scr, acc_scr,
                  *, seq, rc, scale):
    tk = SB_KEYS
    own = rc // tk
    qb_scr[...] = (q_ref[...] * scale).astype(BF16)
    kb_scr[...] = k_ref[...].astype(BF16)
    vb_scr[...] = v_ref[...].astype(BF16)
    urow = lax.broadcasted_iota(jnp.int32, (tk, tk), 0)
    ucol = lax.broadcasted_iota(jnp.int32, (tk, tk), 1)
    later_keys = (urow > ucol).astype(BF16)

    def absorb(q0, r0, ks, rest, diag):
        n = rc - r0
        k0 = pl.multiple_of(ks * tk, tk)
        z = _dot_nt(qb_scr[pl.ds(q0 + r0, n), :], kb_scr[pl.ds(k0, tk), :])
        hi = jnp.maximum(z, 0.0)
        lo = jnp.minimum(z, 0.0)
        t = jnp.log(1.0 + jnp.exp2(lo - hi)) * LOG2E
        sp = hi + t
        ls = lo - t
        if diag:
            strict = (lax.broadcasted_iota(jnp.int32, (n, tk), 1)
                      < lax.broadcasted_iota(jnp.int32, (n, tk), 0))
            sp = jnp.where(strict, sp, 0.0)
        later = _dot(sp.astype(BF16), later_keys)
        a = jnp.exp2(ls - later - rest)
        if diag:
            a = jnp.where(strict, a, 0.0)
        acc_scr[r0:, :] += _dot(a.astype(BF16), vb_scr[pl.ds(k0, tk), :])
        return rest + later[:, 0:1] + sp[:, 0:1]

    def chunk(ci, _):
        q0 = pl.multiple_of(ci * rc, rc)
        acc_scr[...] = jnp.zeros((rc, HEAD_DIM), F32)
        rest = jnp.zeros((tk, 1), F32)
        for j in range(own - 1, -1, -1):
            rest = absorb(q0, j * tk, ci * own + j, rest, True)
            if j > 0:
                rest = jnp.concatenate([jnp.zeros((tk, 1), F32), rest], axis=0)

        def alive(state):
            ks, rest = state
            return jnp.logical_and(ks >= 0, jnp.min(rest) < SB_DEAD_BITS)

        def walk(state):
            ks, rest = state
            for t in range(SB_SUBS):
                rest = absorb(q0, 0, ks - t, rest, False)
            return ks - SB_SUBS, rest

        lax.while_loop(alive, walk, (ci * own - 1, rest))
        o_ref[pl.ds(q0, rc), :] = _rms(acc_scr[...], gn_ref[...]).astype(BF16)
        return 0

    lax.fori_loop(0, seq // rc, chunk, 0)


def sb_attention(proj, head_gain, *, batch, seq, rc=512):
    assert (rc // SB_KEYS) % SB_SUBS == 0
    t = proj.shape[0]
    scale = float(HEAD_DIM ** -0.5) * LOG2E
    gain_col0 = MLA_HEADS
    return pl.pallas_call(
        functools.partial(_sb_attn_body, seq=seq, rc=rc, scale=scale),
        grid=(batch, SB_HEADS),
        in_specs=[
            pl.BlockSpec((seq, HEAD_DIM), lambda b, h: (b, SB_COL0 + h)),
            pl.BlockSpec((seq, HEAD_DIM), lambda b, h: (b, SB_COL0 + SB_HEADS + h)),
            pl.BlockSpec((seq, HEAD_DIM), lambda b, h: (b, SB_COL0 + 2 * SB_HEADS + h)),
            pl.BlockSpec((1, HEAD_DIM), lambda b, h: (0, gain_col0 + h)),
        ],
        out_specs=pl.BlockSpec((seq, HEAD_DIM), lambda b, h: (b, h)),
        out_shape=jax.ShapeDtypeStruct((t, SB_W), BF16),
        scratch_shapes=[pltpu.VMEM((seq, HEAD_DIM), BF16),
                        pltpu.VMEM((seq, HEAD_DIM), BF16),
                        pltpu.VMEM((seq, HEAD_DIM), BF16),
                        pltpu.VMEM((rc, HEAD_DIM), F32)],
        compiler_params=_cparams(2),
        name="sb_attention",
    )(proj, proj, proj, head_gain.reshape(1, -1))


def _dil_attn_body(q_ref, k_ref, v_ref, slope_ref, gn_ref, o_ref, *scr, seq, scale):
    blk = DIL_BACK
    npat = len(DIL_PATTERNS)
    lse_scr, out_scr = scr[:npat], scr[npat:]
    row = lax.broadcasted_iota(jnp.int32, (blk, blk), 0)
    col = lax.broadcasted_iota(jnp.int32, (blk, blk), 1)
    dist_cur = (row - col).astype(F32)
    dist_prev = dist_cur + float(blk)
    ok_cur = row >= col
    ok_prev_band = row <= col
    slope = slope_ref[...]

    for p, (window, d) in enumerate(DIL_PATTERNS):
        assert window // d == blk
        nblk = seq // d // blk
        assert nblk & (nblk - 1) == 0
        bias_cur = -(slope * (float(d) * LOG2E)) * dist_cur
        bias_prev = -(slope * (float(d) * LOG2E)) * dist_prev

        def group(gi, _, p=p, d=d, nblk=nblk, bias_cur=bias_cur, bias_prev=bias_prev):
            def rows(idx, back):
                r = lax.shift_right_logical(idx, nblk.bit_length() - 1)
                nb = lax.bitwise_and(idx, nblk - 1)
                b = jnp.maximum(nb - back, 0)
                start = r + b * (d * blk)
                if d == 1:
                    return pl.ds(pl.multiple_of(start, blk), blk)
                return pl.ds(start, blk, stride=d)

            idxs = [gi * DIL_GROUP + g for g in range(DIL_GROUP)]
            k_own = [k_ref[rows(idx, 0), :].astype(BF16) for idx in idxs]
            k_before = [k_ref[rows(idxs[0], 1), :].astype(BF16)] + k_own[:-1]
            scores = []
            for idx, kc, kp in zip(idxs, k_own, k_before):
                q = (q_ref[rows(idx, 0), :] * scale).astype(BF16)
                scores.append((_dot_nt(q, kc), _dot_nt(q, kp)))
            probs = []
            for idx, (s_cur, s_prev) in zip(idxs, scores):
                nb = lax.bitwise_and(idx, nblk - 1)
                s_cur = jnp.where(ok_cur, s_cur + bias_cur, NEG_BIG)
                no_prev = jnp.where(nb > 0, 0.0, NEG_BIG)
                s_prev = jnp.where(ok_prev_band, s_prev + bias_prev + no_prev, NEG_BIG)
                m_blk = jnp.max(jnp.maximum(s_cur, s_prev), axis=-1, keepdims=True)
                p_cur = jnp.exp2(s_cur - m_blk)
                p_prev = jnp.exp2(s_prev - m_blk)
                l_blk = jnp.broadcast_to(jnp.sum(p_cur + p_prev, axis=-1, keepdims=True),
                                         (blk, HEAD_DIM))
                lse_scr[p][rows(idx, 0), :] = m_blk + jnp.log(l_blk) * LOG2E
                probs.append((p_cur.astype(BF16), p_prev.astype(BF16), l_blk))
            v_own = [v_ref[rows(idx, 0), :].astype(BF16) for idx in idxs]
            v_before = [v_ref[rows(idxs[0], 1), :].astype(BF16)] + v_own[:-1]
            for idx, (p_cur, p_prev, l_blk), vc, vp in zip(idxs, probs, v_own, v_before):
                out_scr[p][rows(idx, 0), :] = (_dot(p_cur, vc) + _dot(p_prev, vp)) / l_blk
            return 0

        lax.fori_loop(0, d * nblk // DIL_GROUP, group, 0)

    def finish(i, _):
        rs = pl.ds(pl.multiple_of(i * blk, blk), blk)
        lses = [lse_scr[p][rs, :] for p in range(npat)]
        top = functools.reduce(jnp.maximum, lses)
        den = jnp.zeros((blk, HEAD_DIM), F32)
        num = jnp.zeros((blk, HEAD_DIM), F32)
        for p in range(npat):
            w = jnp.exp2(lses[p] - top)
            den = den + w
            num = num + out_scr[p][rs, :] * w
        o_ref[rs, :] = _rms(num / den, gn_ref[...]).astype(BF16)
        return 0

    lax.fori_loop(0, seq // blk, finish, 0, unroll=2)


def dil_attention(proj, slopes, head_gain, *, batch, seq):
    t = proj.shape[0]
    scale = float(HEAD_DIM ** -0.5) * LOG2E
    gain_col0 = MLA_HEADS + SB_HEADS
    return pl.pallas_call(
        functools.partial(_dil_attn_body, seq=seq, scale=scale),
        grid=(batch, DIL_HEADS),
        in_specs=[
            pl.BlockSpec((seq, HEAD_DIM), lambda b, h: (b, DIL_COL0 + h)),
            pl.BlockSpec((seq, HEAD_DIM), lambda b, h: (b, DIL_COL0 + DIL_HEADS + h)),
            pl.BlockSpec((seq, HEAD_DIM), lambda b, h: (b, DIL_COL0 + 2 * DIL_HEADS + h)),
            pl.BlockSpec((None, 1, LANES), lambda b, h: (h, 0, 0)),
            pl.BlockSpec((1, HEAD_DIM), lambda b, h: (0, gain_col0 + h)),
        ],
        out_specs=pl.BlockSpec((seq, HEAD_DIM), lambda b, h: (b, h)),
        out_shape=jax.ShapeDtypeStruct((t, DIL_W), BF16),
        scratch_shapes=[pltpu.VMEM((seq, HEAD_DIM), F32)] * (2 * len(DIL_PATTERNS)),
        compiler_params=_cparams(2),
        name="dil_attention",
    )(proj, proj, proj, slopes, head_gain.reshape(1, -1))


def _outproj_body(x_ref, o1_ref, o2_ref, o3_ref, w1_ref, w2_ref, w3_ref, out_ref):
    acc = _dot(o1_ref[...], w1_ref[...])
    acc = acc + _dot(o2_ref[...], w2_ref[...])
    acc = acc + _dot(o3_ref[...], w3_ref[...])
    out_ref[...] = x_ref[...] + acc


def out_proj(x, o_mla, o_sb, o_dil, w1, w2, w3, layer, *, tm=512):
    t, d = x.shape
    full = lambda a: pl.BlockSpec((None,) + a.shape[1:], lambda i: (layer, 0, 0),
                                  pipeline_mode=pl.Buffered(1))
    rows = lambda a: pl.BlockSpec((tm, a.shape[1]), lambda i: (i, 0))
    return pl.pallas_call(
        _outproj_body,
        grid=(t // tm,),
        in_specs=[rows(x), rows(o_mla), rows(o_sb), rows(o_dil), full(w1), full(w2), full(w3)],
        out_specs=rows(x),
        out_shape=jax.ShapeDtypeStruct((t, d), F32),
        compiler_params=_cparams(1),
        name="out_proj",
    )(x, o_mla, o_sb, o_dil, w1, w2, w3)


def _rope_tables(seq):
    inv = ROPE_THETA ** (-jnp.arange(0, MLA_ROPE, 2, dtype=F32) / MLA_ROPE)
    ang = jnp.arange(seq, dtype=F32)[:, None] * inv[None, :]
    cos, sin = jnp.cos(ang), jnp.sin(ang)
    reps = LANES // MLA_ROPE
    cos_t = jnp.tile(jnp.concatenate([cos, cos], axis=1), (1, reps))
    sin_t = jnp.tile(jnp.concatenate([-sin, sin], axis=1), (1, reps))
    return cos_t, sin_t


def _prep_w_in(w_in):
    w = w_in.astype(BF16)
    z = lambda n: jnp.zeros(w.shape[:-1] + (n,), BF16)
    return jnp.concatenate(
        [w[..., :MLA_IN], z(MLA_IN_PAD - MLA_IN), w[..., MLA_IN:], z(LANES)], axis=-1)


def _prep_w_uq(w):
    r = w.shape[0]
    w = w.reshape(r, MLA_HEADS, MLA_NOPE + MLA_ROPE)
    w = jnp.pad(w, ((0, 0), (0, 0), (0, MLA_QK_PAD - MLA_NOPE - MLA_ROPE)))
    return w.reshape(r, MLA_HEADS * MLA_QK_PAD).astype(BF16)


def _prep_w_ukv(w):
    r = w.shape[0]
    w = w.reshape(r, MLA_HEADS, 2, HEAD_DIM).transpose(0, 2, 1, 3)
    return w.reshape(r, 2 * MLA_HEADS * HEAD_DIM).astype(BF16)


def kernel(x, ffn1_norm, ffn1_w_gu, ffn1_w_down, mix_norm, w_in, mla_q_norm, w_mla_uq, mla_kv_norm, w_mla_ukv, head_out_norm, w_out, ffn2_norm, ffn2_w_gu, ffn2_w_down, final_norm_g):
    batch, seq, d = x.shape
    depth = ffn1_norm.shape[0]
    cos_t, sin_t = _rope_tables(seq)
    slopes = 2.0 ** (-ALIBI_MAX_EXP * jnp.arange(1, DIL_HEADS + 1, dtype=F32) / DIL_HEADS)
    slopes = jnp.broadcast_to(slopes[:, None, None], (DIL_HEADS, 1, LANES))
    w_mla = MLA_HEADS * HEAD_DIM

    w_gu1, w_gu2 = ffn1_w_gu.astype(BF16), ffn2_w_gu.astype(BF16)
    w_dn1, w_dn2 = (0.5 * ffn1_w_down).astype(BF16), (0.5 * ffn2_w_down).astype(BF16)
    w_in_all = _prep_w_in(w_in)
    wo = w_out.astype(BF16)
    wo_mla, wo_sb, wo_dil = wo[:, :w_mla], wo[:, w_mla:w_mla + SB_W], wo[:, w_mla + SB_W:]

    xt = x.reshape(batch * seq, d)
    for l in range(depth):
        xt = ffn_block(xt, ffn1_norm[l], w_gu1, w_dn1, l)

        proj = norm_proj(xt, mix_norm[l], w_in_all, l)
        q, k, v = mla_prep(proj, mla_q_norm[l], mla_kv_norm[l], _prep_w_uq(w_mla_uq[l]),
                           _prep_w_ukv(w_mla_ukv[l]), cos_t, sin_t, seq=seq)
        o_mla = mla_attention(q, k, v, head_out_norm[l], batch=batch, seq=seq)
        o_sb = sb_attention(proj, head_out_norm[l], batch=batch, seq=seq)
        o_dil = dil_attention(proj, slopes, head_out_norm[l], batch=batch, seq=seq)
        xt = out_proj(xt, o_mla, o_sb, o_dil, wo_mla, wo_sb, wo_dil, l)

        xt = ffn_block(xt, ffn2_norm[l], w_gu2, w_dn2, l,
                       final_norm_g if l == depth - 1 else None)
    return xt.reshape(batch, seq, d)
```

```python
import functools

import jax
import jax.numpy as jnp
from jax import lax
from jax.experimental import pallas as pl
from jax.experimental.pallas import tpu as pltpu

F32 = jnp.float32
BF16 = jnp.bfloat16

EPS = 1e-6
HEAD_DIM = 128
LANES = 128
MLA_HEADS = 6
MLA_Q_LORA = 512
MLA_KV_LORA = 256
MLA_NOPE = 128
MLA_ROPE = 64
MLA_QK_PAD = 256
ROPE_THETA = 10000.0
SB_HEADS = 4
DIL_HEADS = 6
DIL_PATTERNS = ((128, 1), (512, 4), (2048, 16))
DIL_BACK = 128
DIL_GROUP = 16
ALIBI_MAX_EXP = 8.0
NEG_BIG = -1e30
LOG2E = 1.4426950408889634
SB_KEYS = 256
FFN_TM = 1024
FFN_ROWS = 512
FFN_TF = 512
SB_SUBS = 1
SB_DEAD_BITS = 160.0

MLA_IN = MLA_Q_LORA + MLA_KV_LORA + MLA_ROPE
MLA_IN_PAD = 896
SB_W = SB_HEADS * HEAD_DIM
DIL_W = DIL_HEADS * HEAD_DIM
PROJ_HALF = MLA_IN_PAD + 3 * SB_W
assert PROJ_HALF == 3 * DIL_W + LANES
SB_COL0 = MLA_IN_PAD // LANES
DIL_COL0 = PROJ_HALF // LANES

VMEM_LIMIT = 56 * 1024 * 1024
VMEM_LIMIT_FFN = 62 * 1024 * 1024


def _cparams(grid_rank, vmem_limit=VMEM_LIMIT):
    return pltpu.CompilerParams(dimension_semantics=("arbitrary",) * grid_rank,
                                vmem_limit_bytes=vmem_limit)


def _rms(x, g):
    return x * lax.rsqrt(jnp.mean(x * x, axis=-1, keepdims=True) + EPS) * g


def _dot(a, b):
    return jnp.dot(a, b, preferred_element_type=F32)


def _dot_nt(a, b):
    return lax.dot_general(a, b, (((1,), (1,)), ((), ())), preferred_element_type=F32)


def _ffn_body(x_ref, g_ref, wg_ref, wu_ref, wd_ref, *rest, final_norm):
    fg_ref, o_ref, h_scr = rest if final_norm else (None,) + rest

    def hidden_tile(first):
        for r0 in range(0, h_scr.shape[0], FFN_ROWS):
            rows = slice(r0, r0 + FFN_ROWS)
            if first:
                h = _rms(x_ref[rows, :], g_ref[...]).astype(BF16)
                h_scr[rows, :] = h
            else:
                h = h_scr[rows, :]
            g = _dot(h, wg_ref[...])
            u = _dot(h, wu_ref[...])
            a = (g * (1.0 / (1.0 + jnp.exp(-g))) * u).astype(BF16)
            base = x_ref[rows, :] if first else o_ref[rows, :]
            o_ref[rows, :] = base + _dot(a, wd_ref[...])

    @pl.when(pl.program_id(1) == 0)
    def _():
        hidden_tile(True)

    @pl.when(pl.program_id(1) > 0)
    def _():
        hidden_tile(False)

    if final_norm:
        @pl.when(pl.program_id(1) == pl.num_programs(1) - 1)
        def _():
            o_ref[...] = _rms(o_ref[...], fg_ref[...])


def ffn_block(x, gain, w_gu, w_down_half, layer, final_gain=None, *, tm=FFN_TM, tf=FFN_TF):
    t, d = x.shape
    f = w_down_half.shape[1]
    nf = f // tf
    final_norm = final_gain is not None
    row = pl.BlockSpec((1, d), lambda i, j: (0, 0))
    return pl.pallas_call(
        functools.partial(_ffn_body, final_norm=final_norm),
        grid=(t // tm, nf),
        in_specs=[
            pl.BlockSpec((tm, d), lambda i, j: (i, 0)),
            row,
            pl.BlockSpec((None, d, tf), lambda i, j: (layer, 0, j)),
            pl.BlockSpec((None, d, tf), lambda i, j: (layer, 0, j + nf)),
            pl.BlockSpec((None, tf, d), lambda i, j: (layer, j, 0)),
        ] + ([row] if final_norm else []),
        out_specs=pl.BlockSpec((tm, d), lambda i, j: (i, 0)),
        out_shape=jax.ShapeDtypeStruct((t, d), F32),
        scratch_shapes=[pltpu.VMEM((tm, d), BF16)],
        compiler_params=_cparams(2, VMEM_LIMIT_FFN),
        name="ffn_block",
    )(x, gain.reshape(1, d), w_gu, w_gu, w_down_half,
      *([final_gain.reshape(1, d)] if final_norm else []))


def _normproj_body(x_ref, g_ref, w_ref, o_ref):
    for r0 in range(0, x_ref.shape[0], FFN_ROWS):
        rows = slice(r0, r0 + FFN_ROWS)
        h = _rms(x_ref[rows, :], g_ref[...]).astype(BF16)
        o_ref[rows, :] = _dot(h, w_ref[...])


def norm_proj(x, gain, w, layer, *, tm=1024):
    t, d = x.shape
    n = w.shape[2]
    tn = PROJ_HALF
    return pl.pallas_call(
        _normproj_body,
        grid=(n // tn, t // tm),
        in_specs=[
            pl.BlockSpec((tm, d), lambda c, i: (i, 0)),
            pl.BlockSpec((1, d), lambda c, i: (0, 0)),
            pl.BlockSpec((None, d, tn), lambda c, i: (layer, 0, c),
                         pipeline_mode=pl.Buffered(1)),
        ],
        out_specs=pl.BlockSpec((tm, tn), lambda c, i: (i, c)),
        out_shape=jax.ShapeDtypeStruct((t, n), F32),
        compiler_params=_cparams(2),
        name="norm_proj",
    )(x, gain.reshape(1, d), w)


def _mla_prep_body(a_ref, gq_ref, gkv_ref, wuq_ref, wukv_ref, cos_ref, sin_ref,
                   q_ref, k_ref, v_ref, *, scale):
    a = a_ref[...]
    qn = _rms(a[:, :MLA_Q_LORA], gq_ref[...]).astype(BF16)
    q = _dot(qn, wuq_ref[...]) * scale
    kvn = _rms(a[:, MLA_Q_LORA:MLA_Q_LORA + MLA_KV_LORA], gkv_ref[...]).astype(BF16)
    kv = _dot(kvn, wukv_ref[...])

    cos_t = cos_ref[...]
    sin_t = sin_ref[...]
    lane = lax.broadcasted_iota(jnp.int32, cos_t.shape, 1)
    first_half = (lane & (MLA_ROPE // 2)) == 0

    def rope(x):
        swapped = jnp.where(first_half,
                            pltpu.roll(x, LANES - MLA_ROPE // 2, 1),
                            pltpu.roll(x, MLA_ROPE // 2, 1))
        return x * cos_t + swapped * sin_t

    k_rope = rope(a[:, MLA_Q_LORA + MLA_KV_LORA:]).astype(BF16)
    nk = MLA_HEADS * MLA_NOPE
    for h in range(MLA_HEADS):
        c0 = h * MLA_QK_PAD
        q_ref[:, c0:c0 + LANES] = q[:, c0:c0 + LANES].astype(BF16)
        q_ref[:, c0 + LANES:c0 + 2 * LANES] = rope(q[:, c0 + LANES:c0 + 2 * LANES]).astype(BF16)
        k_ref[:, c0:c0 + LANES] = kv[:, h * MLA_NOPE:(h + 1) * MLA_NOPE].astype(BF16)
        k_ref[:, c0 + LANES:c0 + 2 * LANES] = k_rope
    v_ref[...] = kv[:, nk:].astype(BF16)


def mla_prep(proj, gq, gkv, wuq, wukv, cos_t, sin_t, *, seq, tm=1024):
    t = proj.shape[0]
    nseq = seq // tm
    wq = MLA_HEADS * MLA_QK_PAD
    wv = MLA_HEADS * HEAD_DIM
    scale = float((MLA_NOPE + MLA_ROPE) ** -0.5) * LOG2E
    return pl.pallas_call(
        functools.partial(_mla_prep_body, scale=scale),
        grid=(t // tm,),
        in_specs=[
            pl.BlockSpec((tm, MLA_IN_PAD), lambda i: (i, 0)),
            pl.BlockSpec((1, MLA_Q_LORA), lambda i: (0, 0)),
            pl.BlockSpec((1, MLA_KV_LORA), lambda i: (0, 0)),
            pl.BlockSpec((MLA_Q_LORA, wq), lambda i: (0, 0)),
            pl.BlockSpec((MLA_KV_LORA, 2 * wv), lambda i: (0, 0)),
            pl.BlockSpec((tm, LANES), lambda i: (i % nseq, 0)),
            pl.BlockSpec((tm, LANES), lambda i: (i % nseq, 0)),
        ],
        out_specs=[
            pl.BlockSpec((tm, wq), lambda i: (i, 0)),
            pl.BlockSpec((tm, wq), lambda i: (i, 0)),
            pl.BlockSpec((tm, wv), lambda i: (i, 0)),
        ],
        out_shape=[
            jax.ShapeDtypeStruct((t, wq), BF16),
            jax.ShapeDtypeStruct((t, wq), BF16),
            jax.ShapeDtypeStruct((t, wv), BF16),
        ],
        compiler_params=_cparams(1),
        name="mla_prep",
    )(proj, gq.reshape(1, -1), gkv.reshape(1, -1), wuq, wukv, cos_t, sin_t)


def _mla_attn_body(q_ref, k_ref, v_ref, gn_ref, o_ref, acc_scr, *, seq, nc, rc):
    tq = nc * rc
    row = lax.broadcasted_iota(jnp.int32, (rc, rc), 0)
    col = lax.broadcasted_iota(jnp.int32, (rc, rc), 1)
    causal = col <= row

    def run(q0, jobs, states):
        scores = [_dot_nt(q_ref[pl.ds(q0 + c * rc, rc), :], k_ref[pl.ds(k0, rc), :])
                  for c, k0, _ in jobs]
        states = list(states)
        for (c, k0, diag), s in zip(jobs, scores):
            if diag:
                s = jnp.where(causal, s, NEG_BIG)
            m, l = states[c]
            m_new = jnp.maximum(m, jnp.max(s, axis=1, keepdims=True))
            alpha = jnp.exp2(m - m_new)
            p = jnp.exp2(s - m_new)
            l = alpha * l + jnp.sum(p, axis=1, keepdims=True)
            acc_scr[c] = alpha * acc_scr[c] + _dot(p.astype(BF16), v_ref[pl.ds(k0, rc), :])
            states[c] = (m_new, l)
        return tuple(states)

    def q_loop(qi, _):
        q0 = pl.multiple_of(qi * tq, tq)
        for c in range(nc):
            acc_scr[c] = jnp.zeros((rc, HEAD_DIM), F32)
        init = tuple((jnp.full((rc, 1), NEG_BIG, F32), jnp.zeros((rc, 1), F32))
                     for _ in range(nc))

        def kv_loop(i, states):
            k0 = pl.multiple_of(i * tq, tq)
            return run(q0, [(c, k0 + t * rc, False) for t in range(nc) for c in range(nc)],
                       states)

        states = lax.fori_loop(0, qi, kv_loop, init)
        states = run(q0, [(c, q0 + t * rc, t == c) for t in range(nc) for c in range(t, nc)],
                     states)
        for c, (_, l) in enumerate(states):
            o_ref[pl.ds(q0 + c * rc, rc), :] = _rms(acc_scr[c] / l, gn_ref[...]).astype(BF16)
        return 0

    lax.fori_loop(0, seq // tq, q_loop, 0)


def mla_attention(q, k, v, head_gain, *, batch, seq, nc=4, rc=512):
    t = q.shape[0]
    return pl.pallas_call(
        functools.partial(_mla_attn_body, seq=seq, nc=nc, rc=rc),
        grid=(batch, MLA_HEADS),
        in_specs=[
            pl.BlockSpec((seq, MLA_QK_PAD), lambda b, h: (b, h)),
            pl.BlockSpec((seq, MLA_QK_PAD), lambda b, h: (b, h)),
            pl.BlockSpec((seq, HEAD_DIM), lambda b, h: (b, h)),
            pl.BlockSpec((1, HEAD_DIM), lambda b, h: (0, h)),
        ],
        out_specs=pl.BlockSpec((seq, HEAD_DIM), lambda b, h: (b, h)),
        out_shape=jax.ShapeDtypeStruct((t, MLA_HEADS * HEAD_DIM), BF16),
        scratch_shapes=[pltpu.VMEM((nc, rc, HEAD_DIM), F32)],
        compiler_params=_cparams(2),
        name="mla_attention",
    )(q, k, v, head_gain.reshape(1, -1))


def _sb_attn_body(q_ref, k_ref, v_ref, gn_ref, o_ref, qb_scr, kb_scr, vb_scr, acc_scr,
                  *, seq, rc, scale):
    tk = SB_KEYS
    own = rc // tk
    qb_scr[...] = (q_ref[...] * scale).astype(BF16)
    kb_scr[...] = k_ref[...].astype(BF16)
    vb_scr[...] = v_ref[...].astype(BF16)
    urow = lax.broadcasted_iota(jnp.int32, (tk, tk), 0)
    ucol = lax.broadcasted_iota(jnp.int32, (tk, tk), 1)
    later_keys = (urow > ucol).astype(BF16)

    def absorb(q0, r0, ks, rest, diag):
        n = rc - r0
        k0 = pl.multiple_of(ks * tk, tk)
        z = _dot_nt(qb_scr[pl.ds(q0 + r0, n), :], kb_scr[pl.ds(k0, tk), :])
        hi = jnp.maximum(z, 0.0)
        lo = jnp.minimum(z, 0.0)
        t = jnp.log(1.0 + jnp.exp2(lo - hi)) * LOG2E
        sp = hi + t
        ls = lo - t
        if diag:
            strict = (lax.broadcasted_iota(jnp.int32, (n, tk), 1)
                      < lax.broadcasted_iota(jnp.int32, (n, tk), 0))
            sp = jnp.where(strict, sp, 0.0)
        later = _dot(sp.astype(BF16), later_keys)
        a = jnp.exp2(ls - later - rest)
        if diag:
            a = jnp.where(strict, a, 0.0)
        acc_scr[r0:, :] += _dot(a.astype(BF16), vb_scr[pl.ds(k0, tk), :])
        return rest + later[:, 0:1] + sp[:, 0:1]

    def chunk(ci, _):
        q0 = pl.multiple_of(ci * rc, rc)
        acc_scr[...] = jnp.zeros((rc, HEAD_DIM), F32)
        rest = jnp.zeros((tk, 1), F32)
        for j in range(own - 1, -1, -1):
            rest = absorb(q0, j * tk, ci * own + j, rest, True)
            if j > 0:
                rest = jnp.concatenate([jnp.zeros((tk, 1), F32), rest], axis=0)

        def alive(state):
            ks, rest = state
            return jnp.logical_and(ks >= 0, jnp.min(rest) < SB_DEAD_BITS)

        def walk(state):
            ks, rest = state
            for t in range(SB_SUBS):
                rest = absorb(q0, 0, ks - t, rest, False)
            return ks - SB_SUBS, rest

        lax.while_loop(alive, walk, (ci * own - 1, rest))
        o_ref[pl.ds(q0, rc), :] = _rms(acc_scr[...], gn_ref[...]).astype(BF16)
        return 0

    lax.fori_loop(0, seq // rc, chunk, 0)


def sb_attention(proj, head_gain, *, batch, seq, rc=512):
    assert (rc // SB_KEYS) % SB_SUBS == 0
    t = proj.shape[0]
    scale = float(HEAD_DIM ** -0.5) * LOG2E
    gain_col0 = MLA_HEADS
    return pl.pallas_call(
        functools.partial(_sb_attn_body, seq=seq, rc=rc, scale=scale),
        grid=(batch, SB_HEADS),
        in_specs=[
            pl.BlockSpec((seq, HEAD_DIM), lambda b, h: (b, SB_COL0 + h)),
            pl.BlockSpec((seq, HEAD_DIM), lambda b, h: (b, SB_COL0 + SB_HEADS + h)),
            pl.BlockSpec((seq, HEAD_DIM), lambda b, h: (b, SB_COL0 + 2 * SB_HEADS + h)),
            pl.BlockSpec((1, HEAD_DIM), lambda b, h: (0, gain_col0 + h)),
        ],
        out_specs=pl.BlockSpec((seq, HEAD_DIM), lambda b, h: (b, h)),
        out_shape=jax.ShapeDtypeStruct((t, SB_W), BF16),
        scratch_shapes=[pltpu.VMEM((seq, HEAD_DIM), BF16),
                        pltpu.VMEM((seq, HEAD_DIM), BF16),
                        pltpu.VMEM((seq, HEAD_DIM), BF16),
                        pltpu.VMEM((rc, HEAD_DIM), F32)],
        compiler_params=_cparams(2),
        name="sb_attention",
    )(proj, proj, proj, head_gain.reshape(1, -1))


def _dil_attn_body(q_ref, k_ref, v_ref, slope_ref, gn_ref, o_ref, *scr, seq, scale):
    blk = DIL_BACK
    npat = len(DIL_PATTERNS)
    lse_scr, out_scr = scr[:npat], scr[npat:]
    row = lax.broadcasted_iota(jnp.int32, (blk, blk), 0)
    col = lax.broadcasted_iota(jnp.int32, (blk, blk), 1)
    dist_cur = (row - col).astype(F32)
    dist_prev = dist_cur + float(blk)
    ok_cur = row >= col
    ok_prev_band = row <= col
    slope = slope_ref[...]

    for p, (window, d) in enumerate(DIL_PATTERNS):
        assert window // d == blk
        nblk = seq // d // blk
        assert nblk & (nblk - 1) == 0
        bias_cur = -(slope * (float(d) * LOG2E)) * dist_cur
        bias_prev = -(slope * (float(d) * LOG2E)) * dist_prev

        def group(gi, _, p=p, d=d, nblk=nblk, bias_cur=bias_cur, bias_prev=bias_prev):
            def rows(idx, back):
                r = lax.shift_right_logical(idx, nblk.bit_length() - 1)
                nb = lax.bitwise_and(idx, nblk - 1)
                b = jnp.maximum(nb - back, 0)
                start = r + b * (d * blk)
                if d == 1:
                    return pl.ds(pl.multiple_of(start, blk), blk)
                return pl.ds(start, blk, stride=d)

            idxs = [gi * DIL_GROUP + g for g in range(DIL_GROUP)]
            k_own = [k_ref[rows(idx, 0), :].astype(BF16) for idx in idxs]
            k_before = [k_ref[rows(idxs[0], 1), :].astype(BF16)] + k_own[:-1]
            scores = []
            for idx, kc, kp in zip(idxs, k_own, k_before):
                q = (q_ref[rows(idx, 0), :] * scale).astype(BF16)
                scores.append((_dot_nt(q, kc), _dot_nt(q, kp)))
            probs = []
            for idx, (s_cur, s_prev) in zip(idxs, scores):
                nb = lax.bitwise_and(idx, nblk - 1)
                s_cur = jnp.where(ok_cur, s_cur + bias_cur, NEG_BIG)
                no_prev = jnp.where(nb > 0, 0.0, NEG_BIG)
                s_prev = jnp.where(ok_prev_band, s_prev + bias_prev + no_prev, NEG_BIG)
                m_blk = jnp.max(jnp.maximum(s_cur, s_prev), axis=-1, keepdims=True)
                p_cur = jnp.exp2(s_cur - m_blk)
                p_prev = jnp.exp2(s_prev - m_blk)
                l_blk = jnp.broadcast_to(jnp.sum(p_cur + p_prev, axis=-1, keepdims=True),
                                         (blk, HEAD_DIM))
                lse_scr[p][rows(idx, 0), :] = m_blk + jnp.log(l_blk) * LOG2E
                probs.append((p_cur.astype(BF16), p_prev.astype(BF16), l_blk))
            v_own = [v_ref[rows(idx, 0), :].astype(BF16) for idx in idxs]
            v_before = [v_ref[rows(idxs[0], 1), :].astype(BF16)] + v_own[:-1]
            for idx, (p_cur, p_prev, l_blk), vc, vp in zip(idxs, probs, v_own, v_before):
                out_scr[p][rows(idx, 0), :] = (_dot(p_cur, vc) + _dot(p_prev, vp)) / l_blk
            return 0

        lax.fori_loop(0, d * nblk // DIL_GROUP, group, 0)

    def finish(i, _):
        rs = pl.ds(pl.multiple_of(i * blk, blk), blk)
        lses = [lse_scr[p][rs, :] for p in range(npat)]
        top = functools.reduce(jnp.maximum, lses)
        den = jnp.zeros((blk, HEAD_DIM), F32)
        num = jnp.zeros((blk, HEAD_DIM), F32)
        for p in range(npat):
            w = jnp.exp2(lses[p] - top)
            den = den + w
            num = num + out_scr[p][rs, :] * w
        o_ref[rs, :] = _rms(num / den, gn_ref[...]).astype(BF16)
        return 0

    lax.fori_loop(0, seq // blk, finish, 0, unroll=2)


def dil_attention(proj, slopes, head_gain, *, batch, seq):
    t = proj.shape[0]
    scale = float(HEAD_DIM ** -0.5) * LOG2E
    gain_col0 = MLA_HEADS + SB_HEADS
    return pl.pallas_call(
        functools.partial(_dil_attn_body, seq=seq, scale=scale),
        grid=(batch, DIL_HEADS),
        in_specs=[
            pl.BlockSpec((seq, HEAD_DIM), lambda b, h: (b, DIL_COL0 + h)),
            pl.BlockSpec((seq, HEAD_DIM), lambda b, h: (b, DIL_COL0 + DIL_HEADS + h)),
            pl.BlockSpec((seq, HEAD_DIM), lambda b, h: (b, DIL_COL0 + 2 * DIL_HEADS + h)),
            pl.BlockSpec((None, 1, LANES), lambda b, h: (h, 0, 0)),
            pl.BlockSpec((1, HEAD_DIM), lambda b, h: (0, gain_col0 + h)),
        ],
        out_specs=pl.BlockSpec((seq, HEAD_DIM), lambda b, h: (b, h)),
        out_shape=jax.ShapeDtypeStruct((t, DIL_W), BF16),
        scratch_shapes=[pltpu.VMEM((seq, HEAD_DIM), F32)] * (2 * len(DIL_PATTERNS)),
        compiler_params=_cparams(2),
        name="dil_attention",
    )(proj, proj, proj, slopes, head_gain.reshape(1, -1))


def _outproj_body(x_ref, o1_ref, o2_ref, o3_ref, w1_ref, w2_ref, w3_ref, out_ref):
    acc = _dot(o1_ref[...], w1_ref[...])
    acc = acc + _dot(o2_ref[...], w2_ref[...])
    acc = acc + _dot(o3_ref[...], w3_ref[...])
    out_ref[...] = x_ref[...] + acc


def out_proj(x, o_mla, o_sb, o_dil, w1, w2, w3, layer, *, tm=512):
    t, d = x.shape
    full = lambda a: pl.BlockSpec((None,) + a.shape[1:], lambda i: (layer, 0, 0),
                                  pipeline_mode=pl.Buffered(1))
    rows = lambda a: pl.BlockSpec((tm, a.shape[1]), lambda i: (i, 0))
    return pl.pallas_call(
        _outproj_body,
        grid=(t // tm,),
        in_specs=[rows(x), rows(o_mla), rows(o_sb), rows(o_dil), full(w1), full(w2), full(w3)],
        out_specs=rows(x),
        out_shape=jax.ShapeDtypeStruct((t, d), F32),
        compiler_params=_cparams(1),
        name="out_proj",
    )(x, o_mla, o_sb, o_dil, w1, w2, w3)


def _rope_tables(seq):
    inv = ROPE_THETA ** (-jnp.arange(0, MLA_ROPE, 2, dtype=F32) / MLA_ROPE)
    ang = jnp.arange(seq, dtype=F32)[:, None] * inv[None, :]
    cos, sin = jnp.cos(ang), jnp.sin(ang)
    reps = LANES // MLA_ROPE
    cos_t = jnp.tile(jnp.concatenate([cos, cos], axis=1), (1, reps))
    sin_t = jnp.tile(jnp.concatenate([-sin, sin], axis=1), (1, reps))
    return cos_t, sin_t


def _prep_w_in(w_in):
    w = w_in.astype(BF16)
    z = lambda n: jnp.zeros(w.shape[:-1] + (n,), BF16)
    return jnp.concatenate(
        [w[..., :MLA_IN], z(MLA_IN_PAD - MLA_IN), w[..., MLA_IN:], z(LANES)], axis=-1)


def _prep_w_uq(w):
    r = w.shape[0]
    w = w.reshape(r, MLA_HEADS, MLA_NOPE + MLA_ROPE)
    w = jnp.pad(w, ((0, 0), (0, 0), (0, MLA_QK_PAD - MLA_NOPE - MLA_ROPE)))
    return w.reshape(r, MLA_HEADS * MLA_QK_PAD).astype(BF16)


def _prep_w_ukv(w):
    r = w.shape[0]
    w = w.reshape(r, MLA_HEADS, 2, HEAD_DIM).transpose(0, 2, 1, 3)
    return w.reshape(r, 2 * MLA_HEADS * HEAD_DIM).astype(BF16)


def kernel(x, ffn1_norm, ffn1_w_gu, ffn1_w_down, mix_norm, w_in, mla_q_norm, w_mla_uq, mla_kv_norm, w_mla_ukv, head_out_norm, w_out, ffn2_norm, ffn2_w_gu, ffn2_w_down, final_norm_g):
    batch, seq, d = x.shape
    depth = ffn1_norm.shape[0]
    cos_t, sin_t = _rope_tables(seq)
    slopes = 2.0 ** (-ALIBI_MAX_EXP * jnp.arange(1, DIL_HEADS + 1, dtype=F32) / DIL_HEADS)
    slopes = jnp.broadcast_to(slopes[:, None, None], (DIL_HEADS, 1, LANES))
    w_mla = MLA_HEADS * HEAD_DIM

    w_gu1, w_gu2 = ffn1_w_gu.astype(BF16), ffn2_w_gu.astype(BF16)
    w_dn1, w_dn2 = (0.5 * ffn1_w_down).astype(BF16), (0.5 * ffn2_w_down).astype(BF16)
    w_in_all = _prep_w_in(w_in)
    wo = w_out.astype(BF16)
    wo_mla, wo_sb, wo_dil = wo[:, :w_mla], wo[:, w_mla:w_mla + SB_W], wo[:, w_mla + SB_W:]

    xt = x.reshape(batch * seq, d)
    for l in range(depth):
        xt = ffn_block(xt, ffn1_norm[l], w_gu1, w_dn1, l)

        proj = norm_proj(xt, mix_norm[l], w_in_all, l)
        q, k, v = mla_prep(proj, mla_q_norm[l], mla_kv_norm[l], _prep_w_uq(w_mla_uq[l]),
                           _prep_w_ukv(w_mla_ukv[l]), cos_t, sin_t, seq=seq)
        o_mla = mla_attention(q, k, v, head_out_norm[l], batch=batch, seq=seq)
        o_sb = sb_attention(proj, head_out_norm[l], batch=batch, seq=seq)
        o_dil = dil_attention(proj, slopes, head_out_norm[l], batch=batch, seq=seq)
        xt = out_proj(xt, o_mla, o_sb, o_dil, wo_mla, wo_sb, wo_dil, l)

        xt = ffn_block(xt, ffn2_norm[l], w_gu2, w_dn2, l,
                       final_norm_g if l == depth - 1 else None)
    return xt.reshape(batch, seq, d)
```

```python
import functools

import jax
import jax.numpy as jnp
from jax import lax
from jax.experimental import pallas as pl
from jax.experimental.pallas import tpu as pltpu

F32 = jnp.float32
BF16 = jnp.bfloat16

EPS = 1e-6
HEAD_DIM = 128
LANES = 128
MLA_HEADS = 6
MLA_Q_LORA = 512
MLA_KV_LORA = 256
MLA_NOPE = 128
MLA_ROPE = 64
MLA_QK_PAD = 256
ROPE_THETA = 10000.0
SB_HEADS = 4
DIL_HEADS = 6
DIL_PATTERNS = ((128, 1), (512, 4), (2048, 16))
DIL_BACK = 128
DIL_GROUP = 32
ALIBI_MAX_EXP = 8.0
NEG_BIG = -1e30
LOG2E = 1.4426950408889634
SB_KEYS = 256
FFN_TM = 1024
FFN_ROWS = 512
FFN_TF = 512
SB_SUBS = 1
SB_DEAD_BITS = 160.0

MLA_IN = MLA_Q_LORA + MLA_KV_LORA + MLA_ROPE
MLA_IN_PAD = 896
SB_W = SB_HEADS * HEAD_DIM
DIL_W = DIL_HEADS * HEAD_DIM
PROJ_HALF = MLA_IN_PAD + 3 * SB_W
assert PROJ_HALF == 3 * DIL_W + LANES
SB_COL0 = MLA_IN_PAD // LANES
DIL_COL0 = PROJ_HALF // LANES

VMEM_LIMIT = 56 * 1024 * 1024
VMEM_LIMIT_FFN = 62 * 1024 * 1024


def _cparams(grid_rank, vmem_limit=VMEM_LIMIT):
    return pltpu.CompilerParams(dimension_semantics=("arbitrary",) * grid_rank,
                                vmem_limit_bytes=vmem_limit)


def _rms(x, g):
    return x * lax.rsqrt(jnp.mean(x * x, axis=-1, keepdims=True) + EPS) * g


def _dot(a, b):
    return jnp.dot(a, b, preferred_element_type=F32)


def _dot_nt(a, b):
    return lax.dot_general(a, b, (((1,), (1,)), ((), ())), preferred_element_type=F32)


def _ffn_body(x_ref, g_ref, wg_ref, wu_ref, wd_ref, *rest, final_norm):
    fg_ref, o_ref, h_scr = rest if final_norm else (None,) + rest

    def hidden_tile(first):
        for r0 in range(0, h_scr.shape[0], FFN_ROWS):
            rows = slice(r0, r0 + FFN_ROWS)
            if first:
                h = _rms(x_ref[rows, :], g_ref[...]).astype(BF16)
                h_scr[rows, :] = h
            else:
                h = h_scr[rows, :]
            g = _dot(h, wg_ref[...])
            u = _dot(h, wu_ref[...])
            a = (g * (1.0 / (1.0 + jnp.exp(-g))) * u).astype(BF16)
            base = x_ref[rows, :] if first else o_ref[rows, :]
            o_ref[rows, :] = base + _dot(a, wd_ref[...])

    @pl.when(pl.program_id(1) == 0)
    def _():
        hidden_tile(True)

    @pl.when(pl.program_id(1) > 0)
    def _():
        hidden_tile(False)

    if final_norm:
        @pl.when(pl.program_id(1) == pl.num_programs(1) - 1)
        def _():
            o_ref[...] = _rms(o_ref[...], fg_ref[...])


def ffn_block(x, gain, w_gu, w_down_half, layer, final_gain=None, *, tm=FFN_TM, tf=FFN_TF):
    t, d = x.shape
    f = w_down_half.shape[1]
    nf = f // tf
    final_norm = final_gain is not None
    row = pl.BlockSpec((1, d), lambda i, j: (0, 0))
    return pl.pallas_call(
        functools.partial(_ffn_body, final_norm=final_norm),
        grid=(t // tm, nf),
        in_specs=[
            pl.BlockSpec((tm, d), lambda i, j: (i, 0)),
            row,
            pl.BlockSpec((None, d, tf), lambda i, j: (layer, 0, j)),
            pl.BlockSpec((None, d, tf), lambda i, j: (layer, 0, j + nf)),
            pl.BlockSpec((None, tf, d), lambda i, j: (layer, j, 0)),
        ] + ([row] if final_norm else []),
        out_specs=pl.BlockSpec((tm, d), lambda i, j: (i, 0)),
        out_shape=jax.ShapeDtypeStruct((t, d), F32),
        scratch_shapes=[pltpu.VMEM((tm, d), BF16)],
        compiler_params=_cparams(2, VMEM_LIMIT_FFN),
        name="ffn_block",
    )(x, gain.reshape(1, d), w_gu, w_gu, w_down_half,
      *([final_gain.reshape(1, d)] if final_norm else []))


def _normproj_body(x_ref, g_ref, w_ref, o_ref):
    for r0 in range(0, x_ref.shape[0], FFN_ROWS):
        rows = slice(r0, r0 + FFN_ROWS)
        h = _rms(x_ref[rows, :], g_ref[...]).astype(BF16)
        o_ref[rows, :] = _dot(h, w_ref[...])


def norm_proj(x, gain, w, layer, *, tm=1024):
    t, d = x.shape
    n = w.shape[2]
    tn = PROJ_HALF
    return pl.pallas_call(
        _normproj_body,
        grid=(n // tn, t // tm),
        in_specs=[
            pl.BlockSpec((tm, d), lambda c, i: (i, 0)),
            pl.BlockSpec((1, d), lambda c, i: (0, 0)),
            pl.BlockSpec((None, d, tn), lambda c, i: (layer, 0, c),
                         pipeline_mode=pl.Buffered(1)),
        ],
        out_specs=pl.BlockSpec((tm, tn), lambda c, i: (i, c)),
        out_shape=jax.ShapeDtypeStruct((t, n), F32),
        compiler_params=_cparams(2),
        name="norm_proj",
    )(x, gain.reshape(1, d), w)


def _mla_prep_body(a_ref, gq_ref, gkv_ref, wuq_ref, wukv_ref, cos_ref, sin_ref,
                   q_ref, k_ref, v_ref, *, scale):
    a = a_ref[...]
    qn = _rms(a[:, :MLA_Q_LORA], gq_ref[...]).astype(BF16)
    q = _dot(qn, wuq_ref[...]) * scale
    kvn = _rms(a[:, MLA_Q_LORA:MLA_Q_LORA + MLA_KV_LORA], gkv_ref[...]).astype(BF16)
    kv = _dot(kvn, wukv_ref[...])

    cos_t = cos_ref[...]
    sin_t = sin_ref[...]
    lane = lax.broadcasted_iota(jnp.int32, cos_t.shape, 1)
    first_half = (lane & (MLA_ROPE // 2)) == 0

    def rope(x):
        swapped = jnp.where(first_half,
                            pltpu.roll(x, LANES - MLA_ROPE // 2, 1),
                            pltpu.roll(x, MLA_ROPE // 2, 1))
        return x * cos_t + swapped * sin_t

    k_rope = rope(a[:, MLA_Q_LORA + MLA_KV_LORA:]).astype(BF16)
    nk = MLA_HEADS * MLA_NOPE
    for h in range(MLA_HEADS):
        c0 = h * MLA_QK_PAD
        q_ref[:, c0:c0 + LANES] = q[:, c0:c0 + LANES].astype(BF16)
        q_ref[:, c0 + LANES:c0 + 2 * LANES] = rope(q[:, c0 + LANES:c0 + 2 * LANES]).astype(BF16)
        k_ref[:, c0:c0 + LANES] = kv[:, h * MLA_NOPE:(h + 1) * MLA_NOPE].astype(BF16)
        k_ref[:, c0 + LANES:c0 + 2 * LANES] = k_rope
    v_ref[...] = kv[:, nk:].astype(BF16)


def mla_prep(proj, gq, gkv, wuq, wukv, cos_t, sin_t, *, seq, tm=1024):
    t = proj.shape[0]
    nseq = seq // tm
    wq = MLA_HEADS * MLA_QK_PAD
    wv = MLA_HEADS * HEAD_DIM
    scale = float((MLA_NOPE + MLA_ROPE) ** -0.5) * LOG2E
    return pl.pallas_call(
        functools.partial(_mla_prep_body, scale=scale),
        grid=(t // tm,),
        in_specs=[
            pl.BlockSpec((tm, MLA_IN_PAD), lambda i: (i, 0)),
            pl.BlockSpec((1, MLA_Q_LORA), lambda i: (0, 0)),
            pl.BlockSpec((1, MLA_KV_LORA), lambda i: (0, 0)),
            pl.BlockSpec((MLA_Q_LORA, wq), lambda i: (0, 0)),
            pl.BlockSpec((MLA_KV_LORA, 2 * wv), lambda i: (0, 0)),
            pl.BlockSpec((tm, LANES), lambda i: (i % nseq, 0)),
            pl.BlockSpec((tm, LANES), lambda i: (i % nseq, 0)),
        ],
        out_specs=[
            pl.BlockSpec((tm, wq), lambda i: (i, 0)),
            pl.BlockSpec((tm, wq), lambda i: (i, 0)),
            pl.BlockSpec((tm, wv), lambda i: (i, 0)),
        ],
        out_shape=[
            jax.ShapeDtypeStruct((t, wq), BF16),
            jax.ShapeDtypeStruct((t, wq), BF16),
            jax.ShapeDtypeStruct((t, wv), BF16),
        ],
        compiler_params=_cparams(1),
        name="mla_prep",
    )(proj, gq.reshape(1, -1), gkv.reshape(1, -1), wuq, wukv, cos_t, sin_t)


def _mla_attn_body(q_ref, k_ref, v_ref, gn_ref, o_ref, acc_scr, *, seq, nc, rc):
    tq = nc * rc
    row = lax.broadcasted_iota(jnp.int32, (rc, rc), 0)
    col = lax.broadcasted_iota(jnp.int32, (rc, rc), 1)
    causal = col <= row

    def run(q0, jobs, states):
        scores = [_dot_nt(q_ref[pl.ds(q0 + c * rc, rc), :], k_ref[pl.ds(k0, rc), :])
                  for c, k0, _ in jobs]
        states = list(states)
        for (c, k0, diag), s in zip(jobs, scores):
            if diag:
                s = jnp.where(causal, s, NEG_BIG)
            m, l = states[c]
            m_new = jnp.maximum(m, jnp.max(s, axis=1, keepdims=True))
            alpha = jnp.exp2(m - m_new)
            p = jnp.exp2(s - m_new)
            l = alpha * l + jnp.sum(p, axis=1, keepdims=True)
            acc_scr[c] = alpha * acc_scr[c] + _dot(p.astype(BF16), v_ref[pl.ds(k0, rc), :])
            states[c] = (m_new, l)
        return tuple(states)

    def q_loop(qi, _):
        q0 = pl.multiple_of(qi * tq, tq)
        for c in range(nc):
            acc_scr[c] = jnp.zeros((rc, HEAD_DIM), F32)
        init = tuple((jnp.full((rc, 1), NEG_BIG, F32), jnp.zeros((rc, 1), F32))
                     for _ in range(nc))

        def kv_loop(i, states):
            k0 = pl.multiple_of(i * tq, tq)
            return run(q0, [(c, k0 + t * rc, False) for t in range(nc) for c in range(nc)],
                       states)

        states = lax.fori_loop(0, qi, kv_loop, init)
        states = run(q0, [(c, q0 + t * rc, t == c) for t in range(nc) for c in range(t, nc)],
                     states)
        for c, (_, l) in enumerate(states):
            o_ref[pl.ds(q0 + c * rc, rc), :] = _rms(acc_scr[c] / l, gn_ref[...]).astype(BF16)
        return 0

    lax.fori_loop(0, seq // tq, q_loop, 0)


def mla_attention(q, k, v, head_gain, *, batch, seq, nc=4, rc=512):
    t = q.shape[0]
    return pl.pallas_call(
        functools.partial(_mla_attn_body, seq=seq, nc=nc, rc=rc),
        grid=(batch, MLA_HEADS),
        in_specs=[
            pl.BlockSpec((seq, MLA_QK_PAD), lambda b, h: (b, h)),
            pl.BlockSpec((seq, MLA_QK_PAD), lambda b, h: (b, h)),
            pl.BlockSpec((seq, HEAD_DIM), lambda b, h: (b, h)),
            pl.BlockSpec((1, HEAD_DIM), lambda b, h: (0, h)),
        ],
        out_specs=pl.BlockSpec((seq, HEAD_DIM), lambda b, h: (b, h)),
        out_shape=jax.ShapeDtypeStruct((t, MLA_HEADS * HEAD_DIM), BF16),
        scratch_shapes=[pltpu.VMEM((nc, rc, HEAD_DIM), F32)],
        compiler_params=_cparams(2),
        name="mla_attention",
    )(q, k, v, head_gain.reshape(1, -1))


def _sb_attn_body(q_ref, k_ref, v_ref, gn_ref, o_ref, qb_scr, kb_scr, vb_scr, acc_scr,
                  *, seq, rc, scale):
    tk = SB_KEYS
    own = rc // tk
    qb_scr[...] = (q_ref[...] * scale).astype(BF16)
    kb_scr[...] = k_ref[...].astype(BF16)
    vb_scr[...] = v_ref[...].astype(BF16)
    urow = lax.broadcasted_iota(jnp.int32, (tk, tk), 0)
    ucol = lax.broadcasted_iota(jnp.int32, (tk, tk), 1)
    later_keys = (urow > ucol).astype(BF16)

    def absorb(q0, r0, ks, rest, diag):
        n = rc - r0
        k0 = pl.multiple_of(ks * tk, tk)
        z = _dot_nt(qb_scr[pl.ds(q0 + r0, n), :], kb_scr[pl.ds(k0, tk), :])
        hi = jnp.maximum(z, 0.0)
        lo = jnp.minimum(z, 0.0)
        t = jnp.log(1.0 + jnp.exp2(lo - hi)) * LOG2E
        sp = hi + t
        ls = lo - t
        if diag:
            strict = (lax.broadcasted_iota(jnp.int32, (n, tk), 1)
                      < lax.broadcasted_iota(jnp.int32, (n, tk), 0))
            sp = jnp.where(strict, sp, 0.0)
        later = _dot(sp.astype(BF16), later_keys)
        a = jnp.exp2(ls - later - rest)
        if diag:
            a = jnp.where(strict, a, 0.0)
        acc_scr[r0:, :] += _dot(a.astype(BF16), vb_scr[pl.ds(k0, tk), :])
        return rest + later[:, 0:1] + sp[:, 0:1]

    def chunk(ci, _):
        q0 = pl.multiple_of(ci * rc, rc)
        acc_scr[...] = jnp.zeros((rc, HEAD_DIM), F32)
        rest = jnp.zeros((tk, 1), F32)
        for j in range(own - 1, -1, -1):
            rest = absorb(q0, j * tk, ci * own + j, rest, True)
            if j > 0:
                rest = jnp.concatenate([jnp.zeros((tk, 1), F32), rest], axis=0)

        def alive(state):
            ks, rest = state
            return jnp.logical_and(ks >= 0, jnp.min(rest) < SB_DEAD_BITS)

        def walk(state):
            ks, rest = state
            for t in range(SB_SUBS):
                rest = absorb(q0, 0, ks - t, rest, False)
            return ks - SB_SUBS, rest

        lax.while_loop(alive, walk, (ci * own - 1, rest))
        o_ref[pl.ds(q0, rc), :] = _rms(acc_scr[...], gn_ref[...]).astype(BF16)
        return 0

    lax.fori_loop(0, seq // rc, chunk, 0)


def sb_attention(proj, head_gain, *, batch, seq, rc=512):
    assert (rc // SB_KEYS) % SB_SUBS == 0
    t = proj.shape[0]
    scale = float(HEAD_DIM ** -0.5) * LOG2E
    gain_col0 = MLA_HEADS
    return pl.pallas_call(
        functools.partial(_sb_attn_body, seq=seq, rc=rc, scale=scale),
        grid=(batch, SB_HEADS),
        in_specs=[
            pl.BlockSpec((seq, HEAD_DIM), lambda b, h: (b, SB_COL0 + h)),
            pl.BlockSpec((seq, HEAD_DIM), lambda b, h: (b, SB_COL0 + SB_HEADS + h)),
            pl.BlockSpec((seq, HEAD_DIM), lambda b, h: (b, SB_COL0 + 2 * SB_HEADS + h)),
            pl.BlockSpec((1, HEAD_DIM), lambda b, h: (0, gain_col0 + h)),
        ],
        out_specs=pl.BlockSpec((seq, HEAD_DIM), lambda b, h: (b, h)),
        out_shape=jax.ShapeDtypeStruct((t, SB_W), BF16),
        scratch_shapes=[pltpu.VMEM((seq, HEAD_DIM), BF16),
                        pltpu.VMEM((seq, HEAD_DIM), BF16),
                        pltpu.VMEM((seq, HEAD_DIM), BF16),
                        pltpu.VMEM((rc, HEAD_DIM), F32)],
        compiler_params=_cparams(2),
        name="sb_attention",
    )(proj, proj, proj, head_gain.reshape(1, -1))


def _dil_attn_body(q_ref, k_ref, v_ref, slope_ref, gn_ref, o_ref, *scr, seq, scale):
    blk = DIL_BACK
    npat = len(DIL_PATTERNS)
    lse_scr, out_scr = scr[:npat], scr[npat:]
    row = lax.broadcasted_iota(jnp.int32, (blk, blk), 0)
    col = lax.broadcasted_iota(jnp.int32, (blk, blk), 1)
    dist_cur = (row - col).astype(F32)
    dist_prev = dist_cur + float(blk)
    ok_cur = row >= col
    ok_prev_band = row <= col
    slope = slope_ref[...]

    for p, (window, d) in enumerate(DIL_PATTERNS):
        assert window // d == blk
        nblk = seq // d // blk
        assert nblk & (nblk - 1) == 0
        bias_cur = -(slope * (float(d) * LOG2E)) * dist_cur
        bias_prev = -(slope * (float(d) * LOG2E)) * dist_prev

        def group(gi, _, p=p, d=d, nblk=nblk, bias_cur=bias_cur, bias_prev=bias_prev):
            def rows(idx, back):
                r = lax.shift_right_logical(idx, nblk.bit_length() - 1)
                nb = lax.bitwise_and(idx, nblk - 1)
                b = jnp.maximum(nb - back, 0)
                start = r + b * (d * blk)
                if d == 1:
                    return pl.ds(pl.multiple_of(start, blk), blk)
                return pl.ds(start, blk, stride=d)

            idxs = [gi * DIL_GROUP + g for g in range(DIL_GROUP)]
            k_own = [k_ref[rows(idx, 0), :].astype(BF16) for idx in idxs]
            k_before = [k_ref[rows(idxs[0], 1), :].astype(BF16)] + k_own[:-1]
            scores = []
            for idx, kc, kp in zip(idxs, k_own, k_before):
                q = (q_ref[rows(idx, 0), :] * scale).astype(BF16)
                scores.append((_dot_nt(q, kc), _dot_nt(q, kp)))
            probs = []
            for idx, (s_cur, s_prev) in zip(idxs, scores):
                nb = lax.bitwise_and(idx, nblk - 1)
                s_cur = jnp.where(ok_cur, s_cur + bias_cur, NEG_BIG)
                no_prev = jnp.where(nb > 0, 0.0, NEG_BIG)
                s_prev = jnp.where(ok_prev_band, s_prev + bias_prev + no_prev, NEG_BIG)
                m_blk = jnp.max(jnp.maximum(s_cur, s_prev), axis=-1, keepdims=True)
                p_cur = jnp.exp2(s_cur - m_blk)
                p_prev = jnp.exp2(s_prev - m_blk)
                l_blk = jnp.broadcast_to(jnp.sum(p_cur + p_prev, axis=-1, keepdims=True),
                                         (blk, HEAD_DIM))
                lse_scr[p][rows(idx, 0), :] = m_blk + jnp.log(l_blk) * LOG2E
                probs.append((p_cur.astype(BF16), p_prev.astype(BF16), l_blk))
            v_own = [v_ref[rows(idx, 0), :].astype(BF16) for idx in idxs]
            v_before = [v_ref[rows(idxs[0], 1), :].astype(BF16)] + v_own[:-1]
            for idx, (p_cur, p_prev, l_blk), vc, vp in zip(idxs, probs, v_own, v_before):
                out_scr[p][rows(idx, 0), :] = (_dot(p_cur, vc) + _dot(p_prev, vp)) / l_blk
            return 0

        lax.fori_loop(0, d * nblk // DIL_GROUP, group, 0)

    def finish(i, _):
        rs = pl.ds(pl.multiple_of(i * blk, blk), blk)
        lses = [lse_scr[p][rs, :] for p in range(npat)]
        top = functools.reduce(jnp.maximum, lses)
        den = jnp.zeros((blk, HEAD_DIM), F32)
        num = jnp.zeros((blk, HEAD_DIM), F32)
        for p in range(npat):
            w = jnp.exp2(lses[p] - top)
            den = den + w
            num = num + out_scr[p][rs, :] * w
        o_ref[rs, :] = _rms(num / den, gn_ref[...]).astype(BF16)
        return 0

    lax.fori_loop(0, seq // blk, finish, 0, unroll=8)


def dil_attention(proj, slopes, head_gain, *, batch, seq):
    t = proj.shape[0]
    scale = float(HEAD_DIM ** -0.5) * LOG2E
    gain_col0 = MLA_HEADS + SB_HEADS
    return pl.pallas_call(
        functools.partial(_dil_attn_body, seq=seq, scale=scale),
        grid=(batch, DIL_HEADS),
        in_specs=[
            pl.BlockSpec((seq, HEAD_DIM), lambda b, h: (b, DIL_COL0 + h)),
            pl.BlockSpec((seq, HEAD_DIM), lambda b, h: (b, DIL_COL0 + DIL_HEADS + h)),
            pl.BlockSpec((seq, HEAD_DIM), lambda b, h: (b, DIL_COL0 + 2 * DIL_HEADS + h)),
            pl.BlockSpec((None, 1, LANES), lambda b, h: (h, 0, 0)),
            pl.BlockSpec((1, HEAD_DIM), lambda b, h: (0, gain_col0 + h)),
        ],
        out_specs=pl.BlockSpec((seq, HEAD_DIM), lambda b, h: (b, h)),
        out_shape=jax.ShapeDtypeStruct((t, DIL_W), BF16),
        scratch_shapes=[pltpu.VMEM((seq, HEAD_DIM), F32)] * (2 * len(DIL_PATTERNS)),
        compiler_params=_cparams(2),
        name="dil_attention",
    )(proj, proj, proj, slopes, head_gain.reshape(1, -1))


def _outproj_body(x_ref, o1_ref, o2_ref, o3_ref, w1_ref, w2_ref, w3_ref, out_ref):
    acc = _dot(o1_ref[...], w1_ref[...])
    acc = acc + _dot(o2_ref[...], w2_ref[...])
    acc = acc + _dot(o3_ref[...], w3_ref[...])
    out_ref[...] = x_ref[...] + acc


def out_proj(x, o_mla, o_sb, o_dil, w1, w2, w3, layer, *, tm=512):
    t, d = x.shape
    full = lambda a: pl.BlockSpec((None,) + a.shape[1:], lambda i: (layer, 0, 0),
                                  pipeline_mode=pl.Buffered(1))
    rows = lambda a: pl.BlockSpec((tm, a.shape[1]), lambda i: (i, 0))
    return pl.pallas_call(
        _outproj_body,
        grid=(t // tm,),
        in_specs=[rows(x), rows(o_mla), rows(o_sb), rows(o_dil), full(w1), full(w2), full(w3)],
        out_specs=rows(x),
        out_shape=jax.ShapeDtypeStruct((t, d), F32),
        compiler_params=_cparams(1),
        name="out_proj",
    )(x, o_mla, o_sb, o_dil, w1, w2, w3)


def _rope_tables(seq):
    inv = ROPE_THETA ** (-jnp.arange(0, MLA_ROPE, 2, dtype=F32) / MLA_ROPE)
    ang = jnp.arange(seq, dtype=F32)[:, None] * inv[None, :]
    cos, sin = jnp.cos(ang), jnp.sin(ang)
    reps = LANES // MLA_ROPE
    cos_t = jnp.tile(jnp.concatenate([cos, cos], axis=1), (1, reps))
    sin_t = jnp.tile(jnp.concatenate([-sin, sin], axis=1), (1, reps))
    return cos_t, sin_t


def _prep_w_in(w_in):
    w = w_in.astype(BF16)
    z = lambda n: jnp.zeros(w.shape[:-1] + (n,), BF16)
    return jnp.concatenate(
        [w[..., :MLA_IN], z(MLA_IN_PAD - MLA_IN), w[..., MLA_IN:], z(LANES)], axis=-1)


def _prep_w_uq(w):
    r = w.shape[0]
    w = w.reshape(r, MLA_HEADS, MLA_NOPE + MLA_ROPE)
    w = jnp.pad(w, ((0, 0), (0, 0), (0, MLA_QK_PAD - MLA_NOPE - MLA_ROPE)))
    return w.reshape(r, MLA_HEADS * MLA_QK_PAD).astype(BF16)


def _prep_w_ukv(w):
    r = w.shape[0]
    w = w.reshape(r, MLA_HEADS, 2, HEAD_DIM).transpose(0, 2, 1, 3)
    return w.reshape(r, 2 * MLA_HEADS * HEAD_DIM).astype(BF16)


def kernel(x, ffn1_norm, ffn1_w_gu, ffn1_w_down, mix_norm, w_in, mla_q_norm, w_mla_uq, mla_kv_norm, w_mla_ukv, head_out_norm, w_out, ffn2_norm, ffn2_w_gu, ffn2_w_down, final_norm_g):
    batch, seq, d = x.shape
    depth = ffn1_norm.shape[0]
    cos_t, sin_t = _rope_tables(seq)
    slopes = 2.0 ** (-ALIBI_MAX_EXP * jnp.arange(1, DIL_HEADS + 1, dtype=F32) / DIL_HEADS)
    slopes = jnp.broadcast_to(slopes[:, None, None], (DIL_HEADS, 1, LANES))
    w_mla = MLA_HEADS * HEAD_DIM

    w_gu1, w_gu2 = ffn1_w_gu.astype(BF16), ffn2_w_gu.astype(BF16)
    w_dn1, w_dn2 = (0.5 * ffn1_w_down).astype(BF16), (0.5 * ffn2_w_down).astype(BF16)
    w_in_all = _prep_w_in(w_in)
    wo = w_out.astype(BF16)
    wo_mla, wo_sb, wo_dil = wo[:, :w_mla], wo[:, w_mla:w_mla + SB_W], wo[:, w_mla + SB_W:]

    xt = x.reshape(batch * seq, d)
    for l in range(depth):
        xt = ffn_block(xt, ffn1_norm[l], w_gu1, w_dn1, l)

        proj = norm_proj(xt, mix_norm[l], w_in_all, l)
        q, k, v = mla_prep(proj, mla_q_norm[l], mla_kv_norm[l], _prep_w_uq(w_mla_uq[l]),
                           _prep_w_ukv(w_mla_ukv[l]), cos_t, sin_t, seq=seq)
        o_mla = mla_attention(q, k, v, head_out_norm[l], batch=batch, seq=seq)
        o_sb = sb_attention(proj, head_out_norm[l], batch=batch, seq=seq)
        o_dil = dil_attention(proj, slopes, head_out_norm[l], batch=batch, seq=seq)
        xt = out_proj(xt, o_mla, o_sb, o_dil, wo_mla, wo_sb, wo_dil, l)

        xt = ffn_block(xt, ffn2_norm[l], w_gu2, w_dn2, l,
                       final_norm_g if l == depth - 1 else None)
    return xt.reshape(batch, seq, d)
```

```python
import functools

import jax
import jax.numpy as jnp
from jax import lax
from jax.experimental import pallas as pl
from jax.experimental.pallas import tpu as pltpu

F32 = jnp.float32
BF16 = jnp.bfloat16

EPS = 1e-6
HEAD_DIM = 128
LANES = 128
MLA_HEADS = 6
MLA_Q_LORA = 512
MLA_KV_LORA = 256
MLA_NOPE = 128
MLA_ROPE = 64
MLA_QK_PAD = 256
ROPE_THETA = 10000.0
SB_HEADS = 4
DIL_HEADS = 6
DIL_PATTERNS = ((128, 1), (512, 4), (2048, 16))
DIL_BACK = 128
DIL_GROUP = 32
ALIBI_MAX_EXP = 8.0
NEG_BIG = -1e30
LOG2E = 1.4426950408889634
SB_KEYS = 256
FFN_TM = 1024
FFN_ROWS = 512
FFN_TF = 512
SB_SUBS = 1
SB_DEAD_BITS = 160.0

MLA_IN = MLA_Q_LORA + MLA_KV_LORA + MLA_ROPE
MLA_IN_PAD = 896
SB_W = SB_HEADS * HEAD_DIM
DIL_W = DIL_HEADS * HEAD_DIM
PROJ_HALF = MLA_IN_PAD + 3 * SB_W
assert PROJ_HALF == 3 * DIL_W + LANES
SB_COL0 = MLA_IN_PAD // LANES
DIL_COL0 = PROJ_HALF // LANES

VMEM_LIMIT = 56 * 1024 * 1024
VMEM_LIMIT_FFN = 62 * 1024 * 1024


def _cparams(grid_rank, vmem_limit=VMEM_LIMIT):
    return pltpu.CompilerParams(dimension_semantics=("arbitrary",) * grid_rank,
                                vmem_limit_bytes=vmem_limit)


def _rms(x, g):
    return x * lax.rsqrt(jnp.mean(x * x, axis=-1, keepdims=True) + EPS) * g


def _dot(a, b):
    return jnp.dot(a, b, preferred_element_type=F32)


def _dot_nt(a, b):
    return lax.dot_general(a, b, (((1,), (1,)), ((), ())), preferred_element_type=F32)


def _ffn_body(x_ref, g_ref, wg_ref, wu_ref, wd_ref, *rest, final_norm):
    fg_ref, o_ref, h_scr = rest if final_norm else (None,) + rest

    def hidden_tile(first):
        for r0 in range(0, h_scr.shape[0], FFN_ROWS):
            rows = slice(r0, r0 + FFN_ROWS)
            if first:
                h = _rms(x_ref[rows, :], g_ref[...]).astype(BF16)
                h_scr[rows, :] = h
            else:
                h = h_scr[rows, :]
            g = _dot(h, wg_ref[...])
            u = _dot(h, wu_ref[...])
            a = (g * (1.0 / (1.0 + jnp.exp(-g))) * u).astype(BF16)
            base = x_ref[rows, :] if first else o_ref[rows, :]
            o_ref[rows, :] = base + _dot(a, wd_ref[...])

    @pl.when(pl.program_id(1) == 0)
    def _():
        hidden_tile(True)

    @pl.when(pl.program_id(1) > 0)
    def _():
        hidden_tile(False)

    if final_norm:
        @pl.when(pl.program_id(1) == pl.num_programs(1) - 1)
        def _():
            o_ref[...] = _rms(o_ref[...], fg_ref[...])


def ffn_block(x, gain, w_gu, w_down_half, layer, final_gain=None, *, tm=FFN_TM, tf=FFN_TF):
    t, d = x.shape
    f = w_down_half.shape[1]
    nf = f // tf
    final_norm = final_gain is not None
    row = pl.BlockSpec((1, d), lambda i, j: (0, 0))
    return pl.pallas_call(
        functools.partial(_ffn_body, final_norm=final_norm),
        grid=(t // tm, nf),
        in_specs=[
            pl.BlockSpec((tm, d), lambda i, j: (i, 0)),
            row,
            pl.BlockSpec((None, d, tf), lambda i, j: (layer, 0, j)),
            pl.BlockSpec((None, d, tf), lambda i, j: (layer, 0, j + nf)),
            pl.BlockSpec((None, tf, d), lambda i, j: (layer, j, 0)),
        ] + ([row] if final_norm else []),
        out_specs=pl.BlockSpec((tm, d), lambda i, j: (i, 0)),
        out_shape=jax.ShapeDtypeStruct((t, d), F32),
        scratch_shapes=[pltpu.VMEM((tm, d), BF16)],
        compiler_params=_cparams(2, VMEM_LIMIT_FFN),
        name="ffn_block",
    )(x, gain.reshape(1, d), w_gu, w_gu, w_down_half,
      *([final_gain.reshape(1, d)] if final_norm else []))


def _normproj_body(x_ref, g_ref, w_ref, o_ref):
    for r0 in range(0, x_ref.shape[0], FFN_ROWS):
        rows = slice(r0, r0 + FFN_ROWS)
        h = _rms(x_ref[rows, :], g_ref[...]).astype(BF16)
        o_ref[rows, :] = _dot(h, w_ref[...])


def norm_proj(x, gain, w, layer, *, tm=1024):
    t, d = x.shape
    n = w.shape[2]
    tn = PROJ_HALF
    return pl.pallas_call(
        _normproj_body,
        grid=(n // tn, t // tm),
        in_specs=[
            pl.BlockSpec((tm, d), lambda c, i: (i, 0)),
            pl.BlockSpec((1, d), lambda c, i: (0, 0)),
            pl.BlockSpec((None, d, tn), lambda c, i: (layer, 0, c),
                         pipeline_mode=pl.Buffered(1)),
        ],
        out_specs=pl.BlockSpec((tm, tn), lambda c, i: (i, c)),
        out_shape=jax.ShapeDtypeStruct((t, n), F32),
        compiler_params=_cparams(2),
        name="norm_proj",
    )(x, gain.reshape(1, d), w)


def _mla_prep_body(a_ref, gq_ref, gkv_ref, wuq_ref, wukv_ref, cos_ref, sin_ref,
                   q_ref, k_ref, v_ref, *, scale):
    a = a_ref[...]
    qn = _rms(a[:, :MLA_Q_LORA], gq_ref[...]).astype(BF16)
    q = _dot(qn, wuq_ref[...]) * scale
    kvn = _rms(a[:, MLA_Q_LORA:MLA_Q_LORA + MLA_KV_LORA], gkv_ref[...]).astype(BF16)
    kv = _dot(kvn, wukv_ref[...])

    cos_t = cos_ref[...]
    sin_t = sin_ref[...]
    lane = lax.broadcasted_iota(jnp.int32, cos_t.shape, 1)
    first_half = (lane & (MLA_ROPE // 2)) == 0

    def rope(x):
        swapped = jnp.where(first_half,
                            pltpu.roll(x, LANES - MLA_ROPE // 2, 1),
                            pltpu.roll(x, MLA_ROPE // 2, 1))
        return x * cos_t + swapped * sin_t

    k_rope = rope(a[:, MLA_Q_LORA + MLA_KV_LORA:]).astype(BF16)
    nk = MLA_HEADS * MLA_NOPE
    for h in range(MLA_HEADS):
        c0 = h * MLA_QK_PAD
        q_ref[:, c0:c0 + LANES] = q[:, c0:c0 + LANES].astype(BF16)
        q_ref[:, c0 + LANES:c0 + 2 * LANES] = rope(q[:, c0 + LANES:c0 + 2 * LANES]).astype(BF16)
        k_ref[:, c0:c0 + LANES] = kv[:, h * MLA_NOPE:(h + 1) * MLA_NOPE].astype(BF16)
        k_ref[:, c0 + LANES:c0 + 2 * LANES] = k_rope
    v_ref[...] = kv[:, nk:].astype(BF16)


def mla_prep(proj, gq, gkv, wuq, wukv, cos_t, sin_t, *, seq, tm=1024):
    t = proj.shape[0]
    nseq = seq // tm
    wq = MLA_HEADS * MLA_QK_PAD
    wv = MLA_HEADS * HEAD_DIM
    scale = float((MLA_NOPE + MLA_ROPE) ** -0.5) * LOG2E
    return pl.pallas_call(
        functools.partial(_mla_prep_body, scale=scale),
        grid=(t // tm,),
        in_specs=[
            pl.BlockSpec((tm, MLA_IN_PAD), lambda i: (i, 0)),
            pl.BlockSpec((1, MLA_Q_LORA), lambda i: (0, 0)),
            pl.BlockSpec((1, MLA_KV_LORA), lambda i: (0, 0)),
            pl.BlockSpec((MLA_Q_LORA, wq), lambda i: (0, 0)),
            pl.BlockSpec((MLA_KV_LORA, 2 * wv), lambda i: (0, 0)),
            pl.BlockSpec((tm, LANES), lambda i: (i % nseq, 0)),
            pl.BlockSpec((tm, LANES), lambda i: (i % nseq, 0)),
        ],
        out_specs=[
            pl.BlockSpec((tm, wq), lambda i: (i, 0)),
            pl.BlockSpec((tm, wq), lambda i: (i, 0)),
            pl.BlockSpec((tm, wv), lambda i: (i, 0)),
        ],
        out_shape=[
            jax.ShapeDtypeStruct((t, wq), BF16),
            jax.ShapeDtypeStruct((t, wq), BF16),
            jax.ShapeDtypeStruct((t, wv), BF16),
        ],
        compiler_params=_cparams(1),
        name="mla_prep",
    )(proj, gq.reshape(1, -1), gkv.reshape(1, -1), wuq, wukv, cos_t, sin_t)


def _mla_attn_body(q_ref, k_ref, v_ref, gn_ref, o_ref, acc_scr, *, seq, nc, rc):
    tq = nc * rc
    row = lax.broadcasted_iota(jnp.int32, (rc, rc), 0)
    col = lax.broadcasted_iota(jnp.int32, (rc, rc), 1)
    causal = col <= row

    def run(q0, jobs, states):
        scores = [_dot_nt(q_ref[pl.ds(q0 + c * rc, rc), :], k_ref[pl.ds(k0, rc), :])
                  for c, k0, _ in jobs]
        states = list(states)
        for (c, k0, diag), s in zip(jobs, scores):
            if diag:
                s = jnp.where(causal, s, NEG_BIG)
            m, l = states[c]
            m_new = jnp.maximum(m, jnp.max(s, axis=1, keepdims=True))
            alpha = jnp.exp2(m - m_new)
            p = jnp.exp2(s - m_new)
            l = alpha * l + jnp.sum(p, axis=1, keepdims=True)
            acc_scr[c] = alpha * acc_scr[c] + _dot(p.astype(BF16), v_ref[pl.ds(k0, rc), :])
            states[c] = (m_new, l)
        return tuple(states)

    def q_loop(qi, _):
        q0 = pl.multiple_of(qi * tq, tq)
        for c in range(nc):
            acc_scr[c] = jnp.zeros((rc, HEAD_DIM), F32)
        init = tuple((jnp.full((rc, 1), NEG_BIG, F32), jnp.zeros((rc, 1), F32))
                     for _ in range(nc))

        def kv_loop(i, states):
            k0 = pl.multiple_of(i * tq, tq)
            return run(q0, [(c, k0 + t * rc, False) for t in range(nc) for c in range(nc)],
                       states)

        states = lax.fori_loop(0, qi, kv_loop, init)
        states = run(q0, [(c, q0 + t * rc, t == c) for t in range(nc) for c in range(t, nc)],
                     states)
        for c, (_, l) in enumerate(states):
            o_ref[pl.ds(q0 + c * rc, rc), :] = _rms(acc_scr[c] / l, gn_ref[...]).astype(BF16)
        return 0

    lax.fori_loop(0, seq // tq, q_loop, 0)


def mla_attention(q, k, v, head_gain, *, batch, seq, nc=4, rc=512):
    t = q.shape[0]
    return pl.pallas_call(
        functools.partial(_mla_attn_body, seq=seq, nc=nc, rc=rc),
        grid=(batch, MLA_HEADS),
        in_specs=[
            pl.BlockSpec((seq, MLA_QK_PAD), lambda b, h: (b, h)),
            pl.BlockSpec((seq, MLA_QK_PAD), lambda b, h: (b, h)),
            pl.BlockSpec((seq, HEAD_DIM), lambda b, h: (b, h)),
            pl.BlockSpec((1, HEAD_DIM), lambda b, h: (0, h)),
        ],
        out_specs=pl.BlockSpec((seq, HEAD_DIM), lambda b, h: (b, h)),
        out_shape=jax.ShapeDtypeStruct((t, MLA_HEADS * HEAD_DIM), BF16),
        scratch_shapes=[pltpu.VMEM((nc, rc, HEAD_DIM), F32)],
        compiler_params=_cparams(2),
        name="mla_attention",
    )(q, k, v, head_gain.reshape(1, -1))


def _sb_attn_body(q_ref, k_ref, v_ref, gn_ref, o_ref, qb_scr, kb_scr, vb_scr, acc_scr,
                  *, seq, rc, scale):
    tk = SB_KEYS
    own = rc // tk
    qb_scr[...] = (q_ref[...] * scale).astype(BF16)
    kb_scr[...] = k_ref[...].astype(BF16)
    vb_scr[...] = v_ref[...].astype(BF16)
    urow = lax.broadcasted_iota(jnp.int32, (tk, tk), 0)
    ucol = lax.broadcasted_iota(jnp.int32, (tk, tk), 1)
    later_keys = (urow > ucol).astype(BF16)

    def run(jobs, rests):
        starts = [pl.multiple_of(ks * tk, tk) for _, _, _, ks, _ in jobs]
        zs = [_dot_nt(qb_scr[pl.ds(q0 + r0, rc - r0), :], kb_scr[pl.ds(k0, tk), :])
              for (_, q0, r0, _, _), k0 in zip(jobs, starts)]
        parts = []
        for (_, _, r0, _, diag), z in zip(jobs, zs):
            hi = jnp.maximum(z, 0.0)
            lo = jnp.minimum(z, 0.0)
            t = jnp.log(1.0 + jnp.exp2(lo - hi)) * LOG2E
            sp = hi + t
            ls = lo - t
            strict = None
            if diag:
                n = rc - r0
                strict = (lax.broadcasted_iota(jnp.int32, (n, tk), 1)
                          < lax.broadcasted_iota(jnp.int32, (n, tk), 0))
                sp = jnp.where(strict, sp, 0.0)
            parts.append((ls, sp[:, 0:1], _dot(sp.astype(BF16), later_keys), strict))
        rests = list(rests)
        for (slot, _, r0, _, _), k0, (ls, sp_first, later, strict) in zip(jobs, starts, parts):
            n = rc - r0
            rest = rests[slot]
            if rest is None:
                rest = jnp.zeros((n, 1), F32)
            elif rest.shape[0] < n:
                rest = jnp.concatenate([jnp.zeros((n - rest.shape[0], 1), F32), rest], axis=0)
            a = jnp.exp2(ls - later - rest)
            if strict is not None:
                a = jnp.where(strict, a, 0.0)
            acc_scr[slot, r0:, :] += _dot(a.astype(BF16), vb_scr[pl.ds(k0, tk), :])
            rests[slot] = rest + later[:, 0:1] + sp_first
        return rests

    def chunk_pair(pi, _):
        base = pl.multiple_of(pi * (2 * rc), 2 * rc)
        acc_scr[...] = jnp.zeros((2, rc, HEAD_DIM), F32)
        jobs = [(slot, base + slot * rc, j * tk, (2 * pi + slot) * own + j, True)
                for slot in range(2) for j in range(own - 1, -1, -1)]
        rests = run(jobs, [None, None])

        for slot in range(2):
            q0 = base + slot * rc

            def alive(state):
                ks, rest = state
                return jnp.logical_and(ks >= 0, jnp.min(rest) < SB_DEAD_BITS)

            def walk(state, slot=slot, q0=q0):
                ks, rest = state
                for t in range(SB_SUBS):
                    rest = run([(slot, q0, 0, ks - t, False)],
                               [rest if s == slot else None for s in range(2)])[slot]
                return ks - SB_SUBS, rest

            lax.while_loop(alive, walk, ((2 * pi + slot) * own - 1, rests[slot]))
            o_ref[pl.ds(q0, rc), :] = _rms(acc_scr[slot], gn_ref[...]).astype(BF16)
        return 0

    lax.fori_loop(0, seq // (2 * rc), chunk_pair, 0)


def sb_attention(proj, head_gain, *, batch, seq, rc=512):
    assert (rc // SB_KEYS) % SB_SUBS == 0
    t = proj.shape[0]
    scale = float(HEAD_DIM ** -0.5) * LOG2E
    gain_col0 = MLA_HEADS
    return pl.pallas_call(
        functools.partial(_sb_attn_body, seq=seq, rc=rc, scale=scale),
        grid=(batch, SB_HEADS),
        in_specs=[
            pl.BlockSpec((seq, HEAD_DIM), lambda b, h: (b, SB_COL0 + h)),
            pl.BlockSpec((seq, HEAD_DIM), lambda b, h: (b, SB_COL0 + SB_HEADS + h)),
            pl.BlockSpec((seq, HEAD_DIM), lambda b, h: (b, SB_COL0 + 2 * SB_HEADS + h)),
            pl.BlockSpec((1, HEAD_DIM), lambda b, h: (0, gain_col0 + h)),
        ],
        out_specs=pl.BlockSpec((seq, HEAD_DIM), lambda b, h: (b, h)),
        out_shape=jax.ShapeDtypeStruct((t, SB_W), BF16),
        scratch_shapes=[pltpu.VMEM((seq, HEAD_DIM), BF16),
                        pltpu.VMEM((seq, HEAD_DIM), BF16),
                        pltpu.VMEM((seq, HEAD_DIM), BF16),
                        pltpu.VMEM((2, rc, HEAD_DIM), F32)],
        compiler_params=_cparams(2),
        name="sb_attention",
    )(proj, proj, proj, head_gain.reshape(1, -1))


def _dil_attn_body(q_ref, k_ref, v_ref, slope_ref, gn_ref, o_ref, *scr, seq, scale):
    blk = DIL_BACK
    npat = len(DIL_PATTERNS)
    lse_scr, out_scr = scr[:npat], scr[npat:]
    row = lax.broadcasted_iota(jnp.int32, (blk, blk), 0)
    col = lax.broadcasted_iota(jnp.int32, (blk, blk), 1)
    dist_cur = (row - col).astype(F32)
    dist_prev = dist_cur + float(blk)
    ok_cur = row >= col
    ok_prev_band = row <= col
    slope = slope_ref[...]

    for p, (window, d) in enumerate(DIL_PATTERNS):
        assert window // d == blk
        nblk = seq // d // blk
        assert nblk & (nblk - 1) == 0
        bias_cur = -(slope * (float(d) * LOG2E)) * dist_cur
        bias_prev = -(slope * (float(d) * LOG2E)) * dist_prev

        def group(gi, _, p=p, d=d, nblk=nblk, bias_cur=bias_cur, bias_prev=bias_prev):
            def rows(idx, back):
                r = lax.shift_right_logical(idx, nblk.bit_length() - 1)
                nb = lax.bitwise_and(idx, nblk - 1)
                b = jnp.maximum(nb - back, 0)
                start = r + b * (d * blk)
                if d == 1:
                    return pl.ds(pl.multiple_of(start, blk), blk)
                return pl.ds(start, blk, stride=d)

            idxs = [gi * DIL_GROUP + g for g in range(DIL_GROUP)]
            k_own = [k_ref[rows(idx, 0), :].astype(BF16) for idx in idxs]
            k_before = [k_ref[rows(idxs[0], 1), :].astype(BF16)] + k_own[:-1]
            scores = []
            for idx, kc, kp in zip(idxs, k_own, k_before):
                q = (q_ref[rows(idx, 0), :] * scale).astype(BF16)
                scores.append((_dot_nt(q, kc), _dot_nt(q, kp)))
            probs = []
            for idx, (s_cur, s_prev) in zip(idxs, scores):
                nb = lax.bitwise_and(idx, nblk - 1)
                s_cur = jnp.where(ok_cur, s_cur + bias_cur, NEG_BIG)
                no_prev = jnp.where(nb > 0, 0.0, NEG_BIG)
                s_prev = jnp.where(ok_prev_band, s_prev + bias_prev + no_prev, NEG_BIG)
                m_blk = jnp.max(jnp.maximum(s_cur, s_prev), axis=-1, keepdims=True)
                p_cur = jnp.exp2(s_cur - m_blk)
                p_prev = jnp.exp2(s_prev - m_blk)
                l_blk = jnp.broadcast_to(jnp.sum(p_cur + p_prev, axis=-1, keepdims=True),
                                         (blk, HEAD_DIM))
                lse_scr[p][rows(idx, 0), :] = m_blk + jnp.log(l_blk) * LOG2E
                probs.append((p_cur.astype(BF16), p_prev.astype(BF16), l_blk))
            v_own = [v_ref[rows(idx, 0), :].astype(BF16) for idx in idxs]
            v_before = [v_ref[rows(idxs[0], 1), :].astype(BF16)] + v_own[:-1]
            for idx, (p_cur, p_prev, l_blk), vc, vp in zip(idxs, probs, v_own, v_before):
                out_scr[p][rows(idx, 0), :] = (_dot(p_cur, vc) + _dot(p_prev, vp)) / l_blk
            return 0

        assert (d * nblk) % DIL_GROUP == 0
        lax.fori_loop(0, d * nblk // DIL_GROUP, group, 0)

    def finish(i, _):
        rs = pl.ds(pl.multiple_of(i * blk, blk), blk)
        lses = [lse_scr[p][rs, :] for p in range(npat)]
        top = functools.reduce(jnp.maximum, lses)
        den = jnp.zeros((blk, HEAD_DIM), F32)
        num = jnp.zeros((blk, HEAD_DIM), F32)
        for p in range(npat):
            w = jnp.exp2(lses[p] - top)
            den = den + w
            num = num + out_scr[p][rs, :] * w
        o_ref[rs, :] = _rms(num / den, gn_ref[...]).astype(BF16)
        return 0

    lax.fori_loop(0, seq // blk, finish, 0, unroll=8)


def dil_attention(proj, slopes, head_gain, *, batch, seq):
    t = proj.shape[0]
    scale = float(HEAD_DIM ** -0.5) * LOG2E
    gain_col0 = MLA_HEADS + SB_HEADS
    return pl.pallas_call(
        functools.partial(_dil_attn_body, seq=seq, scale=scale),
        grid=(batch, DIL_HEADS),
        in_specs=[
            pl.BlockSpec((seq, HEAD_DIM), lambda b, h: (b, DIL_COL0 + h)),
            pl.BlockSpec((seq, HEAD_DIM), lambda b, h: (b, DIL_COL0 + DIL_HEADS + h)),
            pl.BlockSpec((seq, HEAD_DIM), lambda b, h: (b, DIL_COL0 + 2 * DIL_HEADS + h)),
            pl.BlockSpec((None, 1, LANES), lambda b, h: (h, 0, 0)),
            pl.BlockSpec((1, HEAD_DIM), lambda b, h: (0, gain_col0 + h)),
        ],
        out_specs=pl.BlockSpec((seq, HEAD_DIM), lambda b, h: (b, h)),
        out_shape=jax.ShapeDtypeStruct((t, DIL_W), BF16),
        scratch_shapes=[pltpu.VMEM((seq, HEAD_DIM), F32)] * (2 * len(DIL_PATTERNS)),
        compiler_params=_cparams(2),
        name="dil_attention",
    )(proj, proj, proj, slopes, head_gain.reshape(1, -1))


def _outproj_body(x_ref, o1_ref, o2_ref, o3_ref, w1_ref, w2_ref, w3_ref, out_ref):
    acc = _dot(o1_ref[...], w1_ref[...])
    acc = acc + _dot(o2_ref[...], w2_ref[...])
    acc = acc + _dot(o3_ref[...], w3_ref[...])
    out_ref[...] = x_ref[...] + acc


def out_proj(x, o_mla, o_sb, o_dil, w1, w2, w3, layer, *, tm=512):
    t, d = x.shape
    full = lambda a: pl.BlockSpec((None,) + a.shape[1:], lambda i: (layer, 0, 0),
                                  pipeline_mode=pl.Buffered(1))
    rows = lambda a: pl.BlockSpec((tm, a.shape[1]), lambda i: (i, 0))
    return pl.pallas_call(
        _outproj_body,
        grid=(t // tm,),
        in_specs=[rows(x), rows(o_mla), rows(o_sb), rows(o_dil), full(w1), full(w2), full(w3)],
        out_specs=rows(x),
        out_shape=jax.ShapeDtypeStruct((t, d), F32),
        compiler_params=_cparams(1),
        name="out_proj",
    )(x, o_mla, o_sb, o_dil, w1, w2, w3)


def _rope_tables(seq):
    inv = ROPE_THETA ** (-jnp.arange(0, MLA_ROPE, 2, dtype=F32) / MLA_ROPE)
    ang = jnp.arange(seq, dtype=F32)[:, None] * inv[None, :]
    cos, sin = jnp.cos(ang), jnp.sin(ang)
    reps = LANES // MLA_ROPE
    cos_t = jnp.tile(jnp.concatenate([cos, cos], axis=1), (1, reps))
    sin_t = jnp.tile(jnp.concatenate([-sin, sin], axis=1), (1, reps))
    return cos_t, sin_t


def _prep_w_in(w_in):
    w = w_in.astype(BF16)
    z = lambda n: jnp.zeros(w.shape[:-1] + (n,), BF16)
    return jnp.concatenate(
        [w[..., :MLA_IN], z(MLA_IN_PAD - MLA_IN), w[..., MLA_IN:], z(LANES)], axis=-1)


def _prep_w_uq(w):
    r = w.shape[0]
    w = w.reshape(r, MLA_HEADS, MLA_NOPE + MLA_ROPE)
    w = jnp.pad(w, ((0, 0), (0, 0), (0, MLA_QK_PAD - MLA_NOPE - MLA_ROPE)))
    return w.reshape(r, MLA_HEADS * MLA_QK_PAD).astype(BF16)


def _prep_w_ukv(w):
    r = w.shape[0]
    w = w.reshape(r, MLA_HEADS, 2, HEAD_DIM).transpose(0, 2, 1, 3)
    return w.reshape(r, 2 * MLA_HEADS * HEAD_DIM).astype(BF16)


def kernel(x, ffn1_norm, ffn1_w_gu, ffn1_w_down, mix_norm, w_in, mla_q_norm, w_mla_uq, mla_kv_norm, w_mla_ukv, head_out_norm, w_out, ffn2_norm, ffn2_w_gu, ffn2_w_down, final_norm_g):
    batch, seq, d = x.shape
    depth = ffn1_norm.shape[0]
    cos_t, sin_t = _rope_tables(seq)
    slopes = 2.0 ** (-ALIBI_MAX_EXP * jnp.arange(1, DIL_HEADS + 1, dtype=F32) / DIL_HEADS)
    slopes = jnp.broadcast_to(slopes[:, None, None], (DIL_HEADS, 1, LANES))
    w_mla = MLA_HEADS * HEAD_DIM

    w_gu1, w_gu2 = ffn1_w_gu.astype(BF16), ffn2_w_gu.astype(BF16)
    w_dn1, w_dn2 = (0.5 * ffn1_w_down).astype(BF16), (0.5 * ffn2_w_down).astype(BF16)
    w_in_all = _prep_w_in(w_in)
    wo = w_out.astype(BF16)
    wo_mla, wo_sb, wo_dil = wo[:, :w_mla], wo[:, w_mla:w_mla + SB_W], wo[:, w_mla + SB_W:]

    xt = x.reshape(batch * seq, d)
    for l in range(depth):
        xt = ffn_block(xt, ffn1_norm[l], w_gu1, w_dn1, l)

        proj = norm_proj(xt, mix_norm[l], w_in_all, l)
        q, k, v = mla_prep(proj, mla_q_norm[l], mla_kv_norm[l], _prep_w_uq(w_mla_uq[l]),
                           _prep_w_ukv(w_mla_ukv[l]), cos_t, sin_t, seq=seq)
        o_mla = mla_attention(q, k, v, head_out_norm[l], batch=batch, seq=seq)
        o_sb = sb_attention(proj, head_out_norm[l], batch=batch, seq=seq)
        o_dil = dil_attention(proj, slopes, head_out_norm[l], batch=batch, seq=seq)
        xt = out_proj(xt, o_mla, o_sb, o_dil, wo_mla, wo_sb, wo_dil, l)

        xt = ffn_block(xt, ffn2_norm[l], w_gu2, w_dn2, l,
                       final_norm_g if l == depth - 1 else None)
    return xt.reshape(batch, seq, d)
```

```python
import functools

import jax
import jax.numpy as jnp
from jax import lax
from jax.experimental import pallas as pl
from jax.experimental.pallas import tpu as pltpu

F32 = jnp.float32
BF16 = jnp.bfloat16

EPS = 1e-6
HEAD_DIM = 128
LANES = 128
MLA_HEADS = 6
MLA_Q_LORA = 512
MLA_KV_LORA = 256
MLA_NOPE = 128
MLA_ROPE = 64
MLA_QK_PAD = 256
ROPE_THETA = 10000.0
SB_HEADS = 4
DIL_HEADS = 6
DIL_PATTERNS = ((128, 1), (512, 4), (2048, 16))
DIL_BACK = 128
DIL_GROUP = 32
ALIBI_MAX_EXP = 8.0
NEG_BIG = -1e30
LOG2E = 1.4426950408889634
SB_KEYS = 256
FFN_TM = 1024
FFN_ROWS = 512
FFN_TF = 512
SB_SUBS = 1
SB_CHUNKS = 4
SB_DEAD_BITS = 160.0

MLA_IN = MLA_Q_LORA + MLA_KV_LORA + MLA_ROPE
MLA_IN_PAD = 896
SB_W = SB_HEADS * HEAD_DIM
DIL_W = DIL_HEADS * HEAD_DIM
PROJ_HALF = MLA_IN_PAD + 3 * SB_W
assert PROJ_HALF == 3 * DIL_W + LANES
SB_COL0 = MLA_IN_PAD // LANES
DIL_COL0 = PROJ_HALF // LANES

VMEM_LIMIT = 56 * 1024 * 1024
VMEM_LIMIT_FFN = 62 * 1024 * 1024


def _cparams(grid_rank, vmem_limit=VMEM_LIMIT):
    return pltpu.CompilerParams(dimension_semantics=("arbitrary",) * grid_rank,
                                vmem_limit_bytes=vmem_limit)


def _rms(x, g):
    return x * lax.rsqrt(jnp.mean(x * x, axis=-1, keepdims=True) + EPS) * g


def _dot(a, b):
    return jnp.dot(a, b, preferred_element_type=F32)


def _dot_nt(a, b):
    return lax.dot_general(a, b, (((1,), (1,)), ((), ())), preferred_element_type=F32)


def _ffn_body(x_ref, g_ref, wg_ref, wu_ref, wd_ref, *rest, final_norm):
    fg_ref, o_ref, h_scr = rest if final_norm else (None,) + rest

    def hidden_tile(first):
        for r0 in range(0, h_scr.shape[0], FFN_ROWS):
            rows = slice(r0, r0 + FFN_ROWS)
            if first:
                h = _rms(x_ref[rows, :], g_ref[...]).astype(BF16)
                h_scr[rows, :] = h
            else:
                h = h_scr[rows, :]
            g = _dot(h, wg_ref[...])
            u = _dot(h, wu_ref[...])
            a = (g * (1.0 / (1.0 + jnp.exp(-g))) * u).astype(BF16)
            base = x_ref[rows, :] if first else o_ref[rows, :]
            o_ref[rows, :] = base + _dot(a, wd_ref[...])

    @pl.when(pl.program_id(1) == 0)
    def _():
        hidden_tile(True)

    @pl.when(pl.program_id(1) > 0)
    def _():
        hidden_tile(False)

    if final_norm:
        @pl.when(pl.program_id(1) == pl.num_programs(1) - 1)
        def _():
            o_ref[...] = _rms(o_ref[...], fg_ref[...])


def ffn_block(x, gain, w_gu, w_down_half, layer, final_gain=None, *, tm=FFN_TM, tf=FFN_TF):
    t, d = x.shape
    f = w_down_half.shape[1]
    nf = f // tf
    final_norm = final_gain is not None
    row = pl.BlockSpec((1, d), lambda i, j: (0, 0))
    return pl.pallas_call(
        functools.partial(_ffn_body, final_norm=final_norm),
        grid=(t // tm, nf),
        in_specs=[
            pl.BlockSpec((tm, d), lambda i, j: (i, 0)),
            row,
            pl.BlockSpec((None, d, tf), lambda i, j: (layer, 0, j)),
            pl.BlockSpec((None, d, tf), lambda i, j: (layer, 0, j + nf)),
            pl.BlockSpec((None, tf, d), lambda i, j: (layer, j, 0)),
        ] + ([row] if final_norm else []),
        out_specs=pl.BlockSpec((tm, d), lambda i, j: (i, 0)),
        out_shape=jax.ShapeDtypeStruct((t, d), F32),
        scratch_shapes=[pltpu.VMEM((tm, d), BF16)],
        compiler_params=_cparams(2, VMEM_LIMIT_FFN),
        name="ffn_block",
    )(x, gain.reshape(1, d), w_gu, w_gu, w_down_half,
      *([final_gain.reshape(1, d)] if final_norm else []))


def _normproj_body(x_ref, g_ref, w_ref, o_ref):
    for r0 in range(0, x_ref.shape[0], FFN_ROWS):
        rows = slice(r0, r0 + FFN_ROWS)
        h = _rms(x_ref[rows, :], g_ref[...]).astype(BF16)
        o_ref[rows, :] = _dot(h, w_ref[...])


def norm_proj(x, gain, w, layer, *, tm=1024):
    t, d = x.shape
    n = w.shape[2]
    tn = PROJ_HALF
    return pl.pallas_call(
        _normproj_body,
        grid=(n // tn, t // tm),
        in_specs=[
            pl.BlockSpec((tm, d), lambda c, i: (i, 0)),
            pl.BlockSpec((1, d), lambda c, i: (0, 0)),
            pl.BlockSpec((None, d, tn), lambda c, i: (layer, 0, c),
                         pipeline_mode=pl.Buffered(1)),
        ],
        out_specs=pl.BlockSpec((tm, tn), lambda c, i: (i, c)),
        out_shape=jax.ShapeDtypeStruct((t, n), F32),
        compiler_params=_cparams(2),
        name="norm_proj",
    )(x, gain.reshape(1, d), w)


def _mla_prep_body(a_ref, gq_ref, gkv_ref, wuq_ref, wukv_ref, cos_ref, sin_ref,
                   q_ref, k_ref, v_ref, *, scale):
    a = a_ref[...]
    qn = _rms(a[:, :MLA_Q_LORA], gq_ref[...]).astype(BF16)
    q = _dot(qn, wuq_ref[...]) * scale
    kvn = _rms(a[:, MLA_Q_LORA:MLA_Q_LORA + MLA_KV_LORA], gkv_ref[...]).astype(BF16)
    kv = _dot(kvn, wukv_ref[...])

    cos_t = cos_ref[...]
    sin_t = sin_ref[...]
    lane = lax.broadcasted_iota(jnp.int32, cos_t.shape, 1)
    first_half = (lane & (MLA_ROPE // 2)) == 0

    def rope(x):
        swapped = jnp.where(first_half,
                            pltpu.roll(x, LANES - MLA_ROPE // 2, 1),
                            pltpu.roll(x, MLA_ROPE // 2, 1))
        return x * cos_t + swapped * sin_t

    k_rope = rope(a[:, MLA_Q_LORA + MLA_KV_LORA:]).astype(BF16)
    nk = MLA_HEADS * MLA_NOPE
    for h in range(MLA_HEADS):
        c0 = h * MLA_QK_PAD
        q_ref[:, c0:c0 + LANES] = q[:, c0:c0 + LANES].astype(BF16)
        q_ref[:, c0 + LANES:c0 + 2 * LANES] = rope(q[:, c0 + LANES:c0 + 2 * LANES]).astype(BF16)
        k_ref[:, c0:c0 + LANES] = kv[:, h * MLA_NOPE:(h + 1) * MLA_NOPE].astype(BF16)
        k_ref[:, c0 + LANES:c0 + 2 * LANES] = k_rope
    v_ref[...] = kv[:, nk:].astype(BF16)


def mla_prep(proj, gq, gkv, wuq, wukv, cos_t, sin_t, *, seq, tm=1024):
    t = proj.shape[0]
    nseq = seq // tm
    wq = MLA_HEADS * MLA_QK_PAD
    wv = MLA_HEADS * HEAD_DIM
    scale = float((MLA_NOPE + MLA_ROPE) ** -0.5) * LOG2E
    return pl.pallas_call(
        functools.partial(_mla_prep_body, scale=scale),
        grid=(t // tm,),
        in_specs=[
            pl.BlockSpec((tm, MLA_IN_PAD), lambda i: (i, 0)),
            pl.BlockSpec((1, MLA_Q_LORA), lambda i: (0, 0)),
            pl.BlockSpec((1, MLA_KV_LORA), lambda i: (0, 0)),
            pl.BlockSpec((MLA_Q_LORA, wq), lambda i: (0, 0)),
            pl.BlockSpec((MLA_KV_LORA, 2 * wv), lambda i: (0, 0)),
            pl.BlockSpec((tm, LANES), lambda i: (i % nseq, 0)),
            pl.BlockSpec((tm, LANES), lambda i: (i % nseq, 0)),
        ],
        out_specs=[
            pl.BlockSpec((tm, wq), lambda i: (i, 0)),
            pl.BlockSpec((tm, wq), lambda i: (i, 0)),
            pl.BlockSpec((tm, wv), lambda i: (i, 0)),
        ],
        out_shape=[
            jax.ShapeDtypeStruct((t, wq), BF16),
            jax.ShapeDtypeStruct((t, wq), BF16),
            jax.ShapeDtypeStruct((t, wv), BF16),
        ],
        compiler_params=_cparams(1),
        name="mla_prep",
    )(proj, gq.reshape(1, -1), gkv.reshape(1, -1), wuq, wukv, cos_t, sin_t)


def _mla_attn_body(q_ref, k_ref, v_ref, gn_ref, o_ref, acc_scr, *, seq, nc, rc):
    tq = nc * rc
    row = lax.broadcasted_iota(jnp.int32, (rc, rc), 0)
    col = lax.broadcasted_iota(jnp.int32, (rc, rc), 1)
    causal = col <= row

    def run(q0, jobs, states):
        scores = [_dot_nt(q_ref[pl.ds(q0 + c * rc, rc), :], k_ref[pl.ds(k0, rc), :])
                  for c, k0, _ in jobs]
        states = list(states)
        for (c, k0, diag), s in zip(jobs, scores):
            if diag:
                s = jnp.where(causal, s, NEG_BIG)
            m, l = states[c]
            m_new = jnp.maximum(m, jnp.max(s, axis=1, keepdims=True))
            alpha = jnp.exp2(m - m_new)
            p = jnp.exp2(s - m_new)
            l = alpha * l + jnp.sum(p, axis=1, keepdims=True)
            acc_scr[c] = alpha * acc_scr[c] + _dot(p.astype(BF16), v_ref[pl.ds(k0, rc), :])
            states[c] = (m_new, l)
        return tuple(states)

    for qi in range(seq // tq):
        q0 = qi * tq
        for c in range(nc):
            acc_scr[c] = jnp.zeros((rc, HEAD_DIM), F32)
        states = tuple((jnp.full((rc, 1), NEG_BIG, F32), jnp.zeros((rc, 1), F32))
                       for _ in range(nc))
        for i in range(qi):
            states = run(q0, [(c, i * tq + t * rc, False) for t in range(nc) for c in range(nc)],
                         states)
        states = run(q0, [(c, q0 + t * rc, t == c) for t in range(nc) for c in range(t, nc)],
                     states)
        for c, (_, l) in enumerate(states):
            o_ref[pl.ds(q0 + c * rc, rc), :] = _rms(acc_scr[c] / l, gn_ref[...]).astype(BF16)


def mla_attention(q, k, v, head_gain, *, batch, seq, nc=4, rc=512):
    t = q.shape[0]
    return pl.pallas_call(
        functools.partial(_mla_attn_body, seq=seq, nc=nc, rc=rc),
        grid=(batch, MLA_HEADS),
        in_specs=[
            pl.BlockSpec((seq, MLA_QK_PAD), lambda b, h: (b, h)),
            pl.BlockSpec((seq, MLA_QK_PAD), lambda b, h: (b, h)),
            pl.BlockSpec((seq, HEAD_DIM), lambda b, h: (b, h)),
            pl.BlockSpec((1, HEAD_DIM), lambda b, h: (0, h)),
        ],
        out_specs=pl.BlockSpec((seq, HEAD_DIM), lambda b, h: (b, h)),
        out_shape=jax.ShapeDtypeStruct((t, MLA_HEADS * HEAD_DIM), BF16),
        scratch_shapes=[pltpu.VMEM((nc, rc, HEAD_DIM), F32)],
        compiler_params=_cparams(2),
        name="mla_attention",
    )(q, k, v, head_gain.reshape(1, -1))


def _sb_attn_body(q_ref, k_ref, v_ref, gn_ref, o_ref, qb_scr, kb_scr, vb_scr, acc_scr,
                  *, seq, rc, scale):
    tk = SB_KEYS
    own = rc // tk
    qb_scr[...] = (q_ref[...] * scale).astype(BF16)
    kb_scr[...] = k_ref[...].astype(BF16)
    vb_scr[...] = v_ref[...].astype(BF16)
    urow = lax.broadcasted_iota(jnp.int32, (tk, tk), 0)
    ucol = lax.broadcasted_iota(jnp.int32, (tk, tk), 1)
    later_keys = (urow > ucol).astype(BF16)

    def run(jobs, rests):
        starts = [pl.multiple_of(ks * tk, tk) for _, _, _, ks, _ in jobs]
        zs = [_dot_nt(qb_scr[pl.ds(q0 + r0, rc - r0), :], kb_scr[pl.ds(k0, tk), :])
              for (_, q0, r0, _, _), k0 in zip(jobs, starts)]
        parts = []
        for (_, _, r0, _, diag), z in zip(jobs, zs):
            hi = jnp.maximum(z, 0.0)
            lo = jnp.minimum(z, 0.0)
            t = jnp.log(1.0 + jnp.exp2(lo - hi)) * LOG2E
            sp = hi + t
            ls = lo - t
            strict = None
            if diag:
                n = rc - r0
                strict = (lax.broadcasted_iota(jnp.int32, (n, tk), 1)
                          < lax.broadcasted_iota(jnp.int32, (n, tk), 0))
                sp = jnp.where(strict, sp, 0.0)
            parts.append((ls, sp[:, 0:1], _dot(sp.astype(BF16), later_keys), strict))
        rests = list(rests)
        for (slot, _, r0, _, _), k0, (ls, sp_first, later, strict) in zip(jobs, starts, parts):
            n = rc - r0
            rest = rests[slot]
            if rest is None:
                rest = jnp.zeros((n, 1), F32)
            elif rest.shape[0] < n:
                rest = jnp.concatenate([jnp.zeros((n - rest.shape[0], 1), F32), rest], axis=0)
            a = jnp.exp2(ls - later - rest)
            if strict is not None:
                a = jnp.where(strict, a, 0.0)
            acc_scr[slot, r0:, :] += _dot(a.astype(BF16), vb_scr[pl.ds(k0, tk), :])
            rests[slot] = rest + later[:, 0:1] + sp_first
        return rests

    def chunk_group(gi, _):
        ng = SB_CHUNKS
        base = pl.multiple_of(gi * (ng * rc), ng * rc)
        acc_scr[...] = jnp.zeros((ng, rc, HEAD_DIM), F32)
        jobs = [(slot, base + slot * rc, j * tk, (ng * gi + slot) * own + j, True)
                for slot in range(ng) for j in range(own - 1, -1, -1)]
        rests = run(jobs, [None] * ng)

        for slot in range(ng):
            q0 = base + slot * rc

            def alive(state):
                ks, rest = state
                return jnp.logical_and(ks >= 0, jnp.min(rest) < SB_DEAD_BITS)

            def walk(state, slot=slot, q0=q0):
                ks, rest = state
                for t in range(SB_SUBS):
                    rest = run([(slot, q0, 0, ks - t, False)],
                               [rest if s == slot else None for s in range(ng)])[slot]
                return ks - SB_SUBS, rest

            lax.while_loop(alive, walk, ((ng * gi + slot) * own - 1, rests[slot]))
            o_ref[pl.ds(q0, rc), :] = _rms(acc_scr[slot], gn_ref[...]).astype(BF16)
        return 0

    assert seq % (SB_CHUNKS * rc) == 0
    lax.fori_loop(0, seq // (SB_CHUNKS * rc), chunk_group, 0)


def sb_attention(proj, head_gain, *, batch, seq, rc=512):
    assert (rc // SB_KEYS) % SB_SUBS == 0
    t = proj.shape[0]
    scale = float(HEAD_DIM ** -0.5) * LOG2E
    gain_col0 = MLA_HEADS
    return pl.pallas_call(
        functools.partial(_sb_attn_body, seq=seq, rc=rc, scale=scale),
        grid=(batch, SB_HEADS),
        in_specs=[
            pl.BlockSpec((seq, HEAD_DIM), lambda b, h: (b, SB_COL0 + h)),
            pl.BlockSpec((seq, HEAD_DIM), lambda b, h: (b, SB_COL0 + SB_HEADS + h)),
            pl.BlockSpec((seq, HEAD_DIM), lambda b, h: (b, SB_COL0 + 2 * SB_HEADS + h)),
            pl.BlockSpec((1, HEAD_DIM), lambda b, h: (0, gain_col0 + h)),
        ],
        out_specs=pl.BlockSpec((seq, HEAD_DIM), lambda b, h: (b, h)),
        out_shape=jax.ShapeDtypeStruct((t, SB_W), BF16),
        scratch_shapes=[pltpu.VMEM((seq, HEAD_DIM), BF16),
                        pltpu.VMEM((seq, HEAD_DIM), BF16),
                        pltpu.VMEM((seq, HEAD_DIM), BF16),
                        pltpu.VMEM((SB_CHUNKS, rc, HEAD_DIM), F32)],
        compiler_params=_cparams(2),
        name="sb_attention",
    )(proj, proj, proj, head_gain.reshape(1, -1))


def _dil_attn_body(q_ref, k_ref, v_ref, slope_ref, gn_ref, o_ref, *scr, seq, scale):
    blk = DIL_BACK
    npat = len(DIL_PATTERNS)
    lse_scr, out_scr = scr[:npat], scr[npat:]
    row = lax.broadcasted_iota(jnp.int32, (blk, blk), 0)
    col = lax.broadcasted_iota(jnp.int32, (blk, blk), 1)
    dist_cur = (row - col).astype(F32)
    dist_prev = dist_cur + float(blk)
    ok_cur = row >= col
    ok_prev_band = row <= col
    slope = slope_ref[...]

    for p, (window, d) in enumerate(DIL_PATTERNS):
        assert window // d == blk
        nblk = seq // d // blk
        assert nblk & (nblk - 1) == 0
        bias_cur = -(slope * (float(d) * LOG2E)) * dist_cur
        bias_prev = -(slope * (float(d) * LOG2E)) * dist_prev

        def group(gi, _, p=p, d=d, nblk=nblk, bias_cur=bias_cur, bias_prev=bias_prev):
            def rows(idx, back):
                r = lax.shift_right_logical(idx, nblk.bit_length() - 1)
                nb = lax.bitwise_and(idx, nblk - 1)
                b = jnp.maximum(nb - back, 0)
                start = r + b * (d * blk)
                if d == 1:
                    return pl.ds(pl.multiple_of(start, blk), blk)
                return pl.ds(start, blk, stride=d)

            idxs = [gi * DIL_GROUP + g for g in range(DIL_GROUP)]
            k_own = [k_ref[rows(idx, 0), :].astype(BF16) for idx in idxs]
            k_before = [k_ref[rows(idxs[0], 1), :].astype(BF16)] + k_own[:-1]
            scores = []
            for idx, kc, kp in zip(idxs, k_own, k_before):
                q = (q_ref[rows(idx, 0), :] * scale).astype(BF16)
                scores.append((_dot_nt(q, kc), _dot_nt(q, kp)))
            probs = []
            for idx, (s_cur, s_prev) in zip(idxs, scores):
                nb = lax.bitwise_and(idx, nblk - 1)
                s_cur = jnp.where(ok_cur, s_cur + bias_cur, NEG_BIG)
                no_prev = jnp.where(nb > 0, 0.0, NEG_BIG)
                s_prev = jnp.where(ok_prev_band, s_prev + bias_prev + no_prev, NEG_BIG)
                m_blk = jnp.max(jnp.maximum(s_cur, s_prev), axis=-1, keepdims=True)
                p_cur = jnp.exp2(s_cur - m_blk)
                p_prev = jnp.exp2(s_prev - m_blk)
                l_blk = jnp.broadcast_to(jnp.sum(p_cur + p_prev, axis=-1, keepdims=True),
                                         (blk, HEAD_DIM))
                lse_scr[p][rows(idx, 0), :] = m_blk + jnp.log(l_blk) * LOG2E
                probs.append((p_cur.astype(BF16), p_prev.astype(BF16), l_blk))
            v_own = [v_ref[rows(idx, 0), :].astype(BF16) for idx in idxs]
            v_before = [v_ref[rows(idxs[0], 1), :].astype(BF16)] + v_own[:-1]
            for idx, (p_cur, p_prev, l_blk), vc, vp in zip(idxs, probs, v_own, v_before):
                out_scr[p][rows(idx, 0), :] = (_dot(p_cur, vc) + _dot(p_prev, vp)) / l_blk
            return 0

        assert (d * nblk) % DIL_GROUP == 0
        lax.fori_loop(0, d * nblk // DIL_GROUP, group, 0)

    def finish(i, _):
        rs = pl.ds(pl.multiple_of(i * blk, blk), blk)
        lses = [lse_scr[p][rs, :] for p in range(npat)]
        top = functools.reduce(jnp.maximum, lses)
        den = jnp.zeros((blk, HEAD_DIM), F32)
        num = jnp.zeros((blk, HEAD_DIM), F32)
        for p in range(npat):
            w = jnp.exp2(lses[p] - top)
            den = den + w
            num = num + out_scr[p][rs, :] * w
        o_ref[rs, :] = _rms(num / den, gn_ref[...]).astype(BF16)
        return 0

    lax.fori_loop(0, seq // blk, finish, 0, unroll=8)


def dil_attention(proj, slopes, head_gain, *, batch, seq):
    t = proj.shape[0]
    scale = float(HEAD_DIM ** -0.5) * LOG2E
    gain_col0 = MLA_HEADS + SB_HEADS
    return pl.pallas_call(
        functools.partial(_dil_attn_body, seq=seq, scale=scale),
        grid=(batch, DIL_HEADS),
        in_specs=[
            pl.BlockSpec((seq, HEAD_DIM), lambda b, h: (b, DIL_COL0 + h)),
            pl.BlockSpec((seq, HEAD_DIM), lambda b, h: (b, DIL_COL0 + DIL_HEADS + h)),
            pl.BlockSpec((seq, HEAD_DIM), lambda b, h: (b, DIL_COL0 + 2 * DIL_HEADS + h)),
            pl.BlockSpec((None, 1, LANES), lambda b, h: (h, 0, 0)),
            pl.BlockSpec((1, HEAD_DIM), lambda b, h: (0, gain_col0 + h)),
        ],
        out_specs=pl.BlockSpec((seq, HEAD_DIM), lambda b, h: (b, h)),
        out_shape=jax.ShapeDtypeStruct((t, DIL_W), BF16),
        scratch_shapes=[pltpu.VMEM((seq, HEAD_DIM), F32)] * (2 * len(DIL_PATTERNS)),
        compiler_params=_cparams(2),
        name="dil_attention",
    )(proj, proj, proj, slopes, head_gain.reshape(1, -1))


def _outproj_body(x_ref, o1_ref, o2_ref, o3_ref, w1_ref, w2_ref, w3_ref, out_ref):
    acc = _dot(o1_ref[...], w1_ref[...])
    acc = acc + _dot(o2_ref[...], w2_ref[...])
    acc = acc + _dot(o3_ref[...], w3_ref[...])
    out_ref[...] = x_ref[...] + acc


def out_proj(x, o_mla, o_sb, o_dil, w1, w2, w3, layer, *, tm=512):
    t, d = x.shape
    full = lambda a: pl.BlockSpec((None,) + a.shape[1:], lambda i: (layer, 0, 0),
                                  pipeline_mode=pl.Buffered(1))
    rows = lambda a: pl.BlockSpec((tm, a.shape[1]), lambda i: (i, 0))
    return pl.pallas_call(
        _outproj_body,
        grid=(t // tm,),
        in_specs=[rows(x), rows(o_mla), rows(o_sb), rows(o_dil), full(w1), full(w2), full(w3)],
        out_specs=rows(x),
        out_shape=jax.ShapeDtypeStruct((t, d), F32),
        compiler_params=_cparams(1),
        name="out_proj",
    )(x, o_mla, o_sb, o_dil, w1, w2, w3)


def _rope_tables(seq):
    inv = ROPE_THETA ** (-jnp.arange(0, MLA_ROPE, 2, dtype=F32) / MLA_ROPE)
    ang = jnp.arange(seq, dtype=F32)[:, None] * inv[None, :]
    cos, sin = jnp.cos(ang), jnp.sin(ang)
    reps = LANES // MLA_ROPE
    cos_t = jnp.tile(jnp.concatenate([cos, cos], axis=1), (1, reps))
    sin_t = jnp.tile(jnp.concatenate([-sin, sin], axis=1), (1, reps))
    return cos_t, sin_t


def _prep_w_in(w_in):
    w = w_in.astype(BF16)
    z = lambda n: jnp.zeros(w.shape[:-1] + (n,), BF16)
    return jnp.concatenate(
        [w[..., :MLA_IN], z(MLA_IN_PAD - MLA_IN), w[..., MLA_IN:], z(LANES)], axis=-1)


def _prep_w_uq(w):
    r = w.shape[0]
    w = w.reshape(r, MLA_HEADS, MLA_NOPE + MLA_ROPE)
    w = jnp.pad(w, ((0, 0), (0, 0), (0, MLA_QK_PAD - MLA_NOPE - MLA_ROPE)))
    return w.reshape(r, MLA_HEADS * MLA_QK_PAD).astype(BF16)


def _prep_w_ukv(w):
    r = w.shape[0]
    w = w.reshape(r, MLA_HEADS, 2, HEAD_DIM).transpose(0, 2, 1, 3)
    return w.reshape(r, 2 * MLA_HEADS * HEAD_DIM).astype(BF16)


def kernel(x, ffn1_norm, ffn1_w_gu, ffn1_w_down, mix_norm, w_in, mla_q_norm, w_mla_uq, mla_kv_norm, w_mla_ukv, head_out_norm, w_out, ffn2_norm, ffn2_w_gu, ffn2_w_down, final_norm_g):
    batch, seq, d = x.shape
    depth = ffn1_norm.shape[0]
    cos_t, sin_t = _rope_tables(seq)
    slopes = 2.0 ** (-ALIBI_MAX_EXP * jnp.arange(1, DIL_HEADS + 1, dtype=F32) / DIL_HEADS)
    slopes = jnp.broadcast_to(slopes[:, None, None], (DIL_HEADS, 1, LANES))
    w_mla = MLA_HEADS * HEAD_DIM

    w_gu1, w_gu2 = ffn1_w_gu.astype(BF16), ffn2_w_gu.astype(BF16)
    w_dn1, w_dn2 = (0.5 * ffn1_w_down).astype(BF16), (0.5 * ffn2_w_down).astype(BF16)
    w_in_all = _prep_w_in(w_in)
    wo = w_out.astype(BF16)
    wo_mla, wo_sb, wo_dil = wo[:, :w_mla], wo[:, w_mla:w_mla + SB_W], wo[:, w_mla + SB_W:]

    xt = x.reshape(batch * seq, d)
    for l in range(depth):
        xt = ffn_block(xt, ffn1_norm[l], w_gu1, w_dn1, l)

        proj = norm_proj(xt, mix_norm[l], w_in_all, l)
        q, k, v = mla_prep(proj, mla_q_norm[l], mla_kv_norm[l], _prep_w_uq(w_mla_uq[l]),
                           _prep_w_ukv(w_mla_ukv[l]), cos_t, sin_t, seq=seq)
        o_mla = mla_attention(q, k, v, head_out_norm[l], batch=batch, seq=seq)
        o_sb = sb_attention(proj, head_out_norm[l], batch=batch, seq=seq)
        o_dil = dil_attention(proj, slopes, head_out_norm[l], batch=batch, seq=seq)
        xt = out_proj(xt, o_mla, o_sb, o_dil, wo_mla, wo_sb, wo_dil, l)

        xt = ffn_block(xt, ffn2_norm[l], w_gu2, w_dn2, l,
                       final_norm_g if l == depth - 1 else None)
    return xt.reshape(batch, seq, d)
```
